```python
import math
import jax, jax.numpy as jnp
from jax import lax
import numpy as np

D_MODEL = 1024
BATCH = 16
SEQ = 2048
DEPTH = 2

CHUNK = 64
CONV_CH = 512
CONV_WIDTH = 31
N_HEADS = 8
HEAD_DIM = 64
N_KV = 2
ATTN_W = N_HEADS * HEAD_DIM
IDX_HEADS = 8
IDX_DIM = 32
TOPK_MAX = 256
Q_BLOCK = 128
REL_BUCKETS = 32
REL_MAX_DIST = 128
N_GROUPS = 4
EXPERTS_PER_GROUP = 8
N_EXPERTS = N_GROUPS * EXPERTS_PER_GROUP
EXPERT_FF = 256
TOP_IN_GROUP = 2
EPS = 1e-6

SPLITS = (2 * CONV_CH, ATTN_W, N_KV * HEAD_DIM, N_KV * HEAD_DIM,
          IDX_HEADS * IDX_DIM, IDX_DIM, IDX_HEADS, D_MODEL, D_MODEL)
IN_COLS = 2 * CONV_CH + ATTN_W + 2 * N_KV * HEAD_DIM + IDX_HEADS * IDX_DIM + IDX_DIM + IDX_HEADS + 2 * D_MODEL

kernel_name = "hybrid_conv_dsa_hmoe_block"


def rms_norm(x, g):
    xf = x.astype(jnp.float32)
    y = xf * lax.rsqrt(jnp.mean(xf * xf, axis=-1, keepdims=True) + EPS)
    return (y * g.astype(jnp.float32)).astype(x.dtype)


def layer_norm(x, g, b):
    xf = x.astype(jnp.float32)
    mu = jnp.mean(xf, axis=-1, keepdims=True)
    var = jnp.mean(jnp.square(xf - mu), axis=-1, keepdims=True)
    y = (xf - mu) * lax.rsqrt(var + EPS) * g.astype(jnp.float32) + b.astype(jnp.float32)
    return y.astype(x.dtype)


def rel_bucket(rel):
    nb = REL_BUCKETS // 2
    max_exact = nb // 2
    ret = jnp.where(rel < 0, nb, 0)
    n = jnp.abs(rel)
    nf = jnp.maximum(n, 1).astype(jnp.float32)
    large = max_exact + (jnp.log(nf / max_exact) / math.log(REL_MAX_DIST / max_exact)
                         * (nb - max_exact)).astype(jnp.int32)
    large = jnp.minimum(large, nb - 1)
    return ret + jnp.where(n < max_exact, n, large)


def conv_branch(u, conv_w, conv_b, ln_g, ln_b, w_o):
    a, gate = jnp.split(u, 2, axis=-1)
    v = a * jax.nn.sigmoid(gate)
    v = lax.conv_general_dilated(
        v, conv_w[:, None, :], window_strides=(1,),
        padding=[(CONV_WIDTH - 1, 0)],
        dimension_numbers=('NWC', 'WIO', 'NWC'),
        feature_group_count=CONV_CH) + conv_b
    v = jax.nn.silu(layer_norm(v, ln_g, ln_b))
    return v @ w_o


def sparse_attention(q, k, v, qi, ki, wi, rel_bias):
    B, L = q.shape[0], q.shape[1]
    topk = min(TOPK_MAX, L // 4)
    nblk = L // Q_BLOCK
    R = N_HEADS // N_KV
    key_pos = jnp.arange(L)
    gather = jax.vmap(lambda a, i: a[i])

    def to_blocks(a):
        return jnp.moveaxis(a.reshape((B, nblk, Q_BLOCK) + a.shape[2:]), 1, 0)

    def block(args):
        blk, qb, qib, wib = args
        t = blk * Q_BLOCK + jnp.arange(Q_BLOCK)
        dots = jnp.einsum('bqhd,bsd->bqhs', qib, ki).astype(jnp.float32) * (IDX_DIM ** -0.5)
        score = jnp.einsum('bqhs,bqh->bqs', jax.nn.relu(dots),
                           wib.astype(jnp.float32)) * (IDX_HEADS ** -0.5)
        limit = (t // CHUNK + 1) * CHUNK
        adm = key_pos[None, :] < limit[:, None]
        score = jnp.where(adm[None], score, -jnp.inf)
        vals, idx = lax.top_k(score, topk)
        valid = jnp.isfinite(vals)
        k_sel = gather(k, idx)
        v_sel = gather(v, idx)
        bias = rel_bias[rel_bucket(t[None, :, None] - idx)]
        bias = jnp.moveaxis(bias.reshape(B, Q_BLOCK, topk, N_KV, R), 2, -1)
        qg = qb.reshape(B, Q_BLOCK, N_KV, R, HEAD_DIM)
        logits = (jnp.einsum('bqgrd,bqkgd->bqgrk', qg, k_sel).astype(jnp.float32)
                  * (HEAD_DIM ** -0.5) + bias.astype(jnp.float32))
        logits = jnp.where(valid[:, :, None, None, :], logits, -jnp.inf)
        p = jax.nn.softmax(logits, axis=-1).astype(v.dtype)
        o = jnp.einsum('bqgrk,bqkgd->bqgrd', p, v_sel)
        return o.reshape(B, Q_BLOCK, ATTN_W)

    out = lax.map(block, (jnp.arange(nblk), to_blocks(q), to_blocks(qi), to_blocks(wi)))
    return jnp.moveaxis(out, 0, 1).reshape(B, L, ATTN_W)


def mixer(h, w_in, conv_w, conv_b, ln_g, ln_b, w_conv_out, w_attn_out, w_out, rel_bias):
    B, L, _ = h.shape
    proj = h @ w_in
    offs = [int(o) for o in np.cumsum(SPLITS)[:-1]]
    u_conv, q, k, v, qi, ki, wi, gc, ga = jnp.split(proj, offs, axis=-1)
    y_conv = conv_branch(u_conv, conv_w, conv_b, ln_g, ln_b, w_conv_out)
    y_attn = sparse_attention(
        q.reshape(B, L, N_HEADS, HEAD_DIM), k.reshape(B, L, N_KV, HEAD_DIM),
        v.reshape(B, L, N_KV, HEAD_DIM), qi.reshape(B, L, IDX_HEADS, IDX_DIM),
        ki, wi, rel_bias) @ w_attn_out
    merged = jax.nn.sigmoid(gc) * y_conv + jax.nn.sigmoid(ga) * y_attn
    return merged @ w_out


def hier_moe(h, w_gr, b_gr, w_er, b_er, w_gate, w_up, w_down):
    B, L, D = h.shape
    ht = h.reshape(-1, D)
    g_logits = (ht @ w_gr + b_gr).astype(jnp.float32)
    g_prob = jax.nn.softmax(g_logits, axis=-1)
    grp = jnp.argmax(g_logits, axis=-1)
    p_grp = jnp.take_along_axis(g_prob, grp[:, None], axis=-1)
    e_logits = (jnp.einsum('td,gde->tge', ht, w_er) + b_er).astype(jnp.float32)
    e_logits = jnp.take_along_axis(e_logits, grp[:, None, None], axis=1)[:, 0]
    top_vals, top_idx = lax.top_k(e_logits, TOP_IN_GROUP)
    p_in = jax.nn.softmax(top_vals, axis=-1) * p_grp
    in_w = jnp.einsum('tj,tje->te', p_in,
                      jax.nn.one_hot(top_idx, EXPERTS_PER_GROUP, dtype=jnp.float32))
    comb = jax.nn.one_hot(grp, N_GROUPS, dtype=jnp.float32)[:, :, None] * in_w[:, None, :]
    comb = jnp.moveaxis(comb, 1, 0).astype(h.dtype)

    def group_ffn(args):
        wg, wu, wd, c = args
        a = jnp.einsum('td,edf->tef', ht, wg)
        b = jnp.einsum('td,edf->tef', ht, wu)
        return jnp.einsum('tef,efd->td', jax.nn.silu(a) * b * c[:, :, None], wd)

    shp_in = (N_GROUPS, EXPERTS_PER_GROUP, D, EXPERT_FF)
    shp_out = (N_GROUPS, EXPERTS_PER_GROUP, EXPERT_FF, D)
    out = lax.map(group_ffn, (w_gate.reshape(shp_in), w_up.reshape(shp_in),
                              w_down.reshape(shp_out), comb)).sum(axis=0)
    return out.reshape(B, L, D)


def setup_inputs(seed: int = 0) -> dict:
    key = jax.random.key(seed)
    ks = jax.random.split(key, 24)
    nrm = lambda k, shape, scale: jax.random.normal(k, shape, jnp.float32) * scale
    D = D_MODEL
    return {
        "x": nrm(ks[0], (BATCH, SEQ, D), 1.0),
        "g_mix": 1.0 + nrm(ks[1], (DEPTH, D), 0.01),
        "w_in": nrm(ks[2], (DEPTH, D, IN_COLS), D ** -0.5),
        "conv_w": nrm(ks[3], (DEPTH, CONV_WIDTH, CONV_CH), CONV_WIDTH ** -0.5),
        "conv_b": nrm(ks[4], (DEPTH, CONV_CH), 0.01),
        "conv_ln_g": 1.0 + nrm(ks[5], (DEPTH, CONV_CH), 0.01),
        "conv_ln_b": nrm(ks[6], (DEPTH, CONV_CH), 0.01),
        "w_conv_out": nrm(ks[7], (DEPTH, CONV_CH, D), CONV_CH ** -0.5),
        "w_attn_out": nrm(ks[8], (DEPTH, ATTN_W, D), ATTN_W ** -0.5),
        "w_out": nrm(ks[9], (DEPTH, D, D), D ** -0.5),
        "rel_bias": nrm(ks[10], (REL_BUCKETS, N_HEADS), 0.5),
        "g_ffn": 1.0 + nrm(ks[11], (DEPTH, D), 0.01),
        "w_group_router": nrm(ks[12], (DEPTH, D, N_GROUPS), D ** -0.5),
        "b_group_router": nrm(ks[13], (DEPTH, N_GROUPS), 0.01),
        "w_expert_router": nrm(ks[14], (DEPTH, N_GROUPS, D, EXPERTS_PER_GROUP), D ** -0.5),
        "b_expert_router": nrm(ks[15], (DEPTH, N_GROUPS, EXPERTS_PER_GROUP), 0.01),
        "w_e_gate": nrm(ks[16], (DEPTH, N_EXPERTS, D, EXPERT_FF), D ** -0.5),
        "w_e_up": nrm(ks[17], (DEPTH, N_EXPERTS, D, EXPERT_FF), D ** -0.5),
        "w_e_down": nrm(ks[18], (DEPTH, N_EXPERTS, EXPERT_FF, D), EXPERT_FF ** -0.5),
        "g_final": 1.0 + nrm(ks[19], (D,), 0.01),
    }


def reference(x, g_mix, w_in, conv_w, conv_b, conv_ln_g, conv_ln_b, w_conv_out, w_attn_out,
              w_out, rel_bias, g_ffn, w_group_router, b_group_router, w_expert_router,
              b_expert_router, w_e_gate, w_e_up, w_e_down, g_final):
    for l in range(DEPTH):
        h = rms_norm(x, g_mix[l])
        x = x + mixer(h, w_in[l], conv_w[l], conv_b[l], conv_ln_g[l], conv_ln_b[l],
                      w_conv_out[l], w_attn_out[l], w_out[l], rel_bias)
        h = rms_norm(x, g_ffn[l])
        x = x + hier_moe(h, w_group_router[l], b_group_router[l], w_expert_router[l],
                         b_expert_router[l], w_e_gate[l], w_e_up[l], w_e_down[l])
    return rms_norm(x, g_final)
```

```python
import functools
import math

import jax
import jax.numpy as jnp
import numpy as np
from jax import lax
from jax.experimental import pallas as pl
from jax.experimental.pallas import tpu as pltpu

CHUNK = 64
CONV_CH = 512
CONV_WIDTH = 31
N_HEADS = 8
HEAD_DIM = 64
N_KV = 2
KV_REP = N_HEADS // N_KV
ATTN_W = N_HEADS * HEAD_DIM
IDX_HEADS = 8
IDX_DIM = 32
TOPK_MAX = 256
Q_BLOCK = 128
REL_BUCKETS = 32
REL_MAX_DIST = 128
N_GROUPS = 4
EXPERTS_PER_GROUP = 8
N_EXPERTS = N_GROUPS * EXPERTS_PER_GROUP
EXPERT_FF = 256
EPS = 1e-6

LANES = 128
SUBLANES = 8
VMEM_LIMIT_BYTES = 56 * 1024 * 1024

NEG_BIG = -1e30
INT_MIN = -(2 ** 31)
KEY_MIN_FINITE = -0x7F800000
KEY_MAX_FINITE = 0x7F7FFFFF

BF16 = jnp.bfloat16
F32 = jnp.float32

PROJ_TM = 512
CONV_TL = 256
CONV_HALO = 32
MERGE_TM = 512
MOE_TM = 1024
MOE_EB = 4
NORM_TM = 1024

ROUTER_E_OFF = 32


def _cparams(*sem):
    return pltpu.CompilerParams(dimension_semantics=sem, vmem_limit_bytes=VMEM_LIMIT_BYTES)


def _sigmoid(x):
    return 1.0 / (1.0 + jnp.exp(-x))


def _dot(a, b):
    return jnp.dot(a, b, preferred_element_type=F32)


def _dot_nt(a, b):
    return lax.dot_general(a, b, (((1,), (1,)), ((), ())), preferred_element_type=F32)


SEG_U = (0, 2 * CONV_CH)
SEG_Q = (SEG_U[1], SEG_U[1] + N_HEADS * LANES)
SEG_K = (SEG_Q[1], SEG_Q[1] + LANES)
SEG_V = (SEG_K[1], SEG_K[1] + LANES)
SEG_QI = (SEG_V[1], SEG_V[1] + IDX_HEADS * IDX_DIM)
SEG_KI = (SEG_QI[1], SEG_QI[1] + LANES)
SEG_WI = (SEG_KI[1], SEG_KI[1] + LANES)
SEG_GC = (SEG_WI[1], SEG_WI[1] + 1024)
SEG_GA = (SEG_GC[1], SEG_GC[1] + 1024)
PROJ_COLS = SEG_GA[1]


def _proj_kernel(x_ref, g_ref, w_ref, u_ref, q_ref, k_ref, vt_ref, qi_ref, ki_ref, wit_ref,
                 gc_ref, ga_ref):
    x = x_ref[...]
    h = x * lax.rsqrt(jnp.mean(x * x, axis=-1, keepdims=True) + EPS) * g_ref[...]
    hb = h.astype(BF16)

    def seg(s):
        return _dot(hb, w_ref[:, s[0]:s[1]])

    u_ref[...] = seg(SEG_U).astype(BF16)
    pq = seg(SEG_Q)
    for hh in range(N_HEADS):
        q_ref[hh] = pq[:, hh * LANES:(hh + 1) * LANES].astype(BF16)
    k_ref[...] = seg(SEG_K).astype(BF16)
    pv = seg(SEG_V)
    tm = pv.shape[0]
    for c in range(tm // LANES):
        vt_ref[c] = pv[c * LANES:(c + 1) * LANES, :].T.astype(BF16)
    pqi = seg(SEG_QI)
    for hh in range(IDX_HEADS):
        qi_ref[hh] = pqi[:, hh * IDX_DIM:(hh + 1) * IDX_DIM].astype(BF16)
    ki_ref[...] = seg(SEG_KI)[:, :IDX_DIM].astype(BF16)
    pwi = seg(SEG_WI)
    for c in range(tm // LANES):
        wit_ref[:, c * LANES:(c + 1) * LANES] = pwi[c * LANES:(c + 1) * LANES, :].T[:IDX_HEADS, :]
    gc_ref[...] = _sigmoid(seg(SEG_GC)).astype(BF16)
    ga_ref[...] = _sigmoid(seg(SEG_GA)).astype(BF16)


def _proj_call(x2, g, w_pack):
    t, d = x2.shape
    tm = PROJ_TM
    nq = t // Q_BLOCK
    row = lambda i: (i, 0)
    out_shape = (
        jax.ShapeDtypeStruct((t, 2 * CONV_CH), BF16),
        jax.ShapeDtypeStruct((N_HEADS, t, LANES), BF16),
        jax.ShapeDtypeStruct((t, LANES), BF16),
        jax.ShapeDtypeStruct((nq, LANES, Q_BLOCK), BF16),
        jax.ShapeDtypeStruct((IDX_HEADS, t, IDX_DIM), BF16),
        jax.ShapeDtypeStruct((t, IDX_DIM), BF16),
        jax.ShapeDtypeStruct((IDX_HEADS, t), F32),
        jax.ShapeDtypeStruct((t, d), BF16),
        jax.ShapeDtypeStruct((t, d), BF16),
    )
    out_specs = (
        pl.BlockSpec((tm, 2 * CONV_CH), row),
        pl.BlockSpec((N_HEADS, tm, LANES), lambda i: (0, i, 0)),
        pl.BlockSpec((tm, LANES), row),
        pl.BlockSpec((tm // Q_BLOCK, LANES, Q_BLOCK), lambda i: (i, 0, 0)),
        pl.BlockSpec((IDX_HEADS, tm, IDX_DIM), lambda i: (0, i, 0)),
        pl.BlockSpec((tm, IDX_DIM), row),
        pl.BlockSpec((IDX_HEADS, tm), lambda i: (0, i)),
        pl.BlockSpec((tm, d), row),
        pl.BlockSpec((tm, d), row),
    )
    return pl.pallas_call(
        _proj_kernel,
        grid=(t // tm,),
        in_specs=[
            pl.BlockSpec((tm, d), row),
            pl.BlockSpec((1, d), lambda i: (0, 0)),
            pl.BlockSpec((d, PROJ_COLS), lambda i: (0, 0)),
        ],
        out_specs=out_specs,
        out_shape=out_shape,
        compiler_params=_cparams("parallel"),
        name="proj",
    )(x2, g, w_pack)


def _pack_proj_weights(w_in):
    d = w_in.shape[0]
    splits = (2 * CONV_CH, ATTN_W, N_KV * HEAD_DIM, N_KV * HEAD_DIM, IDX_HEADS * IDX_DIM,
              IDX_DIM, IDX_HEADS, d, d)
    offs = np.concatenate([[0], np.cumsum(splits)])
    w_u, w_q, w_k, w_v, w_qi, w_ki, w_wi, w_gc, w_ga = [
        w_in[:, int(offs[i]):int(offs[i + 1])] for i in range(len(splits))]
    w_q = (w_q * (HEAD_DIM ** -0.5)).reshape(d, N_KV, KV_REP, HEAD_DIM)
    q_pad = jnp.zeros((d, N_KV, KV_REP, N_KV, HEAD_DIM), w_in.dtype)
    for gidx in range(N_KV):
        q_pad = q_pad.at[:, gidx, :, gidx, :].set(w_q[:, gidx])
    q_pad = q_pad.reshape(d, N_HEADS * LANES)
    pad = lambda w: jnp.pad(w, ((0, 0), (0, LANES - w.shape[1])))
    packed = jnp.concatenate([w_u, q_pad, w_k, w_v, w_qi, pad(w_ki), pad(w_wi), w_gc, w_ga], axis=1)
    assert packed.shape[1] == PROJ_COLS
    return packed.astype(BF16)


def _conv_kernel(u_ref, halo_ref, cw_ref, cb_ref, lg_ref, lb_ref, o_ref, ext_ref, y_ref):
    i = pl.program_id(1)
    tl = u_ref.shape[1]

    def glu(u):
        u = u.astype(F32)
        return u[:, :CONV_CH] * _sigmoid(u[:, CONV_CH:])

    halo = glu(halo_ref[0])
    ext_ref[0:CONV_HALO, :] = jnp.where(i > 0, halo, 0.0)
    ext_ref[CONV_HALO:, :] = glu(u_ref[0])
    first = CONV_HALO - (CONV_WIDTH - 1)
    for c in range(CONV_CH // LANES):
        cs = slice(c * LANES, (c + 1) * LANES)
        acc = jnp.zeros((tl, LANES), F32)
        for j in range(CONV_WIDTH):
            acc = acc + cw_ref[j:j + 1, cs] * ext_ref[first + j:first + j + tl, cs]
        y_ref[:, cs] = acc + cb_ref[:, cs]
    y = y_ref[...]
    mu = jnp.mean(y, axis=-1, keepdims=True)
    yc = y - mu
    var = jnp.mean(yc * yc, axis=-1, keepdims=True)
    yn = yc * lax.rsqrt(var + EPS) * lg_ref[...] + lb_ref[...]
    o_ref[0] = (yn * _sigmoid(yn)).astype(BF16)


def _conv_call(u3, conv_w, conv_b, ln_g, ln_b):
    b, l, _ = u3.shape
    tl = CONV_TL
    halo_blocks = tl // CONV_HALO
    const = lambda bi, i: (0, 0)
    return pl.pallas_call(
        _conv_kernel,
        grid=(b, l // tl),
        in_specs=[
            pl.BlockSpec((1, tl, 2 * CONV_CH), lambda bi, i: (bi, i, 0)),
            pl.BlockSpec((1, CONV_HALO, 2 * CONV_CH),
                         lambda bi, i: (bi, jnp.maximum(i * halo_blocks - 1, 0), 0)),
            pl.BlockSpec((CONV_WIDTH, CONV_CH), const),
            pl.BlockSpec((1, CONV_CH), const),
            pl.BlockSpec((1, CONV_CH), const),
            pl.BlockSpec((1, CONV_CH), const),
        ],
        out_specs=pl.BlockSpec((1, tl, CONV_CH), lambda bi, i: (bi, i, 0)),
        out_shape=jax.ShapeDtypeStruct((b, l, CONV_CH), BF16),
        scratch_shapes=[pltpu.VMEM((CONV_HALO + tl, CONV_CH), F32), pltpu.VMEM((tl, CONV_CH), F32)],
        compiler_params=_cparams("parallel", "parallel"),
        name="conv",
    )(u3, u3, conv_w, conv_b, ln_g, ln_b)


def _sortable_key(score):
    b = lax.bitcast_convert_type(score, jnp.int32)
    return b ^ ((b >> 31) & 0x7FFFFFFF)


def _attn_kernel(q_ref, qi_ref, wit_ref, k_ref, vt_ref, ki_ref, bias_ref, o_ref,
                 key_ref, madd_ref, thr_ref, acc_ref, m_ref, l_ref, *, topk):
    i = pl.program_id(1)
    nkb = i + 1
    qb = Q_BLOCK

    qi_st = qi_ref[...].reshape(IDX_HEADS * qb, IDX_DIM)
    wit = wit_ref[...]
    s_loc = lax.broadcasted_iota(jnp.int32, (qb, qb), 0)
    t_loc = lax.broadcasted_iota(jnp.int32, (qb, qb), 1)
    diag_adm = s_loc < (t_loc // CHUNK + 1) * CHUNK
    idx_scale = (IDX_DIM ** -0.5) * (IDX_HEADS ** -0.5)

    def score_body(jb, carry):
        rows = pl.ds(pl.multiple_of(jb * qb, qb), qb)
        dots = _dot_nt(ki_ref[rows, :], qi_st)
        sc = jnp.zeros((qb, qb), F32)
        for hh in range(IDX_HEADS):
            sc = sc + jnp.maximum(dots[:, hh * qb:(hh + 1) * qb], 0.0) * wit[hh:hh + 1, :]
        key = _sortable_key(sc * idx_scale)
        adm = jnp.logical_or(jb < i, diag_adm)
        key_ref[rows, :] = jnp.where(adm, key, INT_MIN)
        return carry

    lax.fori_loop(0, nkb, score_body, 0)

    def count_keys(pred):
        def body(jb, acc):
            rows = pl.ds(pl.multiple_of(jb * qb, qb), qb)
            ind = jnp.where(pred(key_ref[rows, :]), 1, 0)
            return acc + jnp.sum(ind.reshape(qb // SUBLANES, SUBLANES, qb), axis=0)
        acc = lax.fori_loop(0, nkb, body, jnp.zeros((SUBLANES, qb), jnp.int32))
        return jnp.sum(acc, axis=0, keepdims=True)

    def count_ge(cand):
        return count_keys(lambda key: key >= cand)

    thr_ref[...] = jnp.full(thr_ref.shape, KEY_MIN_FINITE, jnp.int32)

    @pl.when(nkb * qb > topk)
    def _():
        def bit_body(it, lo):
            cand = lo + jnp.left_shift(jnp.int32(1), 31 - it)
            return jnp.where(count_ge(cand) >= topk, cand, lo)
        lo = lax.fori_loop(0, 32, bit_body, jnp.full((1, qb), INT_MIN, jnp.int32))
        thr_ref[0:1, :] = jnp.maximum(lo, KEY_MIN_FINITE)

    thr = thr_ref[0:1, :]
    n_ge = count_ge(thr)
    has_ties = jnp.max(n_ge) > topk

    @pl.when(jnp.logical_not(has_ties))
    def _():
        def body(jb, carry):
            rows = pl.ds(pl.multiple_of(jb * qb, qb), qb)
            key = key_ref[rows, :]
            sel = jnp.logical_and(key >= thr, key <= KEY_MAX_FINITE)
            madd_ref[rows, :] = jnp.where(sel, 0.0, NEG_BIG)
            return carry
        lax.fori_loop(0, nkb, body, 0)

    @pl.when(has_ties)
    def _():
        n_gt = count_keys(lambda key: key > thr)
        need = (topk - n_gt).astype(F32)
        lower = (lax.broadcasted_iota(jnp.int32, (qb, qb), 1)
                 < lax.broadcasted_iota(jnp.int32, (qb, qb), 0)).astype(BF16)

        def body(jb, seen):
            rows = pl.ds(pl.multiple_of(jb * qb, qb), qb)
            key = key_ref[rows, :]
            eq = key == thr
            eqf = jnp.where(eq, 1.0, 0.0)
            before = _dot(lower, eqf.astype(BF16)) + seen
            sel = jnp.logical_or(key > thr, jnp.logical_and(eq, before < need))
            sel = jnp.logical_and(sel, key <= KEY_MAX_FINITE)
            madd_ref[rows, :] = jnp.where(sel, 0.0, NEG_BIG)
            return seen + jnp.sum(eqf, axis=0, keepdims=True)
        lax.fori_loop(0, nkb, body, jnp.zeros((1, qb), F32))

    m_ref[...] = jnp.full(m_ref.shape, NEG_BIG, F32)
    l_ref[...] = jnp.zeros(l_ref.shape, F32)
    acc_ref[...] = jnp.zeros(acc_ref.shape, F32)

    def attn_body(jb, carry):
        rows = pl.ds(pl.multiple_of(jb * qb, qb), qb)
        kb = k_ref[rows, :]
        vtb = vt_ref[jb]
        madd = madd_ref[rows, :]
        dsel = jnp.minimum(i - jb, 2)
        for g in range(N_KV):
            qg = q_ref[g * KV_REP:(g + 1) * KV_REP].reshape(KV_REP * qb, LANES)
            lg = _dot_nt(kb, qg)
            ps, alphas = [], []
            for r in range(KV_REP):
                hh = g * KV_REP + r
                x = lg[:, r * qb:(r + 1) * qb] + bias_ref[dsel, hh] + madd
                m_old = m_ref[hh:hh + 1, :]
                m_new = jnp.maximum(m_old, jnp.max(x, axis=0, keepdims=True))
                alpha = jnp.exp(m_old - m_new)
                p = jnp.exp(x - m_new)
                l_ref[hh:hh + 1, :] = alpha * l_ref[hh:hh + 1, :] + jnp.sum(p, axis=0, keepdims=True)
                m_ref[hh:hh + 1, :] = m_new
                ps.append(p.astype(BF16))
                alphas.append(alpha)
            pv = _dot(vtb[g * HEAD_DIM:(g + 1) * HEAD_DIM, :], jnp.concatenate(ps, axis=1))
            acc_ref[g] = acc_ref[g] * jnp.concatenate(alphas, axis=1) + pv
        return carry

    lax.fori_loop(0, nkb, attn_body, 0)

    inv_l = 1.0 / l_ref[...]
    for hp in range(N_HEADS // 2):
        parts = []
        for hh in (2 * hp, 2 * hp + 1):
            g, r = hh // KV_REP, hh % KV_REP
            parts.append(acc_ref[g][:, r * qb:(r + 1) * qb] * inv_l[hh:hh + 1, :])
        st = jnp.concatenate(parts, axis=0)
        o_ref[:, hp * LANES:(hp + 1) * LANES] = st.T.astype(BF16)


def _attn_call(q_hm, qi_hm, wit, k2, vt, ki2, bias_tiles, batch, seq):
    t = k2.shape[0]
    nblk = seq // Q_BLOCK
    topk = min(TOPK_MAX, seq // 4)
    qrow = lambda b, i: (0, b * nblk + i, 0)
    return pl.pallas_call(
        functools.partial(_attn_kernel, topk=topk),
        grid=(batch, nblk),
        in_specs=[
            pl.BlockSpec((N_HEADS, Q_BLOCK, LANES), qrow),
            pl.BlockSpec((IDX_HEADS, Q_BLOCK, IDX_DIM), qrow),
            pl.BlockSpec((IDX_HEADS, Q_BLOCK), lambda b, i: (0, b * nblk + i)),
            pl.BlockSpec((seq, LANES), lambda b, i: (b, 0)),
            pl.BlockSpec((nblk, LANES, Q_BLOCK), lambda b, i: (b, 0, 0)),
            pl.BlockSpec((seq, IDX_DIM), lambda b, i: (b, 0)),
            pl.BlockSpec((3, N_HEADS, Q_BLOCK, Q_BLOCK), lambda b, i: (0, 0, 0, 0)),
        ],
        out_specs=pl.BlockSpec((Q_BLOCK, ATTN_W), lambda b, i: (b * nblk + i, 0)),
        out_shape=jax.ShapeDtypeStruct((t, ATTN_W), BF16),
        scratch_shapes=[
            pltpu.VMEM((seq, Q_BLOCK), jnp.int32),
            pltpu.VMEM((seq, Q_BLOCK), F32),
            pltpu.VMEM((SUBLANES, Q_BLOCK), jnp.int32),
            pltpu.VMEM((N_KV, HEAD_DIM, KV_REP * Q_BLOCK), F32),
            pltpu.VMEM((N_HEADS, Q_BLOCK), F32),
            pltpu.VMEM((N_HEADS, Q_BLOCK), F32),
        ],
        compiler_params=_cparams("parallel", "arbitrary"),
        name="attn",
    )(q_hm, qi_hm, wit, k2, vt, ki2, bias_tiles)


def _rel_bucket(rel):
    nb = REL_BUCKETS // 2
    max_exact = nb // 2
    ret = jnp.where(rel < 0, nb, 0)
    n = jnp.abs(rel)
    nf = jnp.maximum(n, 1).astype(jnp.float32)
    large = max_exact + (jnp.log(nf / max_exact) / math.log(REL_MAX_DIST / max_exact)
                         * (nb - max_exact)).astype(jnp.int32)
    large = jnp.minimum(large, nb - 1)
    return ret + jnp.where(n < max_exact, n, large)


def _bias_tiles(rel_bias):
    s = jnp.arange(Q_BLOCK, dtype=jnp.int32)[:, None]
    tq = jnp.arange(Q_BLOCK, dtype=jnp.int32)[None, :]
    d = jnp.arange(3, dtype=jnp.int32)[:, None, None]
    rel = d * Q_BLOCK + tq - s
    tiles = rel_bias[_rel_bucket(rel)]
    return jnp.moveaxis(tiles, -1, 1).astype(F32)


def _merge_kernel(x_ref, vc_ref, at_ref, gc_ref, ga_ref, wco_ref, wao_ref, wo_ref, g_ref, wr_ref,
                  br_ref, xo_ref, h_ref, comb_ref):
    y_conv = _dot(vc_ref[...], wco_ref[...])
    y_attn = _dot(at_ref[...], wao_ref[...])
    merged = gc_ref[...].astype(F32) * y_conv + ga_ref[...].astype(F32) * y_attn
    x = x_ref[...] + _dot(merged.astype(BF16), wo_ref[...])
    xo_ref[...] = x
    h = x * lax.rsqrt(jnp.mean(x * x, axis=-1, keepdims=True) + EPS) * g_ref[...]
    hb = h.astype(BF16)
    h_ref[...] = hb

    logits = _dot(hb, wr_ref[...]) + br_ref[...]
    lane = lax.broadcasted_iota(jnp.int32, logits.shape, 1)
    neg_inf = -jnp.inf
    gl = jnp.where(lane < N_GROUPS, logits, neg_inf)
    gmax = jnp.max(gl, axis=-1, keepdims=True)
    grp = jnp.min(jnp.where(gl == gmax, lane, LANES), axis=-1, keepdims=True)
    p_grp = 1.0 / jnp.sum(jnp.exp(gl - gmax), axis=-1, keepdims=True)
    e_lo = ROUTER_E_OFF + grp * EXPERTS_PER_GROUP
    in_grp = jnp.logical_and(lane >= e_lo, lane < e_lo + EXPERTS_PER_GROUP)
    el = jnp.where(in_grp, logits, neg_inf)
    v1 = jnp.max(el, axis=-1, keepdims=True)
    i1 = jnp.min(jnp.where(jnp.logical_and(in_grp, el == v1), lane, LANES), axis=-1, keepdims=True)
    rest = jnp.logical_and(in_grp, lane != i1)
    el2 = jnp.where(rest, logits, neg_inf)
    v2 = jnp.max(el2, axis=-1, keepdims=True)
    i2 = jnp.min(jnp.where(jnp.logical_and(rest, el2 == v2), lane, LANES), axis=-1, keepdims=True)
    e2 = jnp.exp(v2 - v1)
    p1 = p_grp / (1.0 + e2)
    p2 = p_grp * e2 / (1.0 + e2)
    comb_ref[...] = jnp.where(lane == i1, p1, jnp.where(lane == i2, p2, 0.0))


def _merge_call(x2, vc, at, gc, ga, wco, wao, wo, g, wr, br):
    t, d = x2.shape
    tm = MERGE_TM
    row = lambda i: (i, 0)
    const = lambda i: (0, 0)
    return pl.pallas_call(
        _merge_kernel,
        grid=(t // tm,),
        in_specs=[
            pl.BlockSpec((tm, d), row),
            pl.BlockSpec((tm, CONV_CH), row),
            pl.BlockSpec((tm, ATTN_W), row),
            pl.BlockSpec((tm, d), row),
            pl.BlockSpec((tm, d), row),
            pl.BlockSpec((CONV_CH, d), const),
            pl.BlockSpec((ATTN_W, d), const),
            pl.BlockSpec((d, d), const),
            pl.BlockSpec((1, d), const),
            pl.BlockSpec((d, LANES), const),
            pl.BlockSpec((1, LANES), const),
        ],
        out_specs=(pl.BlockSpec((tm, d), row), pl.BlockSpec((tm, d), row),
                   pl.BlockSpec((tm, LANES), row)),
        out_shape=(jax.ShapeDtypeStruct((t, d), F32), jax.ShapeDtypeStruct((t, d), BF16),
                   jax.ShapeDtypeStruct((t, LANES), F32)),
        compiler_params=_cparams("parallel"),
        name="merge",
    )(x2, vc, at, gc, ga, wco, wao, wo, g, wr, br)


def _pack_router(w_gr, b_gr, w_er, b_er):
    d = w_gr.shape[0]
    w = jnp.zeros((d, LANES), F32)
    w = w.at[:, :N_GROUPS].set(w_gr)
    w = w.at[:, ROUTER_E_OFF:ROUTER_E_OFF + N_EXPERTS].set(
        jnp.moveaxis(w_er, 0, 1).reshape(d, N_EXPERTS))
    b = jnp.zeros((1, LANES), F32)
    b = b.at[0, :N_GROUPS].set(b_gr)
    b = b.at[0, ROUTER_E_OFF:ROUTER_E_OFF + N_EXPERTS].set(b_er.reshape(N_EXPERTS))
    return w.astype(BF16), b


def _moe_kernel(h_ref, comb_ref, x_ref, wg_ref, wu_ref, wd_ref, o_ref):
    e = pl.program_id(1)

    @pl.when(e == 0)
    def _():
        o_ref[...] = x_ref[...]

    hb = h_ref[...]
    comb = comb_ref[...]
    lane = lax.broadcasted_iota(jnp.int32, comb.shape, 1)
    mids = []
    for ee in range(MOE_EB):
        col = ROUTER_E_OFF + e * MOE_EB + ee
        c = jnp.sum(jnp.where(lane == col, comb, 0.0), axis=-1, keepdims=True)
        a = _dot(hb, wg_ref[ee])
        b = _dot(hb, wu_ref[ee])
        mids.append((a * _sigmoid(a) * b * c).astype(BF16))
    mid = jnp.concatenate(mids, axis=1)
    o_ref[...] += _dot(mid, wd_ref[...].reshape(MOE_EB * EXPERT_FF, wd_ref.shape[-1]))


def _moe_call(h2, comb, x2, wg, wu, wd):
    t, d = x2.shape
    tm = MOE_TM
    row = lambda i, e: (i, 0)
    wmap = lambda i, e: (e, 0, 0)
    return pl.pallas_call(
        _moe_kernel,
        grid=(t // tm, N_EXPERTS // MOE_EB),
        in_specs=[
            pl.BlockSpec((tm, d), row),
            pl.BlockSpec((tm, LANES), row),
            pl.BlockSpec((tm, d), row),
            pl.BlockSpec((MOE_EB, d, EXPERT_FF), wmap),
            pl.BlockSpec((MOE_EB, d, EXPERT_FF), wmap),
            pl.BlockSpec((MOE_EB, EXPERT_FF, d), wmap),
        ],
        out_specs=pl.BlockSpec((tm, d), row),
        out_shape=jax.ShapeDtypeStruct((t, d), F32),
        compiler_params=_cparams("parallel", "arbitrary"),
        name="moe",
    )(h2, comb, x2, wg, wu, wd)


def _norm_kernel(x_ref, g_ref, o_ref):
    x = x_ref[...]
    o_ref[...] = x * lax.rsqrt(jnp.mean(x * x, axis=-1, keepdims=True) + EPS) * g_ref[...]


def _norm_call(x2, g):
    t, d = x2.shape
    tm = NORM_TM
    return pl.pallas_call(
        _norm_kernel,
        grid=(t // tm,),
        in_specs=[pl.BlockSpec((tm, d), lambda i: (i, 0)), pl.BlockSpec((1, d), lambda i: (0, 0))],
        out_specs=pl.BlockSpec((tm, d), lambda i: (i, 0)),
        out_shape=jax.ShapeDtypeStruct((t, d), F32),
        compiler_params=_cparams("parallel"),
        name="final_norm",
    )(x2, g)


def kernel(x, g_mix, w_in, conv_w, conv_b, conv_ln_g, conv_ln_b, w_conv_out, w_attn_out, w_out, rel_bias, g_ffn, w_group_router, b_group_router, w_expert_router, b_expert_router, w_e_gate, w_e_up, w_e_down, g_final):
    batch, seq, d = x.shape
    depth = g_mix.shape[0]
    t = batch * seq
    assert seq % CONV_TL == 0 and seq % Q_BLOCK == 0 and t % MOE_TM == 0
    x2 = x.reshape(t, d)
    bias_tiles = _bias_tiles(rel_bias)
    for l in range(depth):
        w_pack = _pack_proj_weights(w_in[l])
        u, q_hm, k2, vt, qi_hm, ki2, wit, gc, ga = _proj_call(x2, g_mix[l][None, :], w_pack)
        vc = _conv_call(u.reshape(batch, seq, 2 * CONV_CH), conv_w[l], conv_b[l][None, :],
                        conv_ln_g[l][None, :], conv_ln_b[l][None, :]).reshape(t, CONV_CH)
        at = _attn_call(q_hm, qi_hm, wit, k2, vt, ki2, bias_tiles, batch, seq)
        wr, br = _pack_router(w_group_router[l], b_group_router[l], w_expert_router[l],
                              b_expert_router[l])
        x_mid, h2, comb = _merge_call(x2, vc, at, gc, ga, w_conv_out[l].astype(BF16),
                                      w_attn_out[l].astype(BF16), w_out[l].astype(BF16),
                                      g_ffn[l][None, :], wr, br)
        x2 = _moe_call(h2, comb, x_mid, w_e_gate[l].astype(BF16), w_e_up[l].astype(BF16),
                       w_e_down[l].astype(BF16))
    return _norm_call(x2, g_final[None, :]).reshape(batch, seq, d)
```

```python
import functools
import math

import jax
import jax.numpy as jnp
import numpy as np
from jax import lax
from jax.experimental import pallas as pl
from jax.experimental.pallas import tpu as pltpu

CHUNK = 64
CONV_CH = 512
CONV_WIDTH = 31
N_HEADS = 8
HEAD_DIM = 64
N_KV = 2
KV_REP = N_HEADS // N_KV
ATTN_W = N_HEADS * HEAD_DIM
IDX_HEADS = 8
IDX_DIM = 32
TOPK_MAX = 256
Q_BLOCK = 128
REL_BUCKETS = 32
REL_MAX_DIST = 128
N_GROUPS = 4
EXPERTS_PER_GROUP = 8
N_EXPERTS = N_GROUPS * EXPERTS_PER_GROUP
EXPERT_FF = 256
EPS = 1e-6

LANES = 128
SUBLANES = 8
VMEM_LIMIT_BYTES = 56 * 1024 * 1024

LOG2E = math.log2(math.e)
NEG_BIG = -1e30
INT_MIN = -(2 ** 31)
KEY_MIN_FINITE = -0x7F800000
KEY_MAX_FINITE = 0x7F7FFFFF

BF16 = jnp.bfloat16
F32 = jnp.float32

PROJ_TM = 512
CONV_TL = 256
CONV_HALO = 32
MERGE_TM = 512
MOE_TM = 1024
MOE_EB = 4
NORM_TM = 1024

ROUTER_E_OFF = 32


def _cparams(*sem):
    return pltpu.CompilerParams(dimension_semantics=sem, vmem_limit_bytes=VMEM_LIMIT_BYTES)


def _sigmoid(x):
    return 1.0 / (1.0 + jnp.exp(-x))


def _dot(a, b):
    return jnp.dot(a, b, preferred_element_type=F32)


def _dot_nt(a, b):
    return lax.dot_general(a, b, (((1,), (1,)), ((), ())), preferred_element_type=F32)


def _dot_tn(a, b):
    return lax.dot_general(a, b, (((0,), (0,)), ((), ())), preferred_element_type=F32)


SEG_U = (0, 2 * CONV_CH)
SEG_Q = (SEG_U[1], SEG_U[1] + N_HEADS * LANES)
SEG_K = (SEG_Q[1], SEG_Q[1] + LANES)
SEG_V = (SEG_K[1], SEG_K[1] + LANES)
SEG_QI = (SEG_V[1], SEG_V[1] + IDX_HEADS * IDX_DIM)
SEG_KI = (SEG_QI[1], SEG_QI[1] + LANES)
SEG_WI = (SEG_KI[1], SEG_KI[1] + LANES)
SEG_GC = (SEG_WI[1], SEG_WI[1] + 1024)
SEG_GA = (SEG_GC[1], SEG_GC[1] + 1024)
PROJ_COLS = SEG_GA[1]


def _proj_kernel(x_ref, g_ref, w_ref, u_ref, q_ref, k_ref, vt_ref, qi_ref, ki_ref, wit_ref,
                 gc_ref, ga_ref):
    x = x_ref[...]
    h = x * lax.rsqrt(jnp.mean(x * x, axis=-1, keepdims=True) + EPS) * g_ref[...]
    hb = h.astype(BF16)

    def seg(s):
        return _dot(hb, w_ref[:, s[0]:s[1]])

    u_ref[...] = seg(SEG_U).astype(BF16)
    pq = seg(SEG_Q)
    for hh in range(N_HEADS):
        q_ref[hh] = pq[:, hh * LANES:(hh + 1) * LANES].astype(BF16)
    k_ref[...] = seg(SEG_K).astype(BF16)
    pv = seg(SEG_V)
    tm = pv.shape[0]
    for c in range(tm // LANES):
        vt_ref[c] = pv[c * LANES:(c + 1) * LANES, :].T.astype(BF16)
    pqi = seg(SEG_QI)
    for hh in range(IDX_HEADS):
        qi_ref[hh] = pqi[:, hh * IDX_DIM:(hh + 1) * IDX_DIM].astype(BF16)
    ki_ref[...] = seg(SEG_KI)[:, :IDX_DIM].astype(BF16)
    pwi = seg(SEG_WI)
    for c in range(tm // LANES):
        wit_ref[:, c * LANES:(c + 1) * LANES] = pwi[c * LANES:(c + 1) * LANES, :].T[:IDX_HEADS, :]
    gc_ref[...] = _sigmoid(seg(SEG_GC)).astype(BF16)
    ga_ref[...] = _sigmoid(seg(SEG_GA)).astype(BF16)


def _proj_call(x2, g, w_pack):
    t, d = x2.shape
    tm = PROJ_TM
    nq = t // Q_BLOCK
    row = lambda i: (i, 0)
    out_shape = (
        jax.ShapeDtypeStruct((t, 2 * CONV_CH), BF16),
        jax.ShapeDtypeStruct((N_HEADS, t, LANES), BF16),
        jax.ShapeDtypeStruct((t, LANES), BF16),
        jax.ShapeDtypeStruct((nq, LANES, Q_BLOCK), BF16),
        jax.ShapeDtypeStruct((IDX_HEADS, t, IDX_DIM), BF16),
        jax.ShapeDtypeStruct((t, IDX_DIM), BF16),
        jax.ShapeDtypeStruct((IDX_HEADS, t), F32),
        jax.ShapeDtypeStruct((t, d), BF16),
        jax.ShapeDtypeStruct((t, d), BF16),
    )
    out_specs = (
        pl.BlockSpec((tm, 2 * CONV_CH), row),
        pl.BlockSpec((N_HEADS, tm, LANES), lambda i: (0, i, 0)),
        pl.BlockSpec((tm, LANES), row),
        pl.BlockSpec((tm // Q_BLOCK, LANES, Q_BLOCK), lambda i: (i, 0, 0)),
        pl.BlockSpec((IDX_HEADS, tm, IDX_DIM), lambda i: (0, i, 0)),
        pl.BlockSpec((tm, IDX_DIM), row),
        pl.BlockSpec((IDX_HEADS, tm), lambda i: (0, i)),
        pl.BlockSpec((tm, d), row),
        pl.BlockSpec((tm, d), row),
    )
    return pl.pallas_call(
        _proj_kernel,
        grid=(t // tm,),
        in_specs=[
            pl.BlockSpec((tm, d), row),
            pl.BlockSpec((1, d), lambda i: (0, 0)),
            pl.BlockSpec((d, PROJ_COLS), lambda i: (0, 0)),
        ],
        out_specs=out_specs,
        out_shape=out_shape,
        compiler_params=_cparams("parallel"),
        name="proj",
    )(x2, g, w_pack)


def _pack_proj_weights(w_in):
    d = w_in.shape[0]
    splits = (2 * CONV_CH, ATTN_W, N_KV * HEAD_DIM, N_KV * HEAD_DIM, IDX_HEADS * IDX_DIM,
              IDX_DIM, IDX_HEADS, d, d)
    offs = np.concatenate([[0], np.cumsum(splits)])
    w_u, w_q, w_k, w_v, w_qi, w_ki, w_wi, w_gc, w_ga = [
        w_in[:, int(offs[i]):int(offs[i + 1])] for i in range(len(splits))]
    w_q = (w_q * (HEAD_DIM ** -0.5 * LOG2E)).reshape(d, N_KV, KV_REP, HEAD_DIM)
    q_pad = jnp.zeros((d, N_KV, KV_REP, N_KV, HEAD_DIM), w_in.dtype)
    for gidx in range(N_KV):
        q_pad = q_pad.at[:, gidx, :, gidx, :].set(w_q[:, gidx])
    q_pad = q_pad.reshape(d, N_HEADS * LANES)
    pad = lambda w: jnp.pad(w, ((0, 0), (0, LANES - w.shape[1])))
    packed = jnp.concatenate([w_u, q_pad, w_k, w_v, w_qi, pad(w_ki), pad(w_wi), w_gc, w_ga], axis=1)
    assert packed.shape[1] == PROJ_COLS
    return packed.astype(BF16)


def _conv_kernel(u_ref, halo_ref, cw_ref, cb_ref, lg_ref, lb_ref, o_ref, ext_ref, y_ref):
    i = pl.program_id(1)
    tl = u_ref.shape[1]

    def glu(u):
        u = u.astype(F32)
        return u[:, :CONV_CH] * _sigmoid(u[:, CONV_CH:])

    halo = glu(halo_ref[0])
    ext_ref[0:CONV_HALO, :] = jnp.where(i > 0, halo, 0.0)
    ext_ref[CONV_HALO:, :] = glu(u_ref[0])
    first = CONV_HALO - (CONV_WIDTH - 1)
    for c in range(CONV_CH // LANES):
        cs = slice(c * LANES, (c + 1) * LANES)
        acc = jnp.zeros((tl, LANES), F32)
        for j in range(CONV_WIDTH):
            acc = acc + cw_ref[j:j + 1, cs] * ext_ref[first + j:first + j + tl, cs]
        y_ref[:, cs] = acc + cb_ref[:, cs]
    y = y_ref[...]
    mu = jnp.mean(y, axis=-1, keepdims=True)
    yc = y - mu
    var = jnp.mean(yc * yc, axis=-1, keepdims=True)
    yn = yc * lax.rsqrt(var + EPS) * lg_ref[...] + lb_ref[...]
    o_ref[0] = (yn * _sigmoid(yn)).astype(BF16)


def _conv_call(u3, conv_w, conv_b, ln_g, ln_b):
    b, l, _ = u3.shape
    tl = CONV_TL
    halo_blocks = tl // CONV_HALO
    const = lambda bi, i: (0, 0)
    return pl.pallas_call(
        _conv_kernel,
        grid=(b, l // tl),
        in_specs=[
            pl.BlockSpec((1, tl, 2 * CONV_CH), lambda bi, i: (bi, i, 0)),
            pl.BlockSpec((1, CONV_HALO, 2 * CONV_CH),
                         lambda bi, i: (bi, jnp.maximum(i * halo_blocks - 1, 0), 0)),
            pl.BlockSpec((CONV_WIDTH, CONV_CH), const),
            pl.BlockSpec((1, CONV_CH), const),
            pl.BlockSpec((1, CONV_CH), const),
            pl.BlockSpec((1, CONV_CH), const),
        ],
        out_specs=pl.BlockSpec((1, tl, CONV_CH), lambda bi, i: (bi, i, 0)),
        out_shape=jax.ShapeDtypeStruct((b, l, CONV_CH), BF16),
        scratch_shapes=[pltpu.VMEM((CONV_HALO + tl, CONV_CH), F32), pltpu.VMEM((tl, CONV_CH), F32)],
        compiler_params=_cparams("parallel", "parallel"),
        name="conv",
    )(u3, u3, conv_w, conv_b, ln_g, ln_b)


def _sortable_key(score):
    b = lax.bitcast_convert_type(score, jnp.int32)
    return b ^ ((b >> 31) & 0x7FFFFFFF)


KEY_CHUNK = 2 * Q_BLOCK
HALF16 = 1 << 15
ONES_ROWS = 16


def _tree_reduce(x, rows, op):
    parts = [x[j * rows:(j + 1) * rows] for j in range(x.shape[0] // rows)]
    while len(parts) > 1:
        parts = [op(a, b) for a, b in zip(parts[0::2], parts[1::2])]
    return parts[0]


def _attn_kernel(q_ref, qi_ref, wit_ref, k_ref, vt_ref, ki_ref, bias_ref, o_ref,
                 key_ref, khi_ref, klo_ref, madd_ref, thr_ref, x_ref, p_ref, acc_ref, *, topk):
    i = pl.program_id(1)
    qb = Q_BLOCK
    ch = KEY_CHUNK
    nkb = i + 1
    nch = (nkb + 1) // 2
    q_all = q_ref[...].reshape(N_HEADS * qb, LANES)
    qi_st = qi_ref[...].reshape(IDX_HEADS * qb, IDX_DIM)
    wit = wit_ref[...]
    t_loc = lax.broadcasted_iota(jnp.int32, (1, qb), 1)
    limit = ((i * qb + t_loc) // CHUNK + 1) * CHUNK
    idx_scale = (IDX_DIM ** -0.5) * (IDX_HEADS ** -0.5)

    def chunk_rows(c):
        return pl.ds(pl.multiple_of(c * ch, ch), ch)

    def score_body(c, carry):
        rows = chunk_rows(c)
        dots = _dot_nt(ki_ref[rows, :], qi_st)
        sc = jnp.zeros((ch, qb), F32)
        for hh in range(IDX_HEADS):
            sc = sc + jnp.maximum(dots[:, hh * qb:(hh + 1) * qb], 0.0) * wit[hh:hh + 1, :]
        key = _sortable_key(sc * idx_scale)
        s_glob = c * ch + lax.broadcasted_iota(jnp.int32, (ch, qb), 0)
        key = jnp.where(s_glob < limit, key, INT_MIN)
        key_ref[rows, :] = key
        khi_ref[rows, :] = (key >> 16).astype(jnp.int16)
        klo_ref[rows, :] = ((key & 0xFFFF) - HALF16).astype(jnp.int16)
        x_ref[rows, :] = _dot_nt(k_ref[rows, :], q_all)
        return carry

    lax.fori_loop(0, nch, score_body, 0)

    def count16(ref, pred):
        def body(c, acc):
            ind = jnp.where(pred(ref[chunk_rows(c), :]), jnp.int16(1), jnp.int16(0))
            return acc + _tree_reduce(ind, acc.shape[0], jnp.add)
        acc = lax.fori_loop(0, nch, body, jnp.zeros((4 * SUBLANES, qb), jnp.int16))
        return jnp.sum(acc.astype(jnp.int32), axis=0, keepdims=True)

    def count32(pred):
        def body(c, acc):
            ind = jnp.where(pred(key_ref[chunk_rows(c), :]), 1, 0)
            return acc + _tree_reduce(ind, acc.shape[0], jnp.add)
        acc = lax.fori_loop(0, nch, body, jnp.zeros((4 * SUBLANES, qb), jnp.int32))
        return jnp.sum(acc, axis=0, keepdims=True)

    def kth_largest16(ref, want):
        def bit_body(it, lo):
            cand = lo + jnp.left_shift(jnp.int32(1), 15 - it)
            c16 = cand.astype(jnp.int16)
            return jnp.where(count16(ref, lambda v: v >= c16) >= want, cand, lo)
        return lax.fori_loop(0, 16, bit_body, jnp.full((1, qb), -HALF16, jnp.int32))

    thr_ref[...] = jnp.full(thr_ref.shape, KEY_MIN_FINITE, jnp.int32)

    @pl.when(nkb * qb > topk)
    def _():
        t_hi = kth_largest16(khi_ref, topk)
        t_hi16 = t_hi.astype(jnp.int16)
        n_above = count16(khi_ref, lambda v: v > t_hi16)

        def mask_lo(c, carry):
            rows = chunk_rows(c)
            klo_ref[rows, :] = jnp.where(khi_ref[rows, :] == t_hi16, klo_ref[rows, :],
                                         jnp.int16(-HALF16))
            return carry
        lax.fori_loop(0, nch, mask_lo, 0)
        t_lo = kth_largest16(klo_ref, topk - n_above)
        thr_ref[0:1, :] = jnp.maximum(jnp.left_shift(t_hi, 16) | (t_lo + HALF16), KEY_MIN_FINITE)

    thr = thr_ref[0:1, :]
    n_ge = count32(lambda key: key >= thr)
    has_ties = jnp.max(n_ge) > topk

    @pl.when(jnp.logical_not(has_ties))
    def _():
        def body(c, carry):
            rows = chunk_rows(c)
            key = key_ref[rows, :]
            sel = jnp.logical_and(key >= thr, key <= KEY_MAX_FINITE)
            madd_ref[rows, :] = jnp.where(sel, 0.0, NEG_BIG)
            return carry
        lax.fori_loop(0, nch, body, 0)

    @pl.when(has_ties)
    def _():
        n_gt = count32(lambda key: key > thr)
        need = (topk - n_gt).astype(F32)
        lower = (lax.broadcasted_iota(jnp.int32, (ch, ch), 1)
                 < lax.broadcasted_iota(jnp.int32, (ch, ch), 0)).astype(BF16)

        def body(c, seen):
            rows = chunk_rows(c)
            key = key_ref[rows, :]
            eq = key == thr
            eqf = jnp.where(eq, 1.0, 0.0)
            before = _dot(lower, eqf.astype(BF16)) + seen
            sel = jnp.logical_or(key > thr, jnp.logical_and(eq, before < need))
            sel = jnp.logical_and(sel, key <= KEY_MAX_FINITE)
            madd_ref[rows, :] = jnp.where(sel, 0.0, NEG_BIG)
            return seen + jnp.sum(eqf, axis=0, keepdims=True)
        lax.fori_loop(0, nch, body, jnp.zeros((1, qb), F32))

    def mask_rows(rows, tile, mparts):
        madd = madd_ref[rows, :]
        out = []
        for hh in range(N_HEADS):
            cols = slice(hh * qb, (hh + 1) * qb)
            x = x_ref[rows, cols] + madd
            if tile is not None:
                x = x + bias_ref[tile, hh]
            x_ref[rows, cols] = x
            out.append(jnp.maximum(mparts[hh], _tree_reduce(x, SUBLANES, jnp.maximum)))
        return tuple(out)

    def far_body(c, mparts):
        return mask_rows(chunk_rows(c), None, mparts)

    def near_body(c, mparts):
        for half in range(ch // qb):
            jb = c * (ch // qb) + half
            d = i - jb
            tile = jnp.where(d == 0, 0, jnp.where(d == 1, 1, 2))
            rows = pl.ds(pl.multiple_of(jb * qb, qb), qb)
            mparts = mask_rows(rows, tile, mparts)
        return mparts

    n_far = jnp.maximum((i - 1) // 2, 0)
    mparts = tuple(jnp.full((SUBLANES, qb), NEG_BIG, F32) for _ in range(N_HEADS))
    mparts = lax.fori_loop(0, n_far, far_body, mparts)
    mparts = lax.fori_loop(n_far, nch, near_body, mparts)
    m_rows = [jnp.max(mp, axis=0, keepdims=True) for mp in mparts]

    acc_ref[...] = jnp.zeros(acc_ref.shape, F32)
    ones = jnp.ones((ONES_ROWS, ch), BF16)

    def prob_chunk(c):
        rows = chunk_rows(c)
        for hh in range(N_HEADS):
            cols = slice(hh * qb, (hh + 1) * qb)
            p_ref[rows, cols] = jnp.exp2((x_ref[rows, cols] - m_rows[hh]).astype(BF16))

    def pv_chunk(c):
        rows = chunk_rows(c)
        vt2 = jnp.concatenate([vt_ref[2 * c], vt_ref[2 * c + 1]], axis=1)
        for g in range(N_KV):
            lhs = jnp.concatenate([vt2[g * HEAD_DIM:(g + 1) * HEAD_DIM, :], ones], axis=0)
            acc_ref[g] += _dot(lhs, p_ref[rows, g * KV_REP * qb:(g + 1) * KV_REP * qb])

    prob_chunk(0)

    def pv_body(c, carry):
        pv_chunk(c - 1)
        prob_chunk(c)
        return carry

    lax.fori_loop(1, nch, pv_body, 0)
    pv_chunk(nch - 1)

    for hp in range(N_HEADS // 2):
        parts = []
        for hh in (2 * hp, 2 * hp + 1):
            g, r = hh // KV_REP, hh % KV_REP
            cols = slice(r * qb, (r + 1) * qb)
            inv_l = 1.0 / acc_ref[g, HEAD_DIM:HEAD_DIM + 1, cols]
            parts.append(acc_ref[g, 0:HEAD_DIM, cols] * inv_l)
        st = jnp.concatenate(parts, axis=0)
        o_ref[:, hp * LANES:(hp + 1) * LANES] = st.T.astype(BF16)


def _attn_call(q_hm, qi_hm, wit, k2, vt, ki2, bias_tiles, batch, seq):
    t = k2.shape[0]
    nblk = seq // Q_BLOCK
    topk = min(TOPK_MAX, seq // 4)
    qrow = lambda b, i: (0, b * nblk + i, 0)
    return pl.pallas_call(
        functools.partial(_attn_kernel, topk=topk),
        grid=(batch, nblk),
        in_specs=[
            pl.BlockSpec((N_HEADS, Q_BLOCK, LANES), qrow),
            pl.BlockSpec((IDX_HEADS, Q_BLOCK, IDX_DIM), qrow),
            pl.BlockSpec((IDX_HEADS, Q_BLOCK), lambda b, i: (0, b * nblk + i)),
            pl.BlockSpec((seq, LANES), lambda b, i: (b, 0)),
            pl.BlockSpec((nblk, LANES, Q_BLOCK), lambda b, i: (b, 0, 0)),
            pl.BlockSpec((seq, IDX_DIM), lambda b, i: (b, 0)),
            pl.BlockSpec((3, N_HEADS, Q_BLOCK, Q_BLOCK), lambda b, i: (0, 0, 0, 0)),
        ],
        out_specs=pl.BlockSpec((Q_BLOCK, ATTN_W), lambda b, i: (b * nblk + i, 0)),
        out_shape=jax.ShapeDtypeStruct((t, ATTN_W), BF16),
        scratch_shapes=[
            pltpu.VMEM((seq, Q_BLOCK), jnp.int32),
            pltpu.VMEM((seq, Q_BLOCK), jnp.int16),
            pltpu.VMEM((seq, Q_BLOCK), jnp.int16),
            pltpu.VMEM((seq, Q_BLOCK), F32),
            pltpu.VMEM((SUBLANES, Q_BLOCK), jnp.int32),
            pltpu.VMEM((seq, N_HEADS * Q_BLOCK), F32),
            pltpu.VMEM((seq, N_HEADS * Q_BLOCK), BF16),
            pltpu.VMEM((N_KV, HEAD_DIM + ONES_ROWS, KV_REP * Q_BLOCK), F32),
        ],
        compiler_params=_cparams("parallel", "arbitrary"),
        name="attn",
    )(q_hm, qi_hm, wit, k2, vt, ki2, bias_tiles)


def _rel_bucket(rel):
    nb = REL_BUCKETS // 2
    max_exact = nb // 2
    ret = jnp.where(rel < 0, nb, 0)
    n = jnp.abs(rel)
    nf = jnp.maximum(n, 1).astype(jnp.float32)
    large = max_exact + (jnp.log(nf / max_exact) / math.log(REL_MAX_DIST / max_exact)
                         * (nb - max_exact)).astype(jnp.int32)
    large = jnp.minimum(large, nb - 1)
    return ret + jnp.where(n < max_exact, n, large)


def _bias_tiles(rel_bias):
    s = jnp.arange(Q_BLOCK, dtype=jnp.int32)[:, None]
    tq = jnp.arange(Q_BLOCK, dtype=jnp.int32)[None, :]
    d = jnp.arange(3, dtype=jnp.int32)[:, None, None]
    bucket = _rel_bucket(d * Q_BLOCK + tq - s)
    onehot = bucket[:, None, :, :, None] == jnp.arange(REL_BUCKETS, dtype=jnp.int32)
    tiles = jnp.sum(jnp.where(onehot, rel_bias.T[None, :, None, None, :], 0.0), axis=-1)
    return ((tiles - tiles[2:3, :, :1, :1]) * LOG2E).astype(F32)


def _merge_kernel(x_ref, vc_ref, at_ref, gc_ref, ga_ref, wco_ref, wao_ref, wo_ref, g_ref, wr_ref,
                  br_ref, xo_ref, h_ref, comb_ref):
    y_conv = _dot(vc_ref[...], wco_ref[...])
    y_attn = _dot(at_ref[...], wao_ref[...])
    merged = gc_ref[...].astype(F32) * y_conv + ga_ref[...].astype(F32) * y_attn
    x = x_ref[...] + _dot(merged.astype(BF16), wo_ref[...])
    xo_ref[...] = x
    h = x * lax.rsqrt(jnp.mean(x * x, axis=-1, keepdims=True) + EPS) * g_ref[...]
    hb = h.astype(BF16)
    h_ref[...] = hb

    logits = _dot(hb, wr_ref[...]) + br_ref[...]
    lane = lax.broadcasted_iota(jnp.int32, logits.shape, 1)
    neg_inf = -jnp.inf
    gl = jnp.where(lane < N_GROUPS, logits, neg_inf)
    gmax = jnp.max(gl, axis=-1, keepdims=True)
    grp = jnp.min(jnp.where(gl == gmax, lane, LANES), axis=-1, keepdims=True)
    p_grp = 1.0 / jnp.sum(jnp.exp(gl - gmax), axis=-1, keepdims=True)
    e_lo = ROUTER_E_OFF + grp * EXPERTS_PER_GROUP
    in_grp = jnp.logical_and(lane >= e_lo, lane < e_lo + EXPERTS_PER_GROUP)
    el = jnp.where(in_grp, logits, neg_inf)
    v1 = jnp.max(el, axis=-1, keepdims=True)
    i1 = jnp.min(jnp.where(jnp.logical_and(in_grp, el == v1), lane, LANES), axis=-1, keepdims=True)
    rest = jnp.logical_and(in_grp, lane != i1)
    el2 = jnp.where(rest, logits, neg_inf)
    v2 = jnp.max(el2, axis=-1, keepdims=True)
    i2 = jnp.min(jnp.where(jnp.logical_and(rest, el2 == v2), lane, LANES), axis=-1, keepdims=True)
    e2 = jnp.exp(v2 - v1)
    p1 = p_grp / (1.0 + e2)
    p2 = p_grp * e2 / (1.0 + e2)
    comb_ref[...] = jnp.where(lane == i1, p1, jnp.where(lane == i2, p2, 0.0))


def _merge_call(x2, vc, at, gc, ga, wco, wao, wo, g, wr, br):
    t, d = x2.shape
    tm = MERGE_TM
    row = lambda i: (i, 0)
    const = lambda i: (0, 0)
    return pl.pallas_call(
        _merge_kernel,
        grid=(t // tm,),
        in_specs=[
            pl.BlockSpec((tm, d), row),
            pl.BlockSpec((tm, CONV_CH), row),
            pl.BlockSpec((tm, ATTN_W), row),
            pl.BlockSpec((tm, d), row),
            pl.BlockSpec((tm, d), row),
            pl.BlockSpec((CONV_CH, d), const),
            pl.BlockSpec((ATTN_W, d), const),
            pl.BlockSpec((d, d), const),
            pl.BlockSpec((1, d), const),
            pl.BlockSpec((d, LANES), const),
            pl.BlockSpec((1, LANES), const),
        ],
        out_specs=(pl.BlockSpec((tm, d), row), pl.BlockSpec((tm, d), row),
                   pl.BlockSpec((tm, LANES), row)),
        out_shape=(jax.ShapeDtypeStruct((t, d), F32), jax.ShapeDtypeStruct((t, d), BF16),
                   jax.ShapeDtypeStruct((t, LANES), F32)),
        compiler_params=_cparams("parallel"),
        name="merge",
    )(x2, vc, at, gc, ga, wco, wao, wo, g, wr, br)


def _pack_router(w_gr, b_gr, w_er, b_er):
    d = w_gr.shape[0]
    w = jnp.zeros((d, LANES), F32)
    w = w.at[:, :N_GROUPS].set(w_gr)
    w = w.at[:, ROUTER_E_OFF:ROUTER_E_OFF + N_EXPERTS].set(
        jnp.moveaxis(w_er, 0, 1).reshape(d, N_EXPERTS))
    b = jnp.zeros((1, LANES), F32)
    b = b.at[0, :N_GROUPS].set(b_gr)
    b = b.at[0, ROUTER_E_OFF:ROUTER_E_OFF + N_EXPERTS].set(b_er.reshape(N_EXPERTS))
    return w.astype(BF16), b


def _moe_kernel(h_ref, comb_ref, x_ref, wg_ref, wu_ref, wd_ref, o_ref):
    e = pl.program_id(1)

    @pl.when(e == 0)
    def _():
        o_ref[...] = x_ref[...]

    hb = h_ref[...]
    comb = comb_ref[...]
    lane = lax.broadcasted_iota(jnp.int32, comb.shape, 1)
    mids = []
    for ee in range(MOE_EB):
        col = ROUTER_E_OFF + e * MOE_EB + ee
        c = jnp.sum(jnp.where(lane == col, comb, 0.0), axis=-1, keepdims=True)
        a = _dot(hb, wg_ref[ee])
        b = _dot(hb, wu_ref[ee])
        mids.append((a * _sigmoid(a) * b * c).astype(BF16))
    mid = jnp.concatenate(mids, axis=1)
    o_ref[...] += _dot(mid, wd_ref[...].reshape(MOE_EB * EXPERT_FF, wd_ref.shape[-1]))


def _moe_call(h2, comb, x2, wg, wu, wd):
    t, d = x2.shape
    tm = MOE_TM
    row = lambda i, e: (i, 0)
    wmap = lambda i, e: (e, 0, 0)
    return pl.pallas_call(
        _moe_kernel,
        grid=(t // tm, N_EXPERTS // MOE_EB),
        in_specs=[
            pl.BlockSpec((tm, d), row),
            pl.BlockSpec((tm, LANES), row),
            pl.BlockSpec((tm, d), row),
            pl.BlockSpec((MOE_EB, d, EXPERT_FF), wmap),
            pl.BlockSpec((MOE_EB, d, EXPERT_FF), wmap),
            pl.BlockSpec((MOE_EB, EXPERT_FF, d), wmap),
        ],
        out_specs=pl.BlockSpec((tm, d), row),
        out_shape=jax.ShapeDtypeStruct((t, d), F32),
        compiler_params=_cparams("parallel", "arbitrary"),
        name="moe",
    )(h2, comb, x2, wg, wu, wd)


def _norm_kernel(x_ref, g_ref, o_ref):
    x = x_ref[...]
    o_ref[...] = x * lax.rsqrt(jnp.mean(x * x, axis=-1, keepdims=True) + EPS) * g_ref[...]


def _norm_call(x2, g):
    t, d = x2.shape
    tm = NORM_TM
    return pl.pallas_call(
        _norm_kernel,
        grid=(t // tm,),
        in_specs=[pl.BlockSpec((tm, d), lambda i: (i, 0)), pl.BlockSpec((1, d), lambda i: (0, 0))],
        out_specs=pl.BlockSpec((tm, d), lambda i: (i, 0)),
        out_shape=jax.ShapeDtypeStruct((t, d), F32),
        compiler_params=_cparams("parallel"),
        name="final_norm",
    )(x2, g)


def kernel(x, g_mix, w_in, conv_w, conv_b, conv_ln_g, conv_ln_b, w_conv_out, w_attn_out, w_out, rel_bias, g_ffn, w_group_router, b_group_router, w_expert_router, b_expert_router, w_e_gate, w_e_up, w_e_down, g_final):
    batch, seq, d = x.shape
    depth = g_mix.shape[0]
    t = batch * seq
    assert seq % CONV_TL == 0 and seq % KEY_CHUNK == 0 and t % MOE_TM == 0
    x2 = x.reshape(t, d)
    bias_tiles = _bias_tiles(rel_bias)
    for l in range(depth):
        w_pack = _pack_proj_weights(w_in[l])
        u, q_hm, k2, vt, qi_hm, ki2, wit, gc, ga = _proj_call(x2, g_mix[l][None, :], w_pack)
        vc = _conv_call(u.reshape(batch, seq, 2 * CONV_CH), conv_w[l], conv_b[l][None, :],
                        conv_ln_g[l][None, :], conv_ln_b[l][None, :]).reshape(t, CONV_CH)
        at = _attn_call(q_hm, qi_hm, wit, k2, vt, ki2, bias_tiles, batch, seq)
        wr, br = _pack_router(w_group_router[l], b_group_router[l], w_expert_router[l],
                              b_expert_router[l])
        x_mid, h2, comb = _merge_call(x2, vc, at, gc, ga, w_conv_out[l].astype(BF16),
                                      w_attn_out[l].astype(BF16), w_out[l].astype(BF16),
                                      g_ffn[l][None, :], wr, br)
        x2 = _moe_call(h2, comb, x_mid, w_e_gate[l].astype(BF16), w_e_up[l].astype(BF16),
                       w_e_down[l].astype(BF16))
    return _norm_call(x2, g_final[None, :]).reshape(batch, seq, d)
```

```python
import functools
import math

import jax
import jax.numpy as jnp
import numpy as np
from jax import lax
from jax.experimental import pallas as pl
from jax.experimental.pallas import tpu as pltpu

CHUNK = 64
CONV_CH = 512
CONV_WIDTH = 31
N_HEADS = 8
HEAD_DIM = 64
N_KV = 2
KV_REP = N_HEADS // N_KV
ATTN_W = N_HEADS * HEAD_DIM
IDX_HEADS = 8
IDX_DIM = 32
TOPK_MAX = 256
Q_BLOCK = 128
REL_BUCKETS = 32
REL_MAX_DIST = 128
N_GROUPS = 4
EXPERTS_PER_GROUP = 8
N_EXPERTS = N_GROUPS * EXPERTS_PER_GROUP
EXPERT_FF = 256
EPS = 1e-6

LANES = 128
SUBLANES = 8
VMEM_LIMIT_BYTES = 56 * 1024 * 1024

LOG2E = math.log2(math.e)
NEG_BIG = -1e30
INT_MIN = -(2 ** 31)
KEY_MIN_FINITE = -0x7F800000
KEY_MAX_FINITE = 0x7F7FFFFF

BF16 = jnp.bfloat16
F32 = jnp.float32

PROJ_TM = 512
CONV_TL = 256
CONV_HALO = 32
MERGE_TM = 512
MOE_TM = 1024
MOE_EB = 4
NORM_TM = 1024

ROUTER_E_OFF = 32


def _cparams(*sem):
    return pltpu.CompilerParams(dimension_semantics=sem, vmem_limit_bytes=VMEM_LIMIT_BYTES)


def _sigmoid(x):
    return 1.0 / (1.0 + jnp.exp(-x))


def _dot(a, b):
    return jnp.dot(a, b, preferred_element_type=F32)


def _dot_nt(a, b):
    return lax.dot_general(a, b, (((1,), (1,)), ((), ())), preferred_element_type=F32)


def _dot_tn(a, b):
    return lax.dot_general(a, b, (((0,), (0,)), ((), ())), preferred_element_type=F32)


SEG_U = (0, 2 * CONV_CH)
SEG_Q = (SEG_U[1], SEG_U[1] + N_HEADS * LANES)
SEG_K = (SEG_Q[1], SEG_Q[1] + LANES)
SEG_V = (SEG_K[1], SEG_K[1] + LANES)
SEG_QI = (SEG_V[1], SEG_V[1] + IDX_HEADS * IDX_DIM)
SEG_KI = (SEG_QI[1], SEG_QI[1] + LANES)
SEG_WI = (SEG_KI[1], SEG_KI[1] + LANES)
SEG_GC = (SEG_WI[1], SEG_WI[1] + 1024)
SEG_GA = (SEG_GC[1], SEG_GC[1] + 1024)
PROJ_COLS = SEG_GA[1]


def _proj_kernel(x_ref, g_ref, w_ref, u_ref, q_ref, k_ref, vt_ref, qi_ref, ki_ref, wit_ref,
                 gc_ref, ga_ref):
    x = x_ref[...]
    h = x * lax.rsqrt(jnp.mean(x * x, axis=-1, keepdims=True) + EPS) * g_ref[...]
    hb = h.astype(BF16)

    def seg(s):
        return _dot(hb, w_ref[:, s[0]:s[1]])

    u_ref[...] = seg(SEG_U).astype(BF16)
    pq = seg(SEG_Q)
    for hh in range(N_HEADS):
        q_ref[hh] = pq[:, hh * LANES:(hh + 1) * LANES].astype(BF16)
    k_ref[...] = seg(SEG_K).astype(BF16)
    pv = seg(SEG_V)
    tm = pv.shape[0]
    for c in range(tm // LANES):
        vt_ref[c] = pv[c * LANES:(c + 1) * LANES, :].T.astype(BF16)
    pqi = seg(SEG_QI)
    for hh in range(IDX_HEADS):
        qi_ref[hh] = pqi[:, hh * IDX_DIM:(hh + 1) * IDX_DIM].astype(BF16)
    ki_ref[...] = seg(SEG_KI)[:, :IDX_DIM].astype(BF16)
    pwi = seg(SEG_WI)
    for c in range(tm // LANES):
        wit_ref[:, c * LANES:(c + 1) * LANES] = pwi[c * LANES:(c + 1) * LANES, :].T[:IDX_HEADS, :]
    gc_ref[...] = _sigmoid(seg(SEG_GC)).astype(BF16)
    ga_ref[...] = _sigmoid(seg(SEG_GA)).astype(BF16)


def _proj_call(x2, g, w_pack):
    t, d = x2.shape
    tm = PROJ_TM
    nq = t // Q_BLOCK
    row = lambda i: (i, 0)
    out_shape = (
        jax.ShapeDtypeStruct((t, 2 * CONV_CH), BF16),
        jax.ShapeDtypeStruct((N_HEADS, t, LANES), BF16),
        jax.ShapeDtypeStruct((t, LANES), BF16),
        jax.ShapeDtypeStruct((nq, LANES, Q_BLOCK), BF16),
        jax.ShapeDtypeStruct((IDX_HEADS, t, IDX_DIM), BF16),
        jax.ShapeDtypeStruct((t, IDX_DIM), BF16),
        jax.ShapeDtypeStruct((IDX_HEADS, t), F32),
        jax.ShapeDtypeStruct((t, d), BF16),
        jax.ShapeDtypeStruct((t, d), BF16),
    )
    out_specs = (
        pl.BlockSpec((tm, 2 * CONV_CH), row),
        pl.BlockSpec((N_HEADS, tm, LANES), lambda i: (0, i, 0)),
        pl.BlockSpec((tm, LANES), row),
        pl.BlockSpec((tm // Q_BLOCK, LANES, Q_BLOCK), lambda i: (i, 0, 0)),
        pl.BlockSpec((IDX_HEADS, tm, IDX_DIM), lambda i: (0, i, 0)),
        pl.BlockSpec((tm, IDX_DIM), row),
        pl.BlockSpec((IDX_HEADS, tm), lambda i: (0, i)),
        pl.BlockSpec((tm, d), row),
        pl.BlockSpec((tm, d), row),
    )
    return pl.pallas_call(
        _proj_kernel,
        grid=(t // tm,),
        in_specs=[
            pl.BlockSpec((tm, d), row),
            pl.BlockSpec((1, d), lambda i: (0, 0)),
            pl.BlockSpec((d, PROJ_COLS), lambda i: (0, 0)),
        ],
        out_specs=out_specs,
        out_shape=out_shape,
        compiler_params=_cparams("parallel"),
        name="proj",
    )(x2, g, w_pack)


def _pack_proj_weights(w_in):
    d = w_in.shape[0]
    splits = (2 * CONV_CH, ATTN_W, N_KV * HEAD_DIM, N_KV * HEAD_DIM, IDX_HEADS * IDX_DIM,
              IDX_DIM, IDX_HEADS, d, d)
    offs = np.concatenate([[0], np.cumsum(splits)])
    w_u, w_q, w_k, w_v, w_qi, w_ki, w_wi, w_gc, w_ga = [
        w_in[:, int(offs[i]):int(offs[i + 1])] for i in range(len(splits))]
    w_q = (w_q * (HEAD_DIM ** -0.5 * LOG2E)).reshape(d, N_KV, KV_REP, HEAD_DIM)
    q_pad = jnp.zeros((d, N_KV, KV_REP, N_KV, HEAD_DIM), w_in.dtype)
    for gidx in range(N_KV):
        q_pad = q_pad.at[:, gidx, :, gidx, :].set(w_q[:, gidx])
    q_pad = q_pad.reshape(d, N_HEADS * LANES)
    pad = lambda w: jnp.pad(w, ((0, 0), (0, LANES - w.shape[1])))
    packed = jnp.concatenate([w_u, q_pad, w_k, w_v, w_qi, pad(w_ki), pad(w_wi), w_gc, w_ga], axis=1)
    assert packed.shape[1] == PROJ_COLS
    return packed.astype(BF16)


def _conv_kernel(u_ref, halo_ref, cw_ref, cb_ref, lg_ref, lb_ref, o_ref, ext_ref, y_ref):
    i = pl.program_id(1)
    tl = u_ref.shape[1]

    def glu(u):
        u = u.astype(F32)
        return u[:, :CONV_CH] * _sigmoid(u[:, CONV_CH:])

    halo = glu(halo_ref[0])
    ext_ref[0:CONV_HALO, :] = jnp.where(i > 0, halo, 0.0)
    ext_ref[CONV_HALO:, :] = glu(u_ref[0])
    first = CONV_HALO - (CONV_WIDTH - 1)
    for c in range(CONV_CH // LANES):
        cs = slice(c * LANES, (c + 1) * LANES)
        acc = jnp.zeros((tl, LANES), F32)
        for j in range(CONV_WIDTH):
            acc = acc + cw_ref[j:j + 1, cs] * ext_ref[first + j:first + j + tl, cs]
        y_ref[:, cs] = acc + cb_ref[:, cs]
    y = y_ref[...]
    mu = jnp.mean(y, axis=-1, keepdims=True)
    yc = y - mu
    var = jnp.mean(yc * yc, axis=-1, keepdims=True)
    yn = yc * lax.rsqrt(var + EPS) * lg_ref[...] + lb_ref[...]
    o_ref[0] = (yn * _sigmoid(yn)).astype(BF16)


def _conv_call(u3, conv_w, conv_b, ln_g, ln_b):
    b, l, _ = u3.shape
    tl = CONV_TL
    halo_blocks = tl // CONV_HALO
    const = lambda bi, i: (0, 0)
    return pl.pallas_call(
        _conv_kernel,
        grid=(b, l // tl),
        in_specs=[
            pl.BlockSpec((1, tl, 2 * CONV_CH), lambda bi, i: (bi, i, 0)),
            pl.BlockSpec((1, CONV_HALO, 2 * CONV_CH),
                         lambda bi, i: (bi, jnp.maximum(i * halo_blocks - 1, 0), 0)),
            pl.BlockSpec((CONV_WIDTH, CONV_CH), const),
            pl.BlockSpec((1, CONV_CH), const),
            pl.BlockSpec((1, CONV_CH), const),
            pl.BlockSpec((1, CONV_CH), const),
        ],
        out_specs=pl.BlockSpec((1, tl, CONV_CH), lambda bi, i: (bi, i, 0)),
        out_shape=jax.ShapeDtypeStruct((b, l, CONV_CH), BF16),
        scratch_shapes=[pltpu.VMEM((CONV_HALO + tl, CONV_CH), F32), pltpu.VMEM((tl, CONV_CH), F32)],
        compiler_params=_cparams("parallel", "parallel"),
        name="conv",
    )(u3, u3, conv_w, conv_b, ln_g, ln_b)


def _sortable_key(score):
    b = lax.bitcast_convert_type(score, jnp.int32)
    return b ^ ((b >> 31) & 0x7FFFFFFF)


KEY_CHUNK = 2 * Q_BLOCK
HALF16 = 1 << 15
ONES_ROWS = 16


def _tree_reduce(x, rows, op):
    parts = [x[j * rows:(j + 1) * rows] for j in range(x.shape[0] // rows)]
    while len(parts) > 1:
        parts = [op(a, b) for a, b in zip(parts[0::2], parts[1::2])]
    return parts[0]


def _attn_kernel(q_ref, qi_ref, wit_ref, k_ref, vt_ref, ki_ref, bias_ref, o_ref,
                 key_ref, khi_ref, klo_ref, madd_ref, thr_ref, x_ref, p_ref, acc_ref, *, topk,
                 max_chunks):
    i = pl.program_id(1)
    qb = Q_BLOCK
    ch = KEY_CHUNK
    nkb = i + 1
    nch = (nkb + 1) // 2
    q_all = q_ref[...].reshape(N_HEADS * qb, LANES)
    qi_st = qi_ref[...].reshape(IDX_HEADS * qb, IDX_DIM)
    wit = wit_ref[...]
    t_loc = lax.broadcasted_iota(jnp.int32, (1, qb), 1)
    limit = ((i * qb + t_loc) // CHUNK + 1) * CHUNK
    idx_scale = (IDX_DIM ** -0.5) * (IDX_HEADS ** -0.5)

    def chunk_rows(c):
        return pl.ds(pl.multiple_of(c * ch, ch), ch)

    def score_body(c, carry):
        rows = chunk_rows(c)
        dots = _dot_nt(ki_ref[rows, :], qi_st)
        sc = jnp.zeros((ch, qb), F32)
        for hh in range(IDX_HEADS):
            sc = sc + jnp.maximum(dots[:, hh * qb:(hh + 1) * qb], 0.0) * wit[hh:hh + 1, :]
        key = _sortable_key(sc * idx_scale)
        s_glob = c * ch + lax.broadcasted_iota(jnp.int32, (ch, qb), 0)
        key = jnp.where(s_glob < limit, key, INT_MIN)
        key_ref[rows, :] = key
        khi_ref[rows, :] = (key >> 16).astype(jnp.int16)
        klo_ref[rows, :] = ((key & 0xFFFF) - HALF16).astype(jnp.int16)
        x_ref[rows, :] = _dot_nt(k_ref[rows, :], q_all)
        return carry

    lax.fori_loop(0, nch, score_body, 0)

    def count32(pred):
        def body(c, acc):
            ind = jnp.where(pred(key_ref[chunk_rows(c), :]), 1, 0)
            return acc + _tree_reduce(ind, acc.shape[0], jnp.add)
        acc = lax.fori_loop(0, nch, body, jnp.zeros((4 * SUBLANES, qb), jnp.int32))
        return jnp.sum(acc, axis=0, keepdims=True)

    def search(n):
        def count16(ref, pred):
            parts = []
            for c in range(n):
                ind = jnp.where(pred(ref[c * ch:(c + 1) * ch, :]), jnp.int16(1), jnp.int16(0))
                parts.append(_tree_reduce(ind, 4 * SUBLANES, jnp.add))
            while len(parts) > 1:
                pairs = [a + b for a, b in zip(parts[0::2], parts[1::2])]
                parts = pairs + parts[len(pairs) * 2:]
            return jnp.sum(parts[0].astype(jnp.int32), axis=0, keepdims=True)

        def kth_largest16(ref, want):
            def bit_body(it, lo):
                cand = lo + jnp.left_shift(jnp.int32(1), 15 - it)
                c16 = cand.astype(jnp.int16)
                return jnp.where(count16(ref, lambda v: v >= c16) >= want, cand, lo)
            return lax.fori_loop(0, 16, bit_body, jnp.full((1, qb), -HALF16, jnp.int32))

        t_hi = kth_largest16(khi_ref, topk)
        t_hi16 = t_hi.astype(jnp.int16)
        n_above = count16(khi_ref, lambda v: v > t_hi16)
        for c in range(n):
            rows = slice(c * ch, (c + 1) * ch)
            klo_ref[rows, :] = jnp.where(khi_ref[rows, :] == t_hi16, klo_ref[rows, :],
                                         jnp.int16(-HALF16))
        t_lo = kth_largest16(klo_ref, topk - n_above)
        thr_ref[0:1, :] = jnp.maximum(jnp.left_shift(t_hi, 16) | (t_lo + HALF16), KEY_MIN_FINITE)

    thr_ref[...] = jnp.full(thr_ref.shape, KEY_MIN_FINITE, jnp.int32)

    @pl.when(nkb * qb > topk)
    def _():
        lax.switch(nch - 1, [functools.partial(search, n + 1) for n in range(max_chunks)])

    thr = thr_ref[0:1, :]
    n_ge = count32(lambda key: key >= thr)
    has_ties = jnp.max(n_ge) > topk

    @pl.when(jnp.logical_not(has_ties))
    def _():
        def body(c, carry):
            rows = chunk_rows(c)
            key = key_ref[rows, :]
            sel = jnp.logical_and(key >= thr, key <= KEY_MAX_FINITE)
            madd_ref[rows, :] = jnp.where(sel, 0.0, NEG_BIG)
            return carry
        lax.fori_loop(0, nch, body, 0)

    @pl.when(has_ties)
    def _():
        n_gt = count32(lambda key: key > thr)
        need = (topk - n_gt).astype(F32)
        lower = (lax.broadcasted_iota(jnp.int32, (ch, ch), 1)
                 < lax.broadcasted_iota(jnp.int32, (ch, ch), 0)).astype(BF16)

        def body(c, seen):
            rows = chunk_rows(c)
            key = key_ref[rows, :]
            eq = key == thr
            eqf = jnp.where(eq, 1.0, 0.0)
            before = _dot(lower, eqf.astype(BF16)) + seen
            sel = jnp.logical_or(key > thr, jnp.logical_and(eq, before < need))
            sel = jnp.logical_and(sel, key <= KEY_MAX_FINITE)
            madd_ref[rows, :] = jnp.where(sel, 0.0, NEG_BIG)
            return seen + jnp.sum(eqf, axis=0, keepdims=True)
        lax.fori_loop(0, nch, body, jnp.zeros((1, qb), F32))

    def mask_rows(rows, tile, mparts):
        madd = madd_ref[rows, :]
        out = []
        for hh in range(N_HEADS):
            cols = slice(hh * qb, (hh + 1) * qb)
            x = x_ref[rows, cols] + madd
            if tile is not None:
                x = x + bias_ref[tile, hh]
            x_ref[rows, cols] = x
            out.append(jnp.maximum(mparts[hh], _tree_reduce(x, SUBLANES, jnp.maximum)))
        return tuple(out)

    def far_body(c, mparts):
        return mask_rows(chunk_rows(c), None, mparts)

    def near_body(c, mparts):
        for half in range(ch // qb):
            jb = c * (ch // qb) + half
            d = i - jb
            tile = jnp.where(d == 0, 0, jnp.where(d == 1, 1, 2))
            rows = pl.ds(pl.multiple_of(jb * qb, qb), qb)
            mparts = mask_rows(rows, tile, mparts)
        return mparts

    n_far = jnp.maximum((i - 1) // 2, 0)
    mparts = tuple(jnp.full((SUBLANES, qb), NEG_BIG, F32) for _ in range(N_HEADS))
    mparts = lax.fori_loop(0, n_far, far_body, mparts)
    mparts = lax.fori_loop(n_far, nch, near_body, mparts)
    m_rows = [jnp.max(mp, axis=0, keepdims=True) for mp in mparts]

    acc_ref[...] = jnp.zeros(acc_ref.shape, F32)
    ones = jnp.ones((ONES_ROWS, ch), BF16)

    def prob_chunk(c):
        rows = chunk_rows(c)
        for hh in range(N_HEADS):
            cols = slice(hh * qb, (hh + 1) * qb)
            p_ref[rows, cols] = jnp.exp2((x_ref[rows, cols] - m_rows[hh]).astype(BF16))

    def pv_chunk(c):
        rows = chunk_rows(c)
        vt2 = jnp.concatenate([vt_ref[2 * c], vt_ref[2 * c + 1]], axis=1)
        for g in range(N_KV):
            lhs = jnp.concatenate([vt2[g * HEAD_DIM:(g + 1) * HEAD_DIM, :], ones], axis=0)
            acc_ref[g] += _dot(lhs, p_ref[rows, g * KV_REP * qb:(g + 1) * KV_REP * qb])

    prob_chunk(0)

    def pv_body(c, carry):
        pv_chunk(c - 1)
        prob_chunk(c)
        return carry

    lax.fori_loop(1, nch, pv_body, 0)
    pv_chunk(nch - 1)

    for hp in range(N_HEADS // 2):
        parts = []
        for hh in (2 * hp, 2 * hp + 1):
            g, r = hh // KV_REP, hh % KV_REP
            cols = slice(r * qb, (r + 1) * qb)
            inv_l = 1.0 / acc_ref[g, HEAD_DIM:HEAD_DIM + 1, cols]
            parts.append(acc_ref[g, 0:HEAD_DIM, cols] * inv_l)
        st = jnp.concatenate(parts, axis=0)
        o_ref[:, hp * LANES:(hp + 1) * LANES] = st.T.astype(BF16)


def _attn_call(q_hm, qi_hm, wit, k2, vt, ki2, bias_tiles, batch, seq):
    t = k2.shape[0]
    nblk = seq // Q_BLOCK
    topk = min(TOPK_MAX, seq // 4)
    qrow = lambda b, i: (0, b * nblk + i, 0)
    return pl.pallas_call(
        functools.partial(_attn_kernel, topk=topk, max_chunks=seq // KEY_CHUNK),
        grid=(batch, nblk),
        in_specs=[
            pl.BlockSpec((N_HEADS, Q_BLOCK, LANES), qrow),
            pl.BlockSpec((IDX_HEADS, Q_BLOCK, IDX_DIM), qrow),
            pl.BlockSpec((IDX_HEADS, Q_BLOCK), lambda b, i: (0, b * nblk + i)),
            pl.BlockSpec((seq, LANES), lambda b, i: (b, 0)),
            pl.BlockSpec((nblk, LANES, Q_BLOCK), lambda b, i: (b, 0, 0)),
            pl.BlockSpec((seq, IDX_DIM), lambda b, i: (b, 0)),
            pl.BlockSpec((3, N_HEADS, Q_BLOCK, Q_BLOCK), lambda b, i: (0, 0, 0, 0)),
        ],
        out_specs=pl.BlockSpec((Q_BLOCK, ATTN_W), lambda b, i: (b * nblk + i, 0)),
        out_shape=jax.ShapeDtypeStruct((t, ATTN_W), BF16),
        scratch_shapes=[
            pltpu.VMEM((seq, Q_BLOCK), jnp.int32),
            pltpu.VMEM((seq, Q_BLOCK), jnp.int16),
            pltpu.VMEM((seq, Q_BLOCK), jnp.int16),
            pltpu.VMEM((seq, Q_BLOCK), F32),
            pltpu.VMEM((SUBLANES, Q_BLOCK), jnp.int32),
            pltpu.VMEM((seq, N_HEADS * Q_BLOCK), F32),
            pltpu.VMEM((seq, N_HEADS * Q_BLOCK), BF16),
            pltpu.VMEM((N_KV, HEAD_DIM + ONES_ROWS, KV_REP * Q_BLOCK), F32),
        ],
        compiler_params=_cparams("parallel", "arbitrary"),
        name="attn",
    )(q_hm, qi_hm, wit, k2, vt, ki2, bias_tiles)


def _rel_bucket(rel):
    nb = REL_BUCKETS // 2
    max_exact = nb // 2
    ret = jnp.where(rel < 0, nb, 0)
    n = jnp.abs(rel)
    nf = jnp.maximum(n, 1).astype(jnp.float32)
    large = max_exact + (jnp.log(nf / max_exact) / math.log(REL_MAX_DIST / max_exact)
                         * (nb - max_exact)).astype(jnp.int32)
    large = jnp.minimum(large, nb - 1)
    return ret + jnp.where(n < max_exact, n, large)


def _bias_tiles(rel_bias):
    s = jnp.arange(Q_BLOCK, dtype=jnp.int32)[:, None]
    tq = jnp.arange(Q_BLOCK, dtype=jnp.int32)[None, :]
    d = jnp.arange(3, dtype=jnp.int32)[:, None, None]
    bucket = _rel_bucket(d * Q_BLOCK + tq - s)
    onehot = bucket[:, None, :, :, None] == jnp.arange(REL_BUCKETS, dtype=jnp.int32)
    tiles = jnp.sum(jnp.where(onehot, rel_bias.T[None, :, None, None, :], 0.0), axis=-1)
    return ((tiles - tiles[2:3, :, :1, :1]) * LOG2E).astype(F32)


def _merge_kernel(x_ref, vc_ref, at_ref, gc_ref, ga_ref, wco_ref, wao_ref, wo_ref, g_ref, wr_ref,
                  br_ref, xo_ref, h_ref, comb_ref):
    y_conv = _dot(vc_ref[...], wco_ref[...])
    y_attn = _dot(at_ref[...], wao_ref[...])
    merged = gc_ref[...].astype(F32) * y_conv + ga_ref[...].astype(F32) * y_attn
    x = x_ref[...] + _dot(merged.astype(BF16), wo_ref[...])
    xo_ref[...] = x
    h = x * lax.rsqrt(jnp.mean(x * x, axis=-1, keepdims=True) + EPS) * g_ref[...]
    hb = h.astype(BF16)
    h_ref[...] = hb

    logits = _dot(hb, wr_ref[...]) + br_ref[...]
    lane = lax.broadcasted_iota(jnp.int32, logits.shape, 1)
    neg_inf = -jnp.inf
    gl = jnp.where(lane < N_GROUPS, logits, neg_inf)
    gmax = jnp.max(gl, axis=-1, keepdims=True)
    grp = jnp.min(jnp.where(gl == gmax, lane, LANES), axis=-1, keepdims=True)
    p_grp = 1.0 / jnp.sum(jnp.exp(gl - gmax), axis=-1, keepdims=True)
    e_lo = ROUTER_E_OFF + grp * EXPERTS_PER_GROUP
    in_grp = jnp.logical_and(lane >= e_lo, lane < e_lo + EXPERTS_PER_GROUP)
    el = jnp.where(in_grp, logits, neg_inf)
    v1 = jnp.max(el, axis=-1, keepdims=True)
    i1 = jnp.min(jnp.where(jnp.logical_and(in_grp, el == v1), lane, LANES), axis=-1, keepdims=True)
    rest = jnp.logical_and(in_grp, lane != i1)
    el2 = jnp.where(rest, logits, neg_inf)
    v2 = jnp.max(el2, axis=-1, keepdims=True)
    i2 = jnp.min(jnp.where(jnp.logical_and(rest, el2 == v2), lane, LANES), axis=-1, keepdims=True)
    e2 = jnp.exp(v2 - v1)
    p1 = p_grp / (1.0 + e2)
    p2 = p_grp * e2 / (1.0 + e2)
    comb_ref[...] = jnp.where(lane == i1, p1, jnp.where(lane == i2, p2, 0.0))


def _merge_call(x2, vc, at, gc, ga, wco, wao, wo, g, wr, br):
    t, d = x2.shape
    tm = MERGE_TM
    row = lambda i: (i, 0)
    const = lambda i: (0, 0)
    return pl.pallas_call(
        _merge_kernel,
        grid=(t // tm,),
        in_specs=[
            pl.BlockSpec((tm, d), row),
            pl.BlockSpec((tm, CONV_CH), row),
            pl.BlockSpec((tm, ATTN_W), row),
            pl.BlockSpec((tm, d), row),
            pl.BlockSpec((tm, d), row),
            pl.BlockSpec((CONV_CH, d), const),
            pl.BlockSpec((ATTN_W, d), const),
            pl.BlockSpec((d, d), const),
            pl.BlockSpec((1, d), const),
            pl.BlockSpec((d, LANES), const),
            pl.BlockSpec((1, LANES), const),
        ],
        out_specs=(pl.BlockSpec((tm, d), row), pl.BlockSpec((tm, d), row),
                   pl.BlockSpec((tm, LANES), row)),
        out_shape=(jax.ShapeDtypeStruct((t, d), F32), jax.ShapeDtypeStruct((t, d), BF16),
                   jax.ShapeDtypeStruct((t, LANES), F32)),
        compiler_params=_cparams("parallel"),
        name="merge",
    )(x2, vc, at, gc, ga, wco, wao, wo, g, wr, br)


def _pack_router(w_gr, b_gr, w_er, b_er):
    d = w_gr.shape[0]
    w = jnp.zeros((d, LANES), F32)
    w = w.at[:, :N_GROUPS].set(w_gr)
    w = w.at[:, ROUTER_E_OFF:ROUTER_E_OFF + N_EXPERTS].set(
        jnp.moveaxis(w_er, 0, 1).reshape(d, N_EXPERTS))
    b = jnp.zeros((1, LANES), F32)
    b = b.at[0, :N_GROUPS].set(b_gr)
    b = b.at[0, ROUTER_E_OFF:ROUTER_E_OFF + N_EXPERTS].set(b_er.reshape(N_EXPERTS))
    return w.astype(BF16), b


def _moe_kernel(h_ref, comb_ref, x_ref, wg_ref, wu_ref, wd_ref, o_ref):
    e = pl.program_id(1)

    @pl.when(e == 0)
    def _():
        o_ref[...] = x_ref[...]

    hb = h_ref[...]
    comb = comb_ref[...]
    lane = lax.broadcasted_iota(jnp.int32, comb.shape, 1)
    mids = []
    for ee in range(MOE_EB):
        col = ROUTER_E_OFF + e * MOE_EB + ee
        c = jnp.sum(jnp.where(lane == col, comb, 0.0), axis=-1, keepdims=True)
        a = _dot(hb, wg_ref[ee])
        b = _dot(hb, wu_ref[ee])
        mids.append((a * _sigmoid(a) * b * c).astype(BF16))
    mid = jnp.concatenate(mids, axis=1)
    o_ref[...] += _dot(mid, wd_ref[...].reshape(MOE_EB * EXPERT_FF, wd_ref.shape[-1]))


def _moe_call(h2, comb, x2, wg, wu, wd):
    t, d = x2.shape
    tm = MOE_TM
    row = lambda i, e: (i, 0)
    wmap = lambda i, e: (e, 0, 0)
    return pl.pallas_call(
        _moe_kernel,
        grid=(t // tm, N_EXPERTS // MOE_EB),
        in_specs=[
            pl.BlockSpec((tm, d), row),
            pl.BlockSpec((tm, LANES), row),
            pl.BlockSpec((tm, d), row),
            pl.BlockSpec((MOE_EB, d, EXPERT_FF), wmap),
            pl.BlockSpec((MOE_EB, d, EXPERT_FF), wmap),
            pl.BlockSpec((MOE_EB, EXPERT_FF, d), wmap),
        ],
        out_specs=pl.BlockSpec((tm, d), row),
        out_shape=jax.ShapeDtypeStruct((t, d), F32),
        compiler_params=_cparams("parallel", "arbitrary"),
        name="moe",
    )(h2, comb, x2, wg, wu, wd)


def _norm_kernel(x_ref, g_ref, o_ref):
    x = x_ref[...]
    o_ref[...] = x * lax.rsqrt(jnp.mean(x * x, axis=-1, keepdims=True) + EPS) * g_ref[...]


def _norm_call(x2, g):
    t, d = x2.shape
    tm = NORM_TM
    return pl.pallas_call(
        _norm_kernel,
        grid=(t // tm,),
        in_specs=[pl.BlockSpec((tm, d), lambda i: (i, 0)), pl.BlockSpec((1, d), lambda i: (0, 0))],
        out_specs=pl.BlockSpec((tm, d), lambda i: (i, 0)),
        out_shape=jax.ShapeDtypeStruct((t, d), F32),
        compiler_params=_cparams("parallel"),
        name="final_norm",
    )(x2, g)


def kernel(x, g_mix, w_in, conv_w, conv_b, conv_ln_g, conv_ln_b, w_conv_out, w_attn_out, w_out, rel_bias, g_ffn, w_group_router, b_group_router, w_expert_router, b_expert_router, w_e_gate, w_e_up, w_e_down, g_final):
    batch, seq, d = x.shape
    depth = g_mix.shape[0]
    t = batch * seq
    assert seq % CONV_TL == 0 and seq % KEY_CHUNK == 0 and t % MOE_TM == 0
    x2 = x.reshape(t, d)
    bias_tiles = _bias_tiles(rel_bias)
    for l in range(depth):
        w_pack = _pack_proj_weights(w_in[l])
        u, q_hm, k2, vt, qi_hm, ki2, wit, gc, ga = _proj_call(x2, g_mix[l][None, :], w_pack)
        vc = _conv_call(u.reshape(batch, seq, 2 * CONV_CH), conv_w[l], conv_b[l][None, :],
                        conv_ln_g[l][None, :], conv_ln_b[l][None, :]).reshape(t, CONV_CH)
        at = _attn_call(q_hm, qi_hm, wit, k2, vt, ki2, bias_tiles, batch, seq)
        wr, br = _pack_router(w_group_router[l], b_group_router[l], w_expert_router[l],
                              b_expert_router[l])
        x_mid, h2, comb = _merge_call(x2, vc, at, gc, ga, w_conv_out[l].astype(BF16),
                                      w_attn_out[l].astype(BF16), w_out[l].astype(BF16),
                                      g_ffn[l][None, :], wr, br)
        x2 = _moe_call(h2, comb, x_mid, w_e_gate[l].astype(BF16), w_e_up[l].astype(BF16),
                       w_e_down[l].astype(BF16))
    return _norm_call(x2, g_final[None, :]).reshape(batch, seq, d)
```

```python
import functools
import math

import jax
import jax.numpy as jnp
import numpy as np
from jax import lax
from jax.experimental import pallas as pl
from jax.experimental.pallas import tpu as pltpu

CHUNK = 64
CONV_CH = 512
CONV_WIDTH = 31
N_HEADS = 8
HEAD_DIM = 64
N_KV = 2
KV_REP = N_HEADS // N_KV
ATTN_W = N_HEADS * HEAD_DIM
IDX_HEADS = 8
IDX_DIM = 32
TOPK_MAX = 256
Q_BLOCK = 128
REL_BUCKETS = 32
REL_MAX_DIST = 128
N_GROUPS = 4
EXPERTS_PER_GROUP = 8
N_EXPERTS = N_GROUPS * EXPERTS_PER_GROUP
EXPERT_FF = 256
EPS = 1e-6

LANES = 128
SUBLANES = 8
VMEM_LIMIT_BYTES = 56 * 1024 * 1024

LOG2E = math.log2(math.e)
NEG_BIG = -1e30
INT_MIN = -(2 ** 31)
KEY_MIN_FINITE = -0x7F800000
KEY_MAX_FINITE = 0x7F7FFFFF

BF16 = jnp.bfloat16
F32 = jnp.float32

PROJ_TM = 512
CONV_TL = 256
CONV_HALO = 32
MERGE_TM = 512
MOE_TM = 1024
MOE_EB = 4
NORM_TM = 1024

ROUTER_E_OFF = 32


def _cparams(*sem):
    return pltpu.CompilerParams(dimension_semantics=sem, vmem_limit_bytes=VMEM_LIMIT_BYTES)


def _sigmoid(x):
    return 1.0 / (1.0 + jnp.exp(-x))


def _dot(a, b):
    return jnp.dot(a, b, preferred_element_type=F32)


def _dot_nt(a, b):
    return lax.dot_general(a, b, (((1,), (1,)), ((), ())), preferred_element_type=F32)


def _dot_tn(a, b):
    return lax.dot_general(a, b, (((0,), (0,)), ((), ())), preferred_element_type=F32)


SEG_U = (0, 2 * CONV_CH)
SEG_Q = (SEG_U[1], SEG_U[1] + N_HEADS * LANES)
SEG_K = (SEG_Q[1], SEG_Q[1] + LANES)
SEG_V = (SEG_K[1], SEG_K[1] + LANES)
SEG_QI = (SEG_V[1], SEG_V[1] + IDX_HEADS * IDX_DIM)
SEG_KI = (SEG_QI[1], SEG_QI[1] + LANES)
SEG_WI = (SEG_KI[1], SEG_KI[1] + LANES)
SEG_GC = (SEG_WI[1], SEG_WI[1] + 1024)
SEG_GA = (SEG_GC[1], SEG_GC[1] + 1024)
PROJ_COLS = SEG_GA[1]


def _proj_kernel(*refs, has_y):
    if has_y:
        x_ref, y_ref, g_ref, w_ref, xo_ref = refs[:5]
        x = x_ref[...] + y_ref[...].astype(F32)
        xo_ref[...] = x
    else:
        x_ref, g_ref, w_ref = refs[:3]
        x = x_ref[...]
    u_ref, q_ref, k_ref, vt_ref, qi_ref, ki_ref, wit_ref, gc_ref, ga_ref = refs[-9:]
    h = x * lax.rsqrt(jnp.mean(x * x, axis=-1, keepdims=True) + EPS) * g_ref[...]
    hb = h.astype(BF16)

    def seg(s):
        return _dot(hb, w_ref[:, s[0]:s[1]])

    u_ref[...] = seg(SEG_U).astype(BF16)
    pq = seg(SEG_Q)
    for hh in range(N_HEADS):
        q_ref[hh] = pq[:, hh * LANES:(hh + 1) * LANES].astype(BF16)
    k_ref[...] = seg(SEG_K).astype(BF16)
    pv = seg(SEG_V)
    tm = pv.shape[0]
    for c in range(tm // LANES):
        vt_ref[c] = pv[c * LANES:(c + 1) * LANES, :].T.astype(BF16)
    pqi = seg(SEG_QI)
    for hh in range(IDX_HEADS):
        qi_ref[hh] = pqi[:, hh * IDX_DIM:(hh + 1) * IDX_DIM].astype(BF16)
    ki_ref[...] = seg(SEG_KI)[:, :IDX_DIM].astype(BF16)
    pwi = seg(SEG_WI)
    for c in range(tm // LANES):
        wit_ref[:, c * LANES:(c + 1) * LANES] = pwi[c * LANES:(c + 1) * LANES, :].T[:IDX_HEADS, :]
    gc_ref[...] = _sigmoid(seg(SEG_GC)).astype(BF16)
    ga_ref[...] = _sigmoid(seg(SEG_GA)).astype(BF16)


def _proj_call(x2, y2, g, w_pack):
    t, d = x2.shape
    tm = PROJ_TM
    nq = t // Q_BLOCK
    row = lambda i: (i, 0)
    has_y = y2 is not None
    out_shape = (jax.ShapeDtypeStruct((t, d), F32),) if has_y else ()
    out_specs = (pl.BlockSpec((tm, d), row),) if has_y else ()
    out_shape += (
        jax.ShapeDtypeStruct((t, 2 * CONV_CH), BF16),
        jax.ShapeDtypeStruct((N_HEADS, t, LANES), BF16),
        jax.ShapeDtypeStruct((t, LANES), BF16),
        jax.ShapeDtypeStruct((nq, LANES, Q_BLOCK), BF16),
        jax.ShapeDtypeStruct((IDX_HEADS, t, IDX_DIM), BF16),
        jax.ShapeDtypeStruct((t, IDX_DIM), BF16),
        jax.ShapeDtypeStruct((IDX_HEADS, t), F32),
        jax.ShapeDtypeStruct((t, d), BF16),
        jax.ShapeDtypeStruct((t, d), BF16),
    )
    out_specs += (
        pl.BlockSpec((tm, 2 * CONV_CH), row),
        pl.BlockSpec((N_HEADS, tm, LANES), lambda i: (0, i, 0)),
        pl.BlockSpec((tm, LANES), row),
        pl.BlockSpec((tm // Q_BLOCK, LANES, Q_BLOCK), lambda i: (i, 0, 0)),
        pl.BlockSpec((IDX_HEADS, tm, IDX_DIM), lambda i: (0, i, 0)),
        pl.BlockSpec((tm, IDX_DIM), row),
        pl.BlockSpec((IDX_HEADS, tm), lambda i: (0, i)),
        pl.BlockSpec((tm, d), row),
        pl.BlockSpec((tm, d), row),
    )
    streams = (x2, y2) if has_y else (x2,)
    return pl.pallas_call(
        functools.partial(_proj_kernel, has_y=has_y),
        grid=(t // tm,),
        in_specs=[pl.BlockSpec((tm, d), row) for _ in streams] + [
            pl.BlockSpec((1, d), lambda i: (0, 0)),
            pl.BlockSpec((d, PROJ_COLS), lambda i: (0, 0)),
        ],
        out_specs=out_specs,
        out_shape=out_shape,
        compiler_params=_cparams("parallel"),
        name="proj",
    )(*streams, g, w_pack)


def _pack_proj_weights(w_in):
    d = w_in.shape[0]
    splits = (2 * CONV_CH, ATTN_W, N_KV * HEAD_DIM, N_KV * HEAD_DIM, IDX_HEADS * IDX_DIM,
              IDX_DIM, IDX_HEADS, d, d)
    offs = np.concatenate([[0], np.cumsum(splits)])
    w_u, w_q, w_k, w_v, w_qi, w_ki, w_wi, w_gc, w_ga = [
        w_in[:, int(offs[i]):int(offs[i + 1])] for i in range(len(splits))]
    w_q = (w_q * (HEAD_DIM ** -0.5 * LOG2E)).reshape(d, N_KV, KV_REP, HEAD_DIM)
    q_pad = jnp.zeros((d, N_KV, KV_REP, N_KV, HEAD_DIM), w_in.dtype)
    for gidx in range(N_KV):
        q_pad = q_pad.at[:, gidx, :, gidx, :].set(w_q[:, gidx])
    q_pad = q_pad.reshape(d, N_HEADS * LANES)
    pad = lambda w: jnp.pad(w, ((0, 0), (0, LANES - w.shape[1])))
    packed = jnp.concatenate([w_u, q_pad, w_k, w_v, w_qi, pad(w_ki), pad(w_wi), w_gc, w_ga], axis=1)
    assert packed.shape[1] == PROJ_COLS
    return packed.astype(BF16)


def _conv_kernel(u_ref, halo_ref, cw_ref, cb_ref, lg_ref, lb_ref, o_ref, ext_ref, y_ref):
    i = pl.program_id(1)
    tl = u_ref.shape[1]

    def glu(u):
        u = u.astype(F32)
        return u[:, :CONV_CH] * _sigmoid(u[:, CONV_CH:])

    halo = glu(halo_ref[0])
    ext_ref[0:CONV_HALO, :] = jnp.where(i > 0, halo, 0.0)
    ext_ref[CONV_HALO:, :] = glu(u_ref[0])
    first = CONV_HALO - (CONV_WIDTH - 1)
    for c in range(CONV_CH // LANES):
        cs = slice(c * LANES, (c + 1) * LANES)
        acc = jnp.zeros((tl, LANES), F32)
        for j in range(CONV_WIDTH):
            acc = acc + cw_ref[j:j + 1, cs] * ext_ref[first + j:first + j + tl, cs]
        y_ref[:, cs] = acc + cb_ref[:, cs]
    y = y_ref[...]
    mu = jnp.mean(y, axis=-1, keepdims=True)
    yc = y - mu
    var = jnp.mean(yc * yc, axis=-1, keepdims=True)
    yn = yc * lax.rsqrt(var + EPS) * lg_ref[...] + lb_ref[...]
    o_ref[0] = (yn * _sigmoid(yn)).astype(BF16)


def _conv_call(u3, conv_w, conv_b, ln_g, ln_b):
    b, l, _ = u3.shape
    tl = CONV_TL
    halo_blocks = tl // CONV_HALO
    const = lambda bi, i: (0, 0)
    return pl.pallas_call(
        _conv_kernel,
        grid=(b, l // tl),
        in_specs=[
            pl.BlockSpec((1, tl, 2 * CONV_CH), lambda bi, i: (bi, i, 0)),
            pl.BlockSpec((1, CONV_HALO, 2 * CONV_CH),
                         lambda bi, i: (bi, jnp.maximum(i * halo_blocks - 1, 0), 0)),
            pl.BlockSpec((CONV_WIDTH, CONV_CH), const),
            pl.BlockSpec((1, CONV_CH), const),
            pl.BlockSpec((1, CONV_CH), const),
            pl.BlockSpec((1, CONV_CH), const),
        ],
        out_specs=pl.BlockSpec((1, tl, CONV_CH), lambda bi, i: (bi, i, 0)),
        out_shape=jax.ShapeDtypeStruct((b, l, CONV_CH), BF16),
        scratch_shapes=[pltpu.VMEM((CONV_HALO + tl, CONV_CH), F32), pltpu.VMEM((tl, CONV_CH), F32)],
        compiler_params=_cparams("parallel", "parallel"),
        name="conv",
    )(u3, u3, conv_w, conv_b, ln_g, ln_b)


def _sortable_key(score):
    b = lax.bitcast_convert_type(score, jnp.int32)
    return b ^ ((b >> 31) & 0x7FFFFFFF)


KEY_CHUNK = 2 * Q_BLOCK
HALF16 = 1 << 15
ONES_ROWS = 16


def _tree_reduce(x, rows, op):
    parts = [x[j * rows:(j + 1) * rows] for j in range(x.shape[0] // rows)]
    while len(parts) > 1:
        parts = [op(a, b) for a, b in zip(parts[0::2], parts[1::2])]
    return parts[0]


def _attn_kernel(q_ref, qi_ref, wit_ref, k_ref, vt_ref, ki_ref, bias_ref, o_ref,
                 key_ref, khi_ref, klo_ref, madd_ref, thr_ref, x_ref, p_ref, acc_ref, *, topk,
                 max_chunks):
    i = pl.program_id(1)
    qb = Q_BLOCK
    ch = KEY_CHUNK
    nkb = i + 1
    nch = (nkb + 1) // 2
    q_all = q_ref[...].reshape(N_HEADS * qb, LANES)
    qi_st = qi_ref[...].reshape(IDX_HEADS * qb, IDX_DIM)
    wit = wit_ref[...]
    t_loc = lax.broadcasted_iota(jnp.int32, (1, qb), 1)
    limit = ((i * qb + t_loc) // CHUNK + 1) * CHUNK
    idx_scale = (IDX_DIM ** -0.5) * (IDX_HEADS ** -0.5)

    def chunk_rows(c):
        return pl.ds(pl.multiple_of(c * ch, ch), ch)

    def score_body(c, carry):
        rows = chunk_rows(c)
        dots = _dot_nt(ki_ref[rows, :], qi_st)
        sc = jnp.zeros((ch, qb), F32)
        for hh in range(IDX_HEADS):
            sc = sc + jnp.maximum(dots[:, hh * qb:(hh + 1) * qb], 0.0) * wit[hh:hh + 1, :]
        key = _sortable_key(sc * idx_scale)
        s_glob = c * ch + lax.broadcasted_iota(jnp.int32, (ch, qb), 0)
        key = jnp.where(s_glob < limit, key, INT_MIN)
        key_ref[rows, :] = key
        khi_ref[rows, :] = (key >> 16).astype(jnp.int16)
        klo_ref[rows, :] = ((key & 0xFFFF) - HALF16).astype(jnp.int16)
        x_ref[rows, :] = _dot_nt(k_ref[rows, :], q_all)
        return carry

    lax.fori_loop(0, nch, score_body, 0)

    def count32(pred):
        def body(c, acc):
            ind = jnp.where(pred(key_ref[chunk_rows(c), :]), 1, 0)
            return acc + _tree_reduce(ind, acc.shape[0], jnp.add)
        acc = lax.fori_loop(0, nch, body, jnp.zeros((4 * SUBLANES, qb), jnp.int32))
        return jnp.sum(acc, axis=0, keepdims=True)

    def search(n):
        def count16(ref, pred):
            parts = []
            for c in range(n):
                ind = jnp.where(pred(ref[c * ch:(c + 1) * ch, :]), jnp.int16(1), jnp.int16(0))
                parts.append(_tree_reduce(ind, 4 * SUBLANES, jnp.add))
            while len(parts) > 1:
                pairs = [a + b for a, b in zip(parts[0::2], parts[1::2])]
                parts = pairs + parts[len(pairs) * 2:]
            return jnp.sum(parts[0].astype(jnp.int32), axis=0, keepdims=True)

        def kth_largest16(ref, want):
            def bit_body(it, lo):
                cand = lo + jnp.left_shift(jnp.int32(1), 15 - it)
                c16 = cand.astype(jnp.int16)
                return jnp.where(count16(ref, lambda v: v >= c16) >= want, cand, lo)
            return lax.fori_loop(0, 16, bit_body, jnp.full((1, qb), -HALF16, jnp.int32))

        t_hi = kth_largest16(khi_ref, topk)
        t_hi16 = t_hi.astype(jnp.int16)
        n_above = count16(khi_ref, lambda v: v > t_hi16)
        for c in range(n):
            rows = slice(c * ch, (c + 1) * ch)
            klo_ref[rows, :] = jnp.where(khi_ref[rows, :] == t_hi16, klo_ref[rows, :],
                                         jnp.int16(-HALF16))
        t_lo = kth_largest16(klo_ref, topk - n_above)
        thr_ref[0:1, :] = jnp.maximum(jnp.left_shift(t_hi, 16) | (t_lo + HALF16), KEY_MIN_FINITE)

    thr_ref[...] = jnp.full(thr_ref.shape, KEY_MIN_FINITE, jnp.int32)

    @pl.when(nkb * qb > topk)
    def _():
        lax.switch(nch - 1, [functools.partial(search, n + 1) for n in range(max_chunks)])

    thr = thr_ref[0:1, :]
    n_ge = count32(lambda key: key >= thr)
    has_ties = jnp.max(n_ge) > topk

    @pl.when(jnp.logical_not(has_ties))
    def _():
        def body(c, carry):
            rows = chunk_rows(c)
            key = key_ref[rows, :]
            sel = jnp.logical_and(key >= thr, key <= KEY_MAX_FINITE)
            madd_ref[rows, :] = jnp.where(sel, 0.0, NEG_BIG)
            return carry
        lax.fori_loop(0, nch, body, 0)

    @pl.when(has_ties)
    def _():
        n_gt = count32(lambda key: key > thr)
        need = (topk - n_gt).astype(F32)
        lower = (lax.broadcasted_iota(jnp.int32, (ch, ch), 1)
                 < lax.broadcasted_iota(jnp.int32, (ch, ch), 0)).astype(BF16)

        def body(c, seen):
            rows = chunk_rows(c)
            key = key_ref[rows, :]
            eq = key == thr
            eqf = jnp.where(eq, 1.0, 0.0)
            before = _dot(lower, eqf.astype(BF16)) + seen
            sel = jnp.logical_or(key > thr, jnp.logical_and(eq, before < need))
            sel = jnp.logical_and(sel, key <= KEY_MAX_FINITE)
            madd_ref[rows, :] = jnp.where(sel, 0.0, NEG_BIG)
            return seen + jnp.sum(eqf, axis=0, keepdims=True)
        lax.fori_loop(0, nch, body, jnp.zeros((1, qb), F32))

    def mask_rows(rows, tile, mparts):
        madd = madd_ref[rows, :]
        out = []
        for hh in range(N_HEADS):
            cols = slice(hh * qb, (hh + 1) * qb)
            x = x_ref[rows, cols] + madd
            if tile is not None:
                x = x + bias_ref[tile, hh]
            x_ref[rows, cols] = x
            out.append(jnp.maximum(mparts[hh], _tree_reduce(x, SUBLANES, jnp.maximum)))
        return tuple(out)

    def far_body(c, mparts):
        return mask_rows(chunk_rows(c), None, mparts)

    def near_body(c, mparts):
        for half in range(ch // qb):
            jb = c * (ch // qb) + half
            d = i - jb
            tile = jnp.where(d == 0, 0, jnp.where(d == 1, 1, 2))
            rows = pl.ds(pl.multiple_of(jb * qb, qb), qb)
            mparts = mask_rows(rows, tile, mparts)
        return mparts

    n_far = jnp.maximum((i - 1) // 2, 0)
    mparts = tuple(jnp.full((SUBLANES, qb), NEG_BIG, F32) for _ in range(N_HEADS))
    mparts = lax.fori_loop(0, n_far, far_body, mparts)
    mparts = lax.fori_loop(n_far, nch, near_body, mparts)
    m_rows = [jnp.max(mp, axis=0, keepdims=True) for mp in mparts]

    acc_ref[...] = jnp.zeros(acc_ref.shape, F32)
    ones = jnp.ones((ONES_ROWS, ch), BF16)

    def prob_chunk(c):
        rows = chunk_rows(c)
        for hh in range(N_HEADS):
            cols = slice(hh * qb, (hh + 1) * qb)
            p_ref[rows, cols] = jnp.exp2((x_ref[rows, cols] - m_rows[hh]).astype(BF16))

    def pv_chunk(c):
        rows = chunk_rows(c)
        vt2 = jnp.concatenate([vt_ref[2 * c], vt_ref[2 * c + 1]], axis=1)
        for g in range(N_KV):
            lhs = jnp.concatenate([vt2[g * HEAD_DIM:(g + 1) * HEAD_DIM, :], ones], axis=0)
            acc_ref[g] += _dot(lhs, p_ref[rows, g * KV_REP * qb:(g + 1) * KV_REP * qb])

    prob_chunk(0)

    def pv_body(c, carry):
        pv_chunk(c - 1)
        prob_chunk(c)
        return carry

    lax.fori_loop(1, nch, pv_body, 0)
    pv_chunk(nch - 1)

    for hp in range(N_HEADS // 2):
        parts = []
        for hh in (2 * hp, 2 * hp + 1):
            g, r = hh // KV_REP, hh % KV_REP
            cols = slice(r * qb, (r + 1) * qb)
            inv_l = 1.0 / acc_ref[g, HEAD_DIM:HEAD_DIM + 1, cols]
            parts.append(acc_ref[g, 0:HEAD_DIM, cols] * inv_l)
        st = jnp.concatenate(parts, axis=0)
        o_ref[:, hp * LANES:(hp + 1) * LANES] = st.T.astype(BF16)


def _attn_call(q_hm, qi_hm, wit, k2, vt, ki2, bias_tiles, batch, seq):
    t = k2.shape[0]
    nblk = seq // Q_BLOCK
    topk = min(TOPK_MAX, seq // 4)
    qrow = lambda b, i: (0, b * nblk + i, 0)
    return pl.pallas_call(
        functools.partial(_attn_kernel, topk=topk, max_chunks=seq // KEY_CHUNK),
        grid=(batch, nblk),
        in_specs=[
            pl.BlockSpec((N_HEADS, Q_BLOCK, LANES), qrow),
            pl.BlockSpec((IDX_HEADS, Q_BLOCK, IDX_DIM), qrow),
            pl.BlockSpec((IDX_HEADS, Q_BLOCK), lambda b, i: (0, b * nblk + i)),
            pl.BlockSpec((seq, LANES), lambda b, i: (b, 0)),
            pl.BlockSpec((nblk, LANES, Q_BLOCK), lambda b, i: (b, 0, 0)),
            pl.BlockSpec((seq, IDX_DIM), lambda b, i: (b, 0)),
            pl.BlockSpec((3, N_HEADS, Q_BLOCK, Q_BLOCK), lambda b, i: (0, 0, 0, 0)),
        ],
        out_specs=pl.BlockSpec((Q_BLOCK, ATTN_W), lambda b, i: (b * nblk + i, 0)),
        out_shape=jax.ShapeDtypeStruct((t, ATTN_W), BF16),
        scratch_shapes=[
            pltpu.VMEM((seq, Q_BLOCK), jnp.int32),
            pltpu.VMEM((seq, Q_BLOCK), jnp.int16),
            pltpu.VMEM((seq, Q_BLOCK), jnp.int16),
            pltpu.VMEM((seq, Q_BLOCK), F32),
            pltpu.VMEM((SUBLANES, Q_BLOCK), jnp.int32),
            pltpu.VMEM((seq, N_HEADS * Q_BLOCK), F32),
            pltpu.VMEM((seq, N_HEADS * Q_BLOCK), BF16),
            pltpu.VMEM((N_KV, HEAD_DIM + ONES_ROWS, KV_REP * Q_BLOCK), F32),
        ],
        compiler_params=_cparams("parallel", "arbitrary"),
        name="attn",
    )(q_hm, qi_hm, wit, k2, vt, ki2, bias_tiles)


def _rel_bucket(rel):
    nb = REL_BUCKETS // 2
    max_exact = nb // 2
    ret = jnp.where(rel < 0, nb, 0)
    n = jnp.abs(rel)
    nf = jnp.maximum(n, 1).astype(jnp.float32)
    large = max_exact + (jnp.log(nf / max_exact) / math.log(REL_MAX_DIST / max_exact)
                         * (nb - max_exact)).astype(jnp.int32)
    large = jnp.minimum(large, nb - 1)
    return ret + jnp.where(n < max_exact, n, large)


def _bias_tiles(rel_bias):
    s = jnp.arange(Q_BLOCK, dtype=jnp.int32)[:, None]
    tq = jnp.arange(Q_BLOCK, dtype=jnp.int32)[None, :]
    d = jnp.arange(3, dtype=jnp.int32)[:, None, None]
    bucket = _rel_bucket(d * Q_BLOCK + tq - s)
    onehot = bucket[:, None, :, :, None] == jnp.arange(REL_BUCKETS, dtype=jnp.int32)
    tiles = jnp.sum(jnp.where(onehot, rel_bias.T[None, :, None, None, :], 0.0), axis=-1)
    return ((tiles - tiles[2:3, :, :1, :1]) * LOG2E).astype(F32)


AUG_ROWS = 48


def _merge_kernel(x_ref, vc_ref, at_ref, gc_ref, ga_ref, wco_ref, wao_ref, wo_ref, g_ref, wr_ref,
                  br_ref, xo_ref, ht_ref, gone_ref):
    y_conv = _dot(vc_ref[...], wco_ref[...])
    y_attn = _dot(at_ref[...], wao_ref[...])
    merged = gc_ref[...].astype(F32) * y_conv + ga_ref[...].astype(F32) * y_attn
    x = x_ref[...] + _dot(merged.astype(BF16), wo_ref[...])
    xo_ref[...] = x
    h = x * lax.rsqrt(jnp.mean(x * x, axis=-1, keepdims=True) + EPS) * g_ref[...]
    hb = h.astype(BF16)
    d = h.shape[1]
    ht_ref[0:d, :] = hb.astype(F32).T.astype(BF16)

    logits = _dot(hb, wr_ref[...]) + br_ref[...]
    lane = lax.broadcasted_iota(jnp.int32, logits.shape, 1)
    neg_inf = -jnp.inf
    gl = jnp.where(lane < N_GROUPS, logits, neg_inf)
    gmax = jnp.max(gl, axis=-1, keepdims=True)
    grp = jnp.min(jnp.where(gl == gmax, lane, LANES), axis=-1, keepdims=True)
    p_grp = 1.0 / jnp.sum(jnp.exp(gl - gmax), axis=-1, keepdims=True)
    e_lo = ROUTER_E_OFF + grp * EXPERTS_PER_GROUP
    in_grp = jnp.logical_and(lane >= e_lo, lane < e_lo + EXPERTS_PER_GROUP)
    el = jnp.where(in_grp, logits, neg_inf)
    v1 = jnp.max(el, axis=-1, keepdims=True)
    i1 = jnp.min(jnp.where(jnp.logical_and(in_grp, el == v1), lane, LANES), axis=-1, keepdims=True)
    rest = jnp.logical_and(in_grp, lane != i1)
    el2 = jnp.where(rest, logits, neg_inf)
    v2 = jnp.max(el2, axis=-1, keepdims=True)
    i2 = jnp.min(jnp.where(jnp.logical_and(rest, el2 == v2), lane, LANES), axis=-1, keepdims=True)
    e2 = jnp.exp(v2 - v1)
    p1 = p_grp / (1.0 + e2)
    p2 = p_grp * e2 / (1.0 + e2)
    gone_ref[...] = jnp.where(lane == grp, 1.0, 0.0).astype(BF16)
    w8 = jnp.where(lane == i1 - e_lo, p1, jnp.where(lane == i2 - e_lo, p2, 0.0))
    w8t = w8.T[0:16, :]
    hi = w8t.astype(BF16)
    r1 = w8t - hi.astype(F32)
    mid = r1.astype(BF16)
    lo = (r1 - mid.astype(F32)).astype(BF16)
    ht_ref[d:d + 16, :] = hi
    ht_ref[d + 16:d + 32, :] = mid
    ht_ref[d + 32:d + 48, :] = lo


def _merge_call(x2, vc, at, gc, ga, wco, wao, wo, g, wr, br):
    t, d = x2.shape
    tm = MERGE_TM
    row = lambda i: (i, 0)
    const = lambda i: (0, 0)
    return pl.pallas_call(
        _merge_kernel,
        grid=(t // tm,),
        in_specs=[
            pl.BlockSpec((tm, d), row),
            pl.BlockSpec((tm, CONV_CH), row),
            pl.BlockSpec((tm, ATTN_W), row),
            pl.BlockSpec((tm, d), row),
            pl.BlockSpec((tm, d), row),
            pl.BlockSpec((CONV_CH, d), const),
            pl.BlockSpec((ATTN_W, d), const),
            pl.BlockSpec((d, d), const),
            pl.BlockSpec((1, d), const),
            pl.BlockSpec((d, LANES), const),
            pl.BlockSpec((1, LANES), const),
        ],
        out_specs=(pl.BlockSpec((tm, d), row), pl.BlockSpec((d + AUG_ROWS, tm), lambda i: (0, i)),
                   pl.BlockSpec((tm, LANES), row)),
        out_shape=(jax.ShapeDtypeStruct((t, d), F32),
                   jax.ShapeDtypeStruct((d + AUG_ROWS, t), BF16),
                   jax.ShapeDtypeStruct((t, LANES), BF16)),
        compiler_params=_cparams("parallel"),
        name="merge",
    )(x2, vc, at, gc, ga, wco, wao, wo, g, wr, br)


def _pack_router(w_gr, b_gr, w_er, b_er):
    d = w_gr.shape[0]
    w = jnp.zeros((d, LANES), F32)
    w = w.at[:, :N_GROUPS].set(w_gr)
    w = w.at[:, ROUTER_E_OFF:ROUTER_E_OFF + N_EXPERTS].set(
        jnp.moveaxis(w_er, 0, 1).reshape(d, N_EXPERTS))
    b = jnp.zeros((1, LANES), F32)
    b = b.at[0, :N_GROUPS].set(b_gr)
    b = b.at[0, ROUTER_E_OFF:ROUTER_E_OFF + N_EXPERTS].set(b_er.reshape(N_EXPERTS))
    return w.astype(BF16), b


SLOT_CHUNK = 256
FFN_SPLIT = 8
MOE_MAX_CHUNKS = MOE_TM // SLOT_CHUNK + N_GROUPS - 1
MOE_STEPS = MOE_MAX_CHUNKS + 1


def _plan_kernel(gone_ref, ltri_ref, utri_ref, slot_ref, cnt_ref):
    gone = gone_ref[...]
    before = _dot(ltri_ref[...], gone)
    last = gone.shape[0] - 1
    cnt = before[last:last + 1, :] + gone[last:last + 1, :].astype(F32)
    nchunks = jnp.floor((cnt + (SLOT_CHUNK - 1)) * (1.0 / SLOT_CHUNK))
    start = _dot(jnp.broadcast_to(nchunks, (SUBLANES, LANES)).astype(BF16), utri_ref[...])[0:1, :]
    slot = jnp.sum(gone.astype(F32) * (before + start * SLOT_CHUNK), axis=-1, keepdims=True)
    slot_ref[...] = jnp.broadcast_to(slot.astype(jnp.int32), slot_ref.shape)
    cnt_ref[0] = jnp.broadcast_to(cnt, (SUBLANES, LANES)).astype(jnp.int32)


def _plan_call(gone):
    t = gone.shape[0]
    tm = MOE_TM
    nt = t // tm
    r = lax.broadcasted_iota(jnp.int32, (tm, tm), 0)
    c = lax.broadcasted_iota(jnp.int32, (tm, tm), 1)
    ltri = (c < r).astype(BF16)
    utri = (lax.broadcasted_iota(jnp.int32, (LANES, LANES), 0)
            < lax.broadcasted_iota(jnp.int32, (LANES, LANES), 1)).astype(BF16)
    return pl.pallas_call(
        _plan_kernel,
        grid=(nt,),
        in_specs=[pl.BlockSpec((tm, LANES), lambda i: (i, 0)),
                  pl.BlockSpec((tm, tm), lambda i: (0, 0)),
                  pl.BlockSpec((LANES, LANES), lambda i: (0, 0))],
        out_specs=(pl.BlockSpec((tm, LANES), lambda i: (i, 0)),
                   pl.BlockSpec((1, SUBLANES, LANES), lambda i: (i, 0, 0))),
        out_shape=(jax.ShapeDtypeStruct((t, LANES), jnp.int32),
                   jax.ShapeDtypeStruct((nt, SUBLANES, LANES), jnp.int32)),
        compiler_params=_cparams("parallel"),
        name="moe_plan",
    )(gone, ltri, utri)


def _moe_kernel(cg_ref, nc_ref, ht_ref, slot_ref, wgu_ref, wd_ref, o_ref,
                pt_ref, acc_ref, xg_ref, cw_ref, yt_ref):
    i = pl.program_id(0)
    j = pl.program_id(1)
    nc = nc_ref[i]
    tm = slot_ref.shape[0]
    d = acc_ref.shape[1]

    def gather(jn, slot_idx):
        xa = _dot(ht_ref[...], pt_ref[jn])
        xg_ref[slot_idx] = xa[0:d, :].astype(BF16)
        cw_ref[slot_idx] = xa[d:d + 16, :] + xa[d + 16:d + 32, :] + xa[d + 32:d + 48, :]

    def scatter(jp, slot_idx):
        acc_ref[...] += _dot_nt(pt_ref[jp], yt_ref[slot_idx])

    @pl.when(j == 0)
    def _():
        acc_ref[...] = jnp.zeros(acc_ref.shape, F32)
        yt_ref[...] = jnp.zeros(yt_ref.shape, BF16)
        slot = slot_ref[...]
        lane = lax.broadcasted_iota(jnp.int32, (tm, LANES), 1)
        for jj in range(MOE_MAX_CHUNKS):
            for half in range(SLOT_CHUNK // LANES):
                off = jj * SLOT_CHUNK + half * LANES
                pt_ref[jj, :, half * LANES:(half + 1) * LANES] = jnp.where(
                    slot == lane + off, 1.0, 0.0).astype(BF16)
        gather(0, 0)

    @pl.when(j < nc)
    def _():
        cur = j % 2
        scatter(jnp.maximum(j - 1, 0), 1 - cur)
        xg = xg_ref[cur]
        cw = cw_ref[cur]
        es = EXPERTS_PER_GROUP // FFN_SPLIT
        sf = es * EXPERT_FF
        yt = None
        for s in range(FFN_SPLIT):
            ab = _dot(wgu_ref[0, 2 * sf * s:2 * sf * (s + 1), :], xg)
            mids = []
            for e in range(es):
                a = ab[e * EXPERT_FF:(e + 1) * EXPERT_FF, :]
                b = ab[sf + e * EXPERT_FF:sf + (e + 1) * EXPERT_FF, :]
                c = cw[s * es + e:s * es + e + 1, :]
                mids.append((a * _sigmoid(a) * b * c).astype(BF16))
            part = _dot(wd_ref[0, :, sf * s:sf * (s + 1)], jnp.concatenate(mids, axis=0))
            yt = part if yt is None else yt + part
        yt_ref[cur] = yt.astype(BF16)
        gather(jnp.minimum(j + 1, MOE_MAX_CHUNKS - 1), 1 - cur)

    @pl.when(j == nc)
    def _():
        scatter(nc - 1, (nc - 1) % 2)

    @pl.when(j == pl.num_programs(1) - 1)
    def _():
        o_ref[...] = acc_ref[...].astype(BF16)


def _moe_call(ht_aug, slot, chunk_group, n_chunks, wgu_t, wd_t):
    da, t = ht_aug.shape
    d = da - AUG_ROWS
    tm = MOE_TM
    nt = t // tm
    ff = EXPERTS_PER_GROUP * EXPERT_FF
    wmap = lambda i, j, cg, nc: (cg[i * MOE_STEPS + j], 0, 0)
    grid_spec = pltpu.PrefetchScalarGridSpec(
        num_scalar_prefetch=2,
        grid=(nt, MOE_STEPS),
        in_specs=[
            pl.BlockSpec((da, tm), lambda i, j, cg, nc: (0, i)),
            pl.BlockSpec((tm, LANES), lambda i, j, cg, nc: (i, 0)),
            pl.BlockSpec((1, 2 * ff, d), wmap),
            pl.BlockSpec((1, d, ff), wmap),
        ],
        out_specs=pl.BlockSpec((tm, d), lambda i, j, cg, nc: (i, 0)),
        scratch_shapes=[pltpu.VMEM((MOE_MAX_CHUNKS, tm, SLOT_CHUNK), BF16),
                        pltpu.VMEM((tm, d), F32),
                        pltpu.VMEM((2, d, SLOT_CHUNK), BF16),
                        pltpu.VMEM((2, 16, SLOT_CHUNK), F32),
                        pltpu.VMEM((2, d, SLOT_CHUNK), BF16)],
    )
    return pl.pallas_call(
        _moe_kernel,
        grid_spec=grid_spec,
        out_shape=jax.ShapeDtypeStruct((t, d), BF16),
        compiler_params=_cparams("parallel", "arbitrary"),
        name="moe",
    )(chunk_group, n_chunks, ht_aug, slot, wgu_t, wd_t)


def _chunk_tables(cnt):
    per_group = (cnt[:, 0, :N_GROUPS] + SLOT_CHUNK - 1) // SLOT_CHUNK
    ends = jnp.cumsum(per_group, axis=1)
    n_chunks = ends[:, -1]
    j = jnp.arange(MOE_STEPS, dtype=jnp.int32)[None, :]
    jj = jnp.minimum(j, n_chunks[:, None] - 1)
    group = jnp.sum((jj[:, :, None] >= ends[:, None, :]).astype(jnp.int32), axis=-1)
    return group.reshape(-1).astype(jnp.int32), n_chunks.astype(jnp.int32)


def _pack_expert_weights(w_gate, w_up, w_down):
    ne, d, f = w_gate.shape
    tr = lambda w: jnp.swapaxes(w, 1, 2).reshape(N_GROUPS, FFN_SPLIT, -1, d)
    wgu = jnp.concatenate([tr(w_gate), tr(w_up)], axis=2).reshape(
        N_GROUPS, 2 * EXPERTS_PER_GROUP * f, d).astype(BF16)
    wd = jnp.swapaxes(w_down.reshape(N_GROUPS, EXPERTS_PER_GROUP * f, d), 1, 2).astype(BF16)
    return wgu, wd


def _norm_kernel(x_ref, y_ref, g_ref, o_ref):
    x = x_ref[...] + y_ref[...].astype(F32)
    o_ref[...] = x * lax.rsqrt(jnp.mean(x * x, axis=-1, keepdims=True) + EPS) * g_ref[...]


def _norm_call(x2, y2, g):
    t, d = x2.shape
    tm = NORM_TM
    row = lambda i: (i, 0)
    return pl.pallas_call(
        _norm_kernel,
        grid=(t // tm,),
        in_specs=[pl.BlockSpec((tm, d), row), pl.BlockSpec((tm, d), row),
                  pl.BlockSpec((1, d), lambda i: (0, 0))],
        out_specs=pl.BlockSpec((tm, d), row),
        out_shape=jax.ShapeDtypeStruct((t, d), F32),
        compiler_params=_cparams("parallel"),
        name="final_norm",
    )(x2, y2, g)


def kernel(x, g_mix, w_in, conv_w, conv_b, conv_ln_g, conv_ln_b, w_conv_out, w_attn_out, w_out, rel_bias, g_ffn, w_group_router, b_group_router, w_expert_router, b_expert_router, w_e_gate, w_e_up, w_e_down, g_final):
    batch, seq, d = x.shape
    depth = g_mix.shape[0]
    t = batch * seq
    assert seq % CONV_TL == 0 and seq % KEY_CHUNK == 0 and t % MOE_TM == 0
    x2 = x.reshape(t, d)
    y2 = None
    bias_tiles = _bias_tiles(rel_bias)
    for l in range(depth):
        w_pack = _pack_proj_weights(w_in[l])
        outs = _proj_call(x2, y2, g_mix[l][None, :], w_pack)
        if y2 is not None:
            x2, outs = outs[0], outs[1:]
        u, q_hm, k2, vt, qi_hm, ki2, wit, gc, ga = outs
        vc = _conv_call(u.reshape(batch, seq, 2 * CONV_CH), conv_w[l], conv_b[l][None, :],
                        conv_ln_g[l][None, :], conv_ln_b[l][None, :]).reshape(t, CONV_CH)
        at = _attn_call(q_hm, qi_hm, wit, k2, vt, ki2, bias_tiles, batch, seq)
        wr, br = _pack_router(w_group_router[l], b_group_router[l], w_expert_router[l],
                              b_expert_router[l])
        x2, ht_aug, gone = _merge_call(x2, vc, at, gc, ga, w_conv_out[l].astype(BF16),
                                       w_attn_out[l].astype(BF16), w_out[l].astype(BF16),
                                       g_ffn[l][None, :], wr, br)
        slot, cnt = _plan_call(gone)
        chunk_group, n_chunks = _chunk_tables(cnt)
        wgu_t, wd_t = _pack_expert_weights(w_e_gate[l], w_e_up[l], w_e_down[l])
        y2 = _moe_call(ht_aug, slot, chunk_group, n_chunks, wgu_t, wd_t)
    return _norm_call(x2, y2, g_final[None, :]).reshape(batch, seq, d)
```

```python
import functools
import math

import jax
import jax.numpy as jnp
import numpy as np
from jax import lax
from jax.experimental import pallas as pl
from jax.experimental.pallas import tpu as pltpu

CHUNK = 64
CONV_CH = 512
CONV_WIDTH = 31
N_HEADS = 8
HEAD_DIM = 64
N_KV = 2
KV_REP = N_HEADS // N_KV
ATTN_W = N_HEADS * HEAD_DIM
IDX_HEADS = 8
IDX_DIM = 32
TOPK_MAX = 256
Q_BLOCK = 128
REL_BUCKETS = 32
REL_MAX_DIST = 128
N_GROUPS = 4
EXPERTS_PER_GROUP = 8
N_EXPERTS = N_GROUPS * EXPERTS_PER_GROUP
EXPERT_FF = 256
EPS = 1e-6

LANES = 128
SUBLANES = 8
VMEM_LIMIT_BYTES = 56 * 1024 * 1024

LOG2E = math.log2(math.e)
NEG_BIG = -1e30
INT_MIN = -(2 ** 31)
KEY_MIN_FINITE = -0x7F800000
KEY_MAX_FINITE = 0x7F7FFFFF

BF16 = jnp.bfloat16
F32 = jnp.float32

PROJ_TM = 512
CONV_TL = 256
CONV_HALO = 32
MERGE_TM = 512
MOE_TM = 1024
MOE_EB = 4
NORM_TM = 1024

ROUTER_E_OFF = 32


def _cparams(*sem):
    return pltpu.CompilerParams(dimension_semantics=sem, vmem_limit_bytes=VMEM_LIMIT_BYTES)


def _sigmoid(x):
    return 1.0 / (1.0 + jnp.exp(-x))


def _dot(a, b):
    return jnp.dot(a, b, preferred_element_type=F32)


def _dot_nt(a, b):
    return lax.dot_general(a, b, (((1,), (1,)), ((), ())), preferred_element_type=F32)


def _dot_tn(a, b):
    return lax.dot_general(a, b, (((0,), (0,)), ((), ())), preferred_element_type=F32)


SEG_U = (0, 2 * CONV_CH)
SEG_Q = (SEG_U[1], SEG_U[1] + N_HEADS * LANES)
SEG_K = (SEG_Q[1], SEG_Q[1] + LANES)
SEG_V = (SEG_K[1], SEG_K[1] + LANES)
SEG_QI = (SEG_V[1], SEG_V[1] + IDX_HEADS * IDX_DIM)
SEG_KI = (SEG_QI[1], SEG_QI[1] + LANES)
SEG_WI = (SEG_KI[1], SEG_KI[1] + LANES)
SEG_GC = (SEG_WI[1], SEG_WI[1] + 1024)
SEG_GA = (SEG_GC[1], SEG_GC[1] + 1024)
PROJ_COLS = SEG_GA[1]


def _proj_kernel(*refs, has_y):
    if has_y:
        x_ref, y_ref, g_ref, w_ref, xo_ref = refs[:5]
        x = x_ref[...] + y_ref[...].astype(F32)
        xo_ref[...] = x
    else:
        x_ref, g_ref, w_ref = refs[:3]
        x = x_ref[...]
    u_ref, q_ref, k_ref, vt_ref, qi_ref, ki_ref, wit_ref, gc_ref, ga_ref = refs[-9:]
    h = x * lax.rsqrt(jnp.mean(x * x, axis=-1, keepdims=True) + EPS) * g_ref[...]
    hb = h.astype(BF16)

    def seg(s):
        return _dot(hb, w_ref[:, s[0]:s[1]])

    u_ref[...] = seg(SEG_U).astype(BF16)
    pq = seg(SEG_Q)
    for hh in range(N_HEADS):
        q_ref[hh] = pq[:, hh * LANES:(hh + 1) * LANES].astype(BF16)
    k_ref[...] = seg(SEG_K).astype(BF16)
    pv = seg(SEG_V)
    tm = pv.shape[0]
    for c in range(tm // LANES):
        vt_ref[c] = pv[c * LANES:(c + 1) * LANES, :].T.astype(BF16)
    pqi = seg(SEG_QI)
    for hh in range(IDX_HEADS):
        qi_ref[hh] = pqi[:, hh * IDX_DIM:(hh + 1) * IDX_DIM].astype(BF16)
    ki_ref[...] = seg(SEG_KI)[:, :IDX_DIM].astype(BF16)
    pwi = seg(SEG_WI)
    for c in range(tm // LANES):
        wit_ref[:, c * LANES:(c + 1) * LANES] = pwi[c * LANES:(c + 1) * LANES, :].T[:IDX_HEADS, :]
    gc_ref[...] = _sigmoid(seg(SEG_GC)).astype(BF16)
    ga_ref[...] = _sigmoid(seg(SEG_GA)).astype(BF16)


def _proj_call(x2, y2, g, w_pack):
    t, d = x2.shape
    tm = PROJ_TM
    nq = t // Q_BLOCK
    row = lambda i: (i, 0)
    has_y = y2 is not None
    out_shape = (jax.ShapeDtypeStruct((t, d), F32),) if has_y else ()
    out_specs = (pl.BlockSpec((tm, d), row),) if has_y else ()
    out_shape += (
        jax.ShapeDtypeStruct((t, 2 * CONV_CH), BF16),
        jax.ShapeDtypeStruct((N_HEADS, t, LANES), BF16),
        jax.ShapeDtypeStruct((t, LANES), BF16),
        jax.ShapeDtypeStruct((nq, LANES, Q_BLOCK), BF16),
        jax.ShapeDtypeStruct((IDX_HEADS, t, IDX_DIM), BF16),
        jax.ShapeDtypeStruct((t, IDX_DIM), BF16),
        jax.ShapeDtypeStruct((IDX_HEADS, t), F32),
        jax.ShapeDtypeStruct((t, d), BF16),
        jax.ShapeDtypeStruct((t, d), BF16),
    )
    out_specs += (
        pl.BlockSpec((tm, 2 * CONV_CH), row),
        pl.BlockSpec((N_HEADS, tm, LANES), lambda i: (0, i, 0)),
        pl.BlockSpec((tm, LANES), row),
        pl.BlockSpec((tm // Q_BLOCK, LANES, Q_BLOCK), lambda i: (i, 0, 0)),
        pl.BlockSpec((IDX_HEADS, tm, IDX_DIM), lambda i: (0, i, 0)),
        pl.BlockSpec((tm, IDX_DIM), row),
        pl.BlockSpec((IDX_HEADS, tm), lambda i: (0, i)),
        pl.BlockSpec((tm, d), row),
        pl.BlockSpec((tm, d), row),
    )
    streams = (x2, y2) if has_y else (x2,)
    return pl.pallas_call(
        functools.partial(_proj_kernel, has_y=has_y),
        grid=(t // tm,),
        in_specs=[pl.BlockSpec((tm, d), row) for _ in streams] + [
            pl.BlockSpec((1, d), lambda i: (0, 0)),
            pl.BlockSpec((d, PROJ_COLS), lambda i: (0, 0)),
        ],
        out_specs=out_specs,
        out_shape=out_shape,
        compiler_params=_cparams("parallel"),
        name="proj",
    )(*streams, g, w_pack)


def _pack_proj_weights(w_in):
    d = w_in.shape[0]
    splits = (2 * CONV_CH, ATTN_W, N_KV * HEAD_DIM, N_KV * HEAD_DIM, IDX_HEADS * IDX_DIM,
              IDX_DIM, IDX_HEADS, d, d)
    offs = np.concatenate([[0], np.cumsum(splits)])
    w_u, w_q, w_k, w_v, w_qi, w_ki, w_wi, w_gc, w_ga = [
        w_in[:, int(offs[i]):int(offs[i + 1])] for i in range(len(splits))]
    w_q = (w_q * (HEAD_DIM ** -0.5 * LOG2E)).reshape(d, N_KV, KV_REP, HEAD_DIM)
    q_pad = jnp.zeros((d, N_KV, KV_REP, N_KV, HEAD_DIM), w_in.dtype)
    for gidx in range(N_KV):
        q_pad = q_pad.at[:, gidx, :, gidx, :].set(w_q[:, gidx])
    q_pad = q_pad.reshape(d, N_HEADS * LANES)
    pad = lambda w: jnp.pad(w, ((0, 0), (0, LANES - w.shape[1])))
    packed = jnp.concatenate([w_u, q_pad, w_k, w_v, w_qi, pad(w_ki), pad(w_wi), w_gc, w_ga], axis=1)
    assert packed.shape[1] == PROJ_COLS
    return packed.astype(BF16)


def _conv_kernel(u_ref, halo_ref, cw_ref, cb_ref, lg_ref, lb_ref, o_ref, ext_ref, y_ref):
    i = pl.program_id(1)
    tl = u_ref.shape[1]

    def glu(u):
        u = u.astype(F32)
        return u[:, :CONV_CH] * _sigmoid(u[:, CONV_CH:])

    halo = glu(halo_ref[0])
    ext_ref[0:CONV_HALO, :] = jnp.where(i > 0, halo, 0.0)
    ext_ref[CONV_HALO:, :] = glu(u_ref[0])
    first = CONV_HALO - (CONV_WIDTH - 1)
    for c in range(CONV_CH // LANES):
        cs = slice(c * LANES, (c + 1) * LANES)
        acc = jnp.zeros((tl, LANES), F32)
        for j in range(CONV_WIDTH):
            acc = acc + cw_ref[j:j + 1, cs] * ext_ref[first + j:first + j + tl, cs]
        y_ref[:, cs] = acc + cb_ref[:, cs]
    y = y_ref[...]
    mu = jnp.mean(y, axis=-1, keepdims=True)
    yc = y - mu
    var = jnp.mean(yc * yc, axis=-1, keepdims=True)
    yn = yc * lax.rsqrt(var + EPS) * lg_ref[...] + lb_ref[...]
    o_ref[0] = (yn * _sigmoid(yn)).astype(BF16)


def _conv_call(u3, conv_w, conv_b, ln_g, ln_b):
    b, l, _ = u3.shape
    tl = CONV_TL
    halo_blocks = tl // CONV_HALO
    const = lambda bi, i: (0, 0)
    return pl.pallas_call(
        _conv_kernel,
        grid=(b, l // tl),
        in_specs=[
            pl.BlockSpec((1, tl, 2 * CONV_CH), lambda bi, i: (bi, i, 0)),
            pl.BlockSpec((1, CONV_HALO, 2 * CONV_CH),
                         lambda bi, i: (bi, jnp.maximum(i * halo_blocks - 1, 0), 0)),
            pl.BlockSpec((CONV_WIDTH, CONV_CH), const),
            pl.BlockSpec((1, CONV_CH), const),
            pl.BlockSpec((1, CONV_CH), const),
            pl.BlockSpec((1, CONV_CH), const),
        ],
        out_specs=pl.BlockSpec((1, tl, CONV_CH), lambda bi, i: (bi, i, 0)),
        out_shape=jax.ShapeDtypeStruct((b, l, CONV_CH), BF16),
        scratch_shapes=[pltpu.VMEM((CONV_HALO + tl, CONV_CH), F32), pltpu.VMEM((tl, CONV_CH), F32)],
        compiler_params=_cparams("parallel", "parallel"),
        name="conv",
    )(u3, u3, conv_w, conv_b, ln_g, ln_b)


def _sortable_key(score):
    b = lax.bitcast_convert_type(score, jnp.int32)
    return b ^ ((b >> 31) & 0x7FFFFFFF)


KEY_CHUNK = 2 * Q_BLOCK
HALF16 = 1 << 15
ONES_ROWS = 16


def _tree_reduce(x, rows, op):
    parts = [x[j * rows:(j + 1) * rows] for j in range(x.shape[0] // rows)]
    while len(parts) > 1:
        parts = [op(a, b) for a, b in zip(parts[0::2], parts[1::2])]
    return parts[0]


def _attn_kernel(*refs, topk, max_chunks):
    nch = (pl.program_id(1) + 2) // 2
    lax.switch(nch - 1, [functools.partial(_attn_body, *refs, topk=topk, nch=n + 1)
                         for n in range(max_chunks)])


def _static_loop(lo, hi, body, carry):
    for c in range(lo, hi):
        carry = body(c, carry)
    return carry


def _attn_body(q_ref, qi_ref, wit_ref, k_ref, vt_ref, ki_ref, bias_ref, o_ref,
               key_ref, khi_ref, klo_ref, madd_ref, thr_ref, x_ref, p_ref, acc_ref, *, topk, nch):
    i = pl.program_id(1)
    qb = Q_BLOCK
    ch = KEY_CHUNK
    nkb = i + 1
    q_all = q_ref[...].reshape(N_HEADS * qb, LANES)
    qi_st = qi_ref[...].reshape(IDX_HEADS * qb, IDX_DIM)
    wit = wit_ref[...]
    t_loc = lax.broadcasted_iota(jnp.int32, (1, qb), 1)
    limit = ((i * qb + t_loc) // CHUNK + 1) * CHUNK
    idx_scale = (IDX_DIM ** -0.5) * (IDX_HEADS ** -0.5)

    def chunk_rows(c):
        return slice(c * ch, (c + 1) * ch)

    def score_body(c, carry):
        rows = chunk_rows(c)
        dots = _dot_nt(ki_ref[rows, :], qi_st)
        sc = jnp.zeros((ch, qb), F32)
        for hh in range(IDX_HEADS):
            sc = sc + jnp.maximum(dots[:, hh * qb:(hh + 1) * qb], 0.0) * wit[hh:hh + 1, :]
        key = _sortable_key(sc * idx_scale)
        s_glob = c * ch + lax.broadcasted_iota(jnp.int32, (ch, qb), 0)
        key = jnp.where(s_glob < limit, key, INT_MIN)
        key_ref[rows, :] = key
        khi_ref[rows, :] = (key >> 16).astype(jnp.int16)
        klo_ref[rows, :] = ((key & 0xFFFF) - HALF16).astype(jnp.int16)
        x_ref[rows, :] = _dot_nt(k_ref[rows, :], q_all)
        return carry

    _static_loop(0, nch, score_body, 0)

    def count32(pred):
        def body(c, acc):
            ind = jnp.where(pred(key_ref[chunk_rows(c), :]), 1, 0)
            return acc + _tree_reduce(ind, acc.shape[0], jnp.add)
        acc = _static_loop(0, nch, body, jnp.zeros((4 * SUBLANES, qb), jnp.int32))
        return jnp.sum(acc, axis=0, keepdims=True)

    def search(n):
        def count16(ref, pred):
            parts = []
            for c in range(n):
                ind = jnp.where(pred(ref[c * ch:(c + 1) * ch, :]), jnp.int16(1), jnp.int16(0))
                parts.append(_tree_reduce(ind, 4 * SUBLANES, jnp.add))
            while len(parts) > 1:
                pairs = [a + b for a, b in zip(parts[0::2], parts[1::2])]
                parts = pairs + parts[len(pairs) * 2:]
            return jnp.sum(parts[0].astype(jnp.int32), axis=0, keepdims=True)

        def kth_largest16(ref, want):
            def bit_body(it, lo):
                cand = lo + jnp.left_shift(jnp.int32(1), 15 - it)
                c16 = cand.astype(jnp.int16)
                return jnp.where(count16(ref, lambda v: v >= c16) >= want, cand, lo)
            return lax.fori_loop(0, 16, bit_body, jnp.full((1, qb), -HALF16, jnp.int32))

        t_hi = kth_largest16(khi_ref, topk)
        t_hi16 = t_hi.astype(jnp.int16)
        n_above = count16(khi_ref, lambda v: v > t_hi16)
        for c in range(n):
            rows = slice(c * ch, (c + 1) * ch)
            klo_ref[rows, :] = jnp.where(khi_ref[rows, :] == t_hi16, klo_ref[rows, :],
                                         jnp.int16(-HALF16))
        t_lo = kth_largest16(klo_ref, topk - n_above)
        thr_ref[0:1, :] = jnp.maximum(jnp.left_shift(t_hi, 16) | (t_lo + HALF16), KEY_MIN_FINITE)

    thr_ref[...] = jnp.full(thr_ref.shape, KEY_MIN_FINITE, jnp.int32)

    @pl.when(nkb * qb > topk)
    def _():
        search(nch)

    thr = thr_ref[0:1, :]
    n_ge = count32(lambda key: key >= thr)
    has_ties = jnp.max(n_ge) > topk

    @pl.when(jnp.logical_not(has_ties))
    def _():
        def body(c, carry):
            rows = chunk_rows(c)
            key = key_ref[rows, :]
            sel = jnp.logical_and(key >= thr, key <= KEY_MAX_FINITE)
            madd_ref[rows, :] = jnp.where(sel, 0.0, NEG_BIG)
            return carry
        _static_loop(0, nch, body, 0)

    @pl.when(has_ties)
    def _():
        n_gt = count32(lambda key: key > thr)
        need = (topk - n_gt).astype(F32)
        lower = (lax.broadcasted_iota(jnp.int32, (ch, ch), 1)
                 < lax.broadcasted_iota(jnp.int32, (ch, ch), 0)).astype(BF16)

        def body(c, seen):
            rows = chunk_rows(c)
            key = key_ref[rows, :]
            eq = key == thr
            eqf = jnp.where(eq, 1.0, 0.0)
            before = _dot(lower, eqf.astype(BF16)) + seen
            sel = jnp.logical_or(key > thr, jnp.logical_and(eq, before < need))
            sel = jnp.logical_and(sel, key <= KEY_MAX_FINITE)
            madd_ref[rows, :] = jnp.where(sel, 0.0, NEG_BIG)
            return seen + jnp.sum(eqf, axis=0, keepdims=True)
        _static_loop(0, nch, body, jnp.zeros((1, qb), F32))

    def mask_rows(rows, tile, mparts):
        madd = madd_ref[rows, :]
        out = []
        for hh in range(N_HEADS):
            cols = slice(hh * qb, (hh + 1) * qb)
            x = x_ref[rows, cols] + madd
            if tile is not None:
                x = x + bias_ref[tile, hh]
            x_ref[rows, cols] = x
            out.append(jnp.maximum(mparts[hh], _tree_reduce(x, SUBLANES, jnp.maximum)))
        return tuple(out)

    def far_body(c, mparts):
        return mask_rows(chunk_rows(c), None, mparts)

    def near_body(c, mparts):
        for half in range(ch // qb):
            jb = c * (ch // qb) + half
            d = i - jb
            tile = jnp.where(d == 0, 0, jnp.where(d == 1, 1, 2))
            mparts = mask_rows(slice(jb * qb, (jb + 1) * qb), tile, mparts)
        return mparts

    n_far = max(nch - 2, 0)
    mparts = tuple(jnp.full((SUBLANES, qb), NEG_BIG, F32) for _ in range(N_HEADS))
    mparts = _static_loop(0, n_far, far_body, mparts)
    mparts = _static_loop(n_far, nch, near_body, mparts)
    m_rows = [jnp.max(mp, axis=0, keepdims=True) for mp in mparts]

    acc_ref[...] = jnp.zeros(acc_ref.shape, F32)
    ones = jnp.ones((ONES_ROWS, ch), BF16)

    def prob_chunk(c):
        rows = chunk_rows(c)
        for hh in range(N_HEADS):
            cols = slice(hh * qb, (hh + 1) * qb)
            p_ref[rows, cols] = jnp.exp2((x_ref[rows, cols] - m_rows[hh]).astype(BF16))

    def pv_chunk(c):
        rows = chunk_rows(c)
        vt2 = jnp.concatenate([vt_ref[2 * c], vt_ref[2 * c + 1]], axis=1)
        for g in range(N_KV):
            lhs = jnp.concatenate([vt2[g * HEAD_DIM:(g + 1) * HEAD_DIM, :], ones], axis=0)
            acc_ref[g] += _dot(lhs, p_ref[rows, g * KV_REP * qb:(g + 1) * KV_REP * qb])

    prob_chunk(0)

    def pv_body(c, carry):
        pv_chunk(c - 1)
        prob_chunk(c)
        return carry

    _static_loop(1, nch, pv_body, 0)
    pv_chunk(nch - 1)

    for hp in range(N_HEADS // 2):
        parts = []
        for hh in (2 * hp, 2 * hp + 1):
            g, r = hh // KV_REP, hh % KV_REP
            cols = slice(r * qb, (r + 1) * qb)
            inv_l = 1.0 / acc_ref[g, HEAD_DIM:HEAD_DIM + 1, cols]
            parts.append(acc_ref[g, 0:HEAD_DIM, cols] * inv_l)
        st = jnp.concatenate(parts, axis=0)
        o_ref[:, hp * LANES:(hp + 1) * LANES] = st.T.astype(BF16)


def _attn_call(q_hm, qi_hm, wit, k2, vt, ki2, bias_tiles, batch, seq):
    t = k2.shape[0]
    nblk = seq // Q_BLOCK
    topk = min(TOPK_MAX, seq // 4)
    qrow = lambda b, i: (0, b * nblk + i, 0)
    return pl.pallas_call(
        functools.partial(_attn_kernel, topk=topk, max_chunks=seq // KEY_CHUNK),
        grid=(batch, nblk),
        in_specs=[
            pl.BlockSpec((N_HEADS, Q_BLOCK, LANES), qrow),
            pl.BlockSpec((IDX_HEADS, Q_BLOCK, IDX_DIM), qrow),
            pl.BlockSpec((IDX_HEADS, Q_BLOCK), lambda b, i: (0, b * nblk + i)),
            pl.BlockSpec((seq, LANES), lambda b, i: (b, 0)),
            pl.BlockSpec((nblk, LANES, Q_BLOCK), lambda b, i: (b, 0, 0)),
            pl.BlockSpec((seq, IDX_DIM), lambda b, i: (b, 0)),
            pl.BlockSpec((3, N_HEADS, Q_BLOCK, Q_BLOCK), lambda b, i: (0, 0, 0, 0)),
        ],
        out_specs=pl.BlockSpec((Q_BLOCK, ATTN_W), lambda b, i: (b * nblk + i, 0)),
        out_shape=jax.ShapeDtypeStruct((t, ATTN_W), BF16),
        scratch_shapes=[
            pltpu.VMEM((seq, Q_BLOCK), jnp.int32),
            pltpu.VMEM((seq, Q_BLOCK), jnp.int16),
            pltpu.VMEM((seq, Q_BLOCK), jnp.int16),
            pltpu.VMEM((seq, Q_BLOCK), F32),
            pltpu.VMEM((SUBLANES, Q_BLOCK), jnp.int32),
            pltpu.VMEM((seq, N_HEADS * Q_BLOCK), F32),
            pltpu.VMEM((seq, N_HEADS * Q_BLOCK), BF16),
            pltpu.VMEM((N_KV, HEAD_DIM + ONES_ROWS, KV_REP * Q_BLOCK), F32),
        ],
        compiler_params=_cparams("parallel", "arbitrary"),
        name="attn",
    )(q_hm, qi_hm, wit, k2, vt, ki2, bias_tiles)


def _rel_bucket(rel):
    nb = REL_BUCKETS // 2
    max_exact = nb // 2
    ret = jnp.where(rel < 0, nb, 0)
    n = jnp.abs(rel)
    nf = jnp.maximum(n, 1).astype(jnp.float32)
    large = max_exact + (jnp.log(nf / max_exact) / math.log(REL_MAX_DIST / max_exact)
                         * (nb - max_exact)).astype(jnp.int32)
    large = jnp.minimum(large, nb - 1)
    return ret + jnp.where(n < max_exact, n, large)


def _bias_tiles(rel_bias):
    s = jnp.arange(Q_BLOCK, dtype=jnp.int32)[:, None]
    tq = jnp.arange(Q_BLOCK, dtype=jnp.int32)[None, :]
    d = jnp.arange(3, dtype=jnp.int32)[:, None, None]
    bucket = _rel_bucket(d * Q_BLOCK + tq - s)
    onehot = bucket[:, None, :, :, None] == jnp.arange(REL_BUCKETS, dtype=jnp.int32)
    tiles = jnp.sum(jnp.where(onehot, rel_bias.T[None, :, None, None, :], 0.0), axis=-1)
    return ((tiles - tiles[2:3, :, :1, :1]) * LOG2E).astype(F32)


AUG_ROWS = 48


def _merge_kernel(x_ref, vc_ref, at_ref, gc_ref, ga_ref, wco_ref, wao_ref, wo_ref, g_ref, wr_ref,
                  br_ref, xo_ref, ht_ref, gone_ref):
    y_conv = _dot(vc_ref[...], wco_ref[...])
    y_attn = _dot(at_ref[...], wao_ref[...])
    merged = gc_ref[...].astype(F32) * y_conv + ga_ref[...].astype(F32) * y_attn
    x = x_ref[...] + _dot(merged.astype(BF16), wo_ref[...])
    xo_ref[...] = x
    h = x * lax.rsqrt(jnp.mean(x * x, axis=-1, keepdims=True) + EPS) * g_ref[...]
    hb = h.astype(BF16)
    d = h.shape[1]
    ht_ref[0:d, :] = hb.astype(F32).T.astype(BF16)

    logits = _dot(hb, wr_ref[...]) + br_ref[...]
    lane = lax.broadcasted_iota(jnp.int32, logits.shape, 1)
    neg_inf = -jnp.inf
    gl = jnp.where(lane < N_GROUPS, logits, neg_inf)
    gmax = jnp.max(gl, axis=-1, keepdims=True)
    grp = jnp.min(jnp.where(gl == gmax, lane, LANES), axis=-1, keepdims=True)
    p_grp = 1.0 / jnp.sum(jnp.exp(gl - gmax), axis=-1, keepdims=True)
    e_lo = ROUTER_E_OFF + grp * EXPERTS_PER_GROUP
    in_grp = jnp.logical_and(lane >= e_lo, lane < e_lo + EXPERTS_PER_GROUP)
    el = jnp.where(in_grp, logits, neg_inf)
    v1 = jnp.max(el, axis=-1, keepdims=True)
    i1 = jnp.min(jnp.where(jnp.logical_and(in_grp, el == v1), lane, LANES), axis=-1, keepdims=True)
    rest = jnp.logical_and(in_grp, lane != i1)
    el2 = jnp.where(rest, logits, neg_inf)
    v2 = jnp.max(el2, axis=-1, keepdims=True)
    i2 = jnp.min(jnp.where(jnp.logical_and(rest, el2 == v2), lane, LANES), axis=-1, keepdims=True)
    e2 = jnp.exp(v2 - v1)
    p1 = p_grp / (1.0 + e2)
    p2 = p_grp * e2 / (1.0 + e2)
    gone_ref[...] = jnp.where(lane == grp, 1.0, 0.0).astype(BF16)
    w8 = jnp.where(lane == i1 - e_lo, p1, jnp.where(lane == i2 - e_lo, p2, 0.0))
    w8t = w8.T[0:16, :]
    hi = w8t.astype(BF16)
    r1 = w8t - hi.astype(F32)
    mid = r1.astype(BF16)
    lo = (r1 - mid.astype(F32)).astype(BF16)
    ht_ref[d:d + 16, :] = hi
    ht_ref[d + 16:d + 32, :] = mid
    ht_ref[d + 32:d + 48, :] = lo


def _merge_call(x2, vc, at, gc, ga, wco, wao, wo, g, wr, br):
    t, d = x2.shape
    tm = MERGE_TM
    row = lambda i: (i, 0)
    const = lambda i: (0, 0)
    return pl.pallas_call(
        _merge_kernel,
        grid=(t // tm,),
        in_specs=[
            pl.BlockSpec((tm, d), row),
            pl.BlockSpec((tm, CONV_CH), row),
            pl.BlockSpec((tm, ATTN_W), row),
            pl.BlockSpec((tm, d), row),
            pl.BlockSpec((tm, d), row),
            pl.BlockSpec((CONV_CH, d), const),
            pl.BlockSpec((ATTN_W, d), const),
            pl.BlockSpec((d, d), const),
            pl.BlockSpec((1, d), const),
            pl.BlockSpec((d, LANES), const),
            pl.BlockSpec((1, LANES), const),
        ],
        out_specs=(pl.BlockSpec((tm, d), row), pl.BlockSpec((d + AUG_ROWS, tm), lambda i: (0, i)),
                   pl.BlockSpec((tm, LANES), row)),
        out_shape=(jax.ShapeDtypeStruct((t, d), F32),
                   jax.ShapeDtypeStruct((d + AUG_ROWS, t), BF16),
                   jax.ShapeDtypeStruct((t, LANES), BF16)),
        compiler_params=_cparams("parallel"),
        name="merge",
    )(x2, vc, at, gc, ga, wco, wao, wo, g, wr, br)


def _pack_router(w_gr, b_gr, w_er, b_er):
    d = w_gr.shape[0]
    w = jnp.zeros((d, LANES), F32)
    w = w.at[:, :N_GROUPS].set(w_gr)
    w = w.at[:, ROUTER_E_OFF:ROUTER_E_OFF + N_EXPERTS].set(
        jnp.moveaxis(w_er, 0, 1).reshape(d, N_EXPERTS))
    b = jnp.zeros((1, LANES), F32)
    b = b.at[0, :N_GROUPS].set(b_gr)
    b = b.at[0, ROUTER_E_OFF:ROUTER_E_OFF + N_EXPERTS].set(b_er.reshape(N_EXPERTS))
    return w.astype(BF16), b


SLOT_CHUNK = 256
FFN_SPLIT = 8
MOE_MAX_CHUNKS = MOE_TM // SLOT_CHUNK + N_GROUPS - 1
MOE_STEPS = MOE_MAX_CHUNKS + 1


def _plan_kernel(gone_ref, ltri_ref, utri_ref, slot_ref, cnt_ref):
    gone = gone_ref[...]
    before = _dot(ltri_ref[...], gone)
    last = gone.shape[0] - 1
    cnt = before[last:last + 1, :] + gone[last:last + 1, :].astype(F32)
    nchunks = jnp.floor((cnt + (SLOT_CHUNK - 1)) * (1.0 / SLOT_CHUNK))
    start = _dot(jnp.broadcast_to(nchunks, (SUBLANES, LANES)).astype(BF16), utri_ref[...])[0:1, :]
    slot = jnp.sum(gone.astype(F32) * (before + start * SLOT_CHUNK), axis=-1, keepdims=True)
    slot_ref[...] = jnp.broadcast_to(slot.astype(jnp.int32), slot_ref.shape)
    cnt_ref[0] = jnp.broadcast_to(cnt, (SUBLANES, LANES)).astype(jnp.int32)


def _plan_call(gone):
    t = gone.shape[0]
    tm = MOE_TM
    nt = t // tm
    r = lax.broadcasted_iota(jnp.int32, (tm, tm), 0)
    c = lax.broadcasted_iota(jnp.int32, (tm, tm), 1)
    ltri = (c < r).astype(BF16)
    utri = (lax.broadcasted_iota(jnp.int32, (LANES, LANES), 0)
            < lax.broadcasted_iota(jnp.int32, (LANES, LANES), 1)).astype(BF16)
    return pl.pallas_call(
        _plan_kernel,
        grid=(nt,),
        in_specs=[pl.BlockSpec((tm, LANES), lambda i: (i, 0)),
                  pl.BlockSpec((tm, tm), lambda i: (0, 0)),
                  pl.BlockSpec((LANES, LANES), lambda i: (0, 0))],
        out_specs=(pl.BlockSpec((tm, LANES), lambda i: (i, 0)),
                   pl.BlockSpec((1, SUBLANES, LANES), lambda i: (i, 0, 0))),
        out_shape=(jax.ShapeDtypeStruct((t, LANES), jnp.int32),
                   jax.ShapeDtypeStruct((nt, SUBLANES, LANES), jnp.int32)),
        compiler_params=_cparams("parallel"),
        name="moe_plan",
    )(gone, ltri, utri)


def _moe_kernel(cg_ref, nc_ref, ht_ref, slot_ref, wgu_ref, wd_ref, o_ref,
                pt_ref, acc_ref, xg_ref, cw_ref, yt_ref):
    i = pl.program_id(0)
    j = pl.program_id(1)
    nc = nc_ref[i]
    tm = slot_ref.shape[0]
    d = acc_ref.shape[1]

    def gather(jn, slot_idx):
        xa = _dot(ht_ref[...], pt_ref[jn])
        xg_ref[slot_idx] = xa[0:d, :].astype(BF16)
        cw_ref[slot_idx] = xa[d:d + 16, :] + xa[d + 16:d + 32, :] + xa[d + 32:d + 48, :]

    def scatter(jp, slot_idx):
        acc_ref[...] += _dot_nt(pt_ref[jp], yt_ref[slot_idx])

    @pl.when(j == 0)
    def _():
        acc_ref[...] = jnp.zeros(acc_ref.shape, F32)
        yt_ref[...] = jnp.zeros(yt_ref.shape, BF16)
        slot = slot_ref[...]
        lane = lax.broadcasted_iota(jnp.int32, (tm, LANES), 1)
        for jj in range(MOE_MAX_CHUNKS):
            for half in range(SLOT_CHUNK // LANES):
                off = jj * SLOT_CHUNK + half * LANES
                pt_ref[jj, :, half * LANES:(half + 1) * LANES] = jnp.where(
                    slot == lane + off, 1.0, 0.0).astype(BF16)
        gather(0, 0)

    @pl.when(j < nc)
    def _():
        cur = j % 2
        scatter(jnp.maximum(j - 1, 0), 1 - cur)
        xg = xg_ref[cur]
        cw = cw_ref[cur]
        es = EXPERTS_PER_GROUP // FFN_SPLIT
        sf = es * EXPERT_FF
        yt = None
        for s in range(FFN_SPLIT):
            ab = _dot(wgu_ref[0, 2 * sf * s:2 * sf * (s + 1), :], xg)
            mids = []
            for e in range(es):
                a = ab[e * EXPERT_FF:(e + 1) * EXPERT_FF, :]
                b = ab[sf + e * EXPERT_FF:sf + (e + 1) * EXPERT_FF, :]
                c = cw[s * es + e:s * es + e + 1, :]
                mids.append((a * _sigmoid(a) * b * c).astype(BF16))
            part = _dot(wd_ref[0, :, sf * s:sf * (s + 1)], jnp.concatenate(mids, axis=0))
            yt = part if yt is None else yt + part
        yt_ref[cur] = yt.astype(BF16)
        gather(jnp.minimum(j + 1, MOE_MAX_CHUNKS - 1), 1 - cur)

    @pl.when(j == nc)
    def _():
        scatter(nc - 1, (nc - 1) % 2)

    @pl.when(j == pl.num_programs(1) - 1)
    def _():
        o_ref[...] = acc_ref[...].astype(BF16)


def _moe_call(ht_aug, slot, chunk_group, n_chunks, wgu_t, wd_t):
    da, t = ht_aug.shape
    d = da - AUG_ROWS
    tm = MOE_TM
    nt = t // tm
    ff = EXPERTS_PER_GROUP * EXPERT_FF
    wmap = lambda i, j, cg, nc: (cg[i * MOE_STEPS + j], 0, 0)
    grid_spec = pltpu.PrefetchScalarGridSpec(
        num_scalar_prefetch=2,
        grid=(nt, MOE_STEPS),
        in_specs=[
            pl.BlockSpec((da, tm), lambda i, j, cg, nc: (0, i)),
            pl.BlockSpec((tm, LANES), lambda i, j, cg, nc: (i, 0)),
            pl.BlockSpec((1, 2 * ff, d), wmap),
            pl.BlockSpec((1, d, ff), wmap),
        ],
        out_specs=pl.BlockSpec((tm, d), lambda i, j, cg, nc: (i, 0)),
        scratch_shapes=[pltpu.VMEM((MOE_MAX_CHUNKS, tm, SLOT_CHUNK), BF16),
                        pltpu.VMEM((tm, d), F32),
                        pltpu.VMEM((2, d, SLOT_CHUNK), BF16),
                        pltpu.VMEM((2, 16, SLOT_CHUNK), F32),
                        pltpu.VMEM((2, d, SLOT_CHUNK), BF16)],
    )
    return pl.pallas_call(
        _moe_kernel,
        grid_spec=grid_spec,
        out_shape=jax.ShapeDtypeStruct((t, d), BF16),
        compiler_params=_cparams("parallel", "arbitrary"),
        name="moe",
    )(chunk_group, n_chunks, ht_aug, slot, wgu_t, wd_t)


def _chunk_tables(cnt):
    per_group = (cnt[:, 0, :N_GROUPS] + SLOT_CHUNK - 1) // SLOT_CHUNK
    ends = jnp.cumsum(per_group, axis=1)
    n_chunks = ends[:, -1]
    j = jnp.arange(MOE_STEPS, dtype=jnp.int32)[None, :]
    jj = jnp.minimum(j, n_chunks[:, None] - 1)
    group = jnp.sum((jj[:, :, None] >= ends[:, None, :]).astype(jnp.int32), axis=-1)
    return group.reshape(-1).astype(jnp.int32), n_chunks.astype(jnp.int32)


def _pack_expert_weights(w_gate, w_up, w_down):
    ne, d, f = w_gate.shape
    tr = lambda w: jnp.swapaxes(w, 1, 2).reshape(N_GROUPS, FFN_SPLIT, -1, d)
    wgu = jnp.concatenate([tr(w_gate), tr(w_up)], axis=2).reshape(
        N_GROUPS, 2 * EXPERTS_PER_GROUP * f, d).astype(BF16)
    wd = jnp.swapaxes(w_down.reshape(N_GROUPS, EXPERTS_PER_GROUP * f, d), 1, 2).astype(BF16)
    return wgu, wd


def _norm_kernel(x_ref, y_ref, g_ref, o_ref):
    x = x_ref[...] + y_ref[...].astype(F32)
    o_ref[...] = x * lax.rsqrt(jnp.mean(x * x, axis=-1, keepdims=True) + EPS) * g_ref[...]


def _norm_call(x2, y2, g):
    t, d = x2.shape
    tm = NORM_TM
    row = lambda i: (i, 0)
    return pl.pallas_call(
        _norm_kernel,
        grid=(t // tm,),
        in_specs=[pl.BlockSpec((tm, d), row), pl.BlockSpec((tm, d), row),
                  pl.BlockSpec((1, d), lambda i: (0, 0))],
        out_specs=pl.BlockSpec((tm, d), row),
        out_shape=jax.ShapeDtypeStruct((t, d), F32),
        compiler_params=_cparams("parallel"),
        name="final_norm",
    )(x2, y2, g)


def kernel(x, g_mix, w_in, conv_w, conv_b, conv_ln_g, conv_ln_b, w_conv_out, w_attn_out, w_out, rel_bias, g_ffn, w_group_router, b_group_router, w_expert_router, b_expert_router, w_e_gate, w_e_up, w_e_down, g_final):
    batch, seq, d = x.shape
    depth = g_mix.shape[0]
    t = batch * seq
    assert seq % CONV_TL == 0 and seq % KEY_CHUNK == 0 and t % MOE_TM == 0
    x2 = x.reshape(t, d)
    y2 = None
    bias_tiles = _bias_tiles(rel_bias)
    for l in range(depth):
        w_pack = _pack_proj_weights(w_in[l])
        outs = _proj_call(x2, y2, g_mix[l][None, :], w_pack)
        if y2 is not None:
            x2, outs = outs[0], outs[1:]
        u, q_hm, k2, vt, qi_hm, ki2, wit, gc, ga = outs
        vc = _conv_call(u.reshape(batch, seq, 2 * CONV_CH), conv_w[l], conv_b[l][None, :],
                        conv_ln_g[l][None, :], conv_ln_b[l][None, :]).reshape(t, CONV_CH)
        at = _attn_call(q_hm, qi_hm, wit, k2, vt, ki2, bias_tiles, batch, seq)
        wr, br = _pack_router(w_group_router[l], b_group_router[l], w_expert_router[l],
                              b_expert_router[l])
        x2, ht_aug, gone = _merge_call(x2, vc, at, gc, ga, w_conv_out[l].astype(BF16),
                                       w_attn_out[l].astype(BF16), w_out[l].astype(BF16),
                                       g_ffn[l][None, :], wr, br)
        slot, cnt = _plan_call(gone)
        chunk_group, n_chunks = _chunk_tables(cnt)
        wgu_t, wd_t = _pack_expert_weights(w_e_gate[l], w_e_up[l], w_e_down[l])
        y2 = _moe_call(ht_aug, slot, chunk_group, n_chunks, wgu_t, wd_t)
    return _norm_call(x2, y2, g_final[None, :]).reshape(batch, seq, d)
```

```python
import functools
import math

import jax
import jax.numpy as jnp
import numpy as np
from jax import lax
from jax.experimental import pallas as pl
from jax.experimental.pallas import tpu as pltpu

CHUNK = 64
CONV_CH = 512
CONV_WIDTH = 31
N_HEADS = 8
HEAD_DIM = 64
N_KV = 2
KV_REP = N_HEADS // N_KV
ATTN_W = N_HEADS * HEAD_DIM
IDX_HEADS = 8
IDX_DIM = 32
TOPK_MAX = 256
Q_BLOCK = 128
REL_BUCKETS = 32
REL_MAX_DIST = 128
N_GROUPS = 4
EXPERTS_PER_GROUP = 8
N_EXPERTS = N_GROUPS * EXPERTS_PER_GROUP
EXPERT_FF = 256
EPS = 1e-6

LANES = 128
SUBLANES = 8
VMEM_LIMIT_BYTES = 56 * 1024 * 1024

LOG2E = math.log2(math.e)
NEG_BIG = -1e30
INT_MIN = -(2 ** 31)
KEY_MIN_FINITE = -0x7F800000
KEY_MAX_FINITE = 0x7F7FFFFF

BF16 = jnp.bfloat16
F32 = jnp.float32

PROJ_TM = 512
CONV_TL = 256
CONV_HALO = 32
CONV_ROWS = 128
MERGE_TM = 512
MOE_TM = 1024
MOE_EB = 4
NORM_TM = 1024

ROUTER_E_OFF = 32


def _cparams(*sem):
    return pltpu.CompilerParams(dimension_semantics=sem, vmem_limit_bytes=VMEM_LIMIT_BYTES)


def _sigmoid(x):
    return 1.0 / (1.0 + jnp.exp(-x))


def _dot(a, b):
    return jnp.dot(a, b, preferred_element_type=F32)


def _dot_nt(a, b):
    return lax.dot_general(a, b, (((1,), (1,)), ((), ())), preferred_element_type=F32)


def _dot_tn(a, b):
    return lax.dot_general(a, b, (((0,), (0,)), ((), ())), preferred_element_type=F32)


SEG_U = (0, 2 * CONV_CH)
SEG_Q = (SEG_U[1], SEG_U[1] + N_HEADS * LANES)
SEG_K = (SEG_Q[1], SEG_Q[1] + LANES)
SEG_V = (SEG_K[1], SEG_K[1] + LANES)
SEG_QI = (SEG_V[1], SEG_V[1] + IDX_HEADS * IDX_DIM)
SEG_KI = (SEG_QI[1], SEG_QI[1] + LANES)
SEG_WI = (SEG_KI[1], SEG_KI[1] + LANES)
SEG_GC = (SEG_WI[1], SEG_WI[1] + 1024)
SEG_GA = (SEG_GC[1], SEG_GC[1] + 1024)
PROJ_COLS = SEG_GA[1]


def _proj_kernel(*refs, has_y):
    if has_y:
        x_ref, y_ref, g_ref, w_ref, xo_ref = refs[:5]
        x = x_ref[...] + y_ref[...].astype(F32)
        xo_ref[...] = x
    else:
        x_ref, g_ref, w_ref = refs[:3]
        x = x_ref[...]
    u_ref, q_ref, k_ref, vt_ref, qi_ref, ki_ref, wit_ref, gc_ref, ga_ref = refs[-9:]
    h = x * lax.rsqrt(jnp.mean(x * x, axis=-1, keepdims=True) + EPS) * g_ref[...]
    hb = h.astype(BF16)

    def seg(s):
        return _dot(hb, w_ref[:, s[0]:s[1]])

    u_ref[...] = seg(SEG_U).astype(BF16)
    pq = seg(SEG_Q)
    for hh in range(N_HEADS):
        q_ref[hh] = pq[:, hh * LANES:(hh + 1) * LANES].astype(BF16)
    k_ref[...] = seg(SEG_K).astype(BF16)
    pv = seg(SEG_V)
    tm = pv.shape[0]
    for c in range(tm // LANES):
        vt_ref[c] = pv[c * LANES:(c + 1) * LANES, :].T.astype(BF16)
    pqi = seg(SEG_QI)
    for hh in range(IDX_HEADS):
        qi_ref[hh] = pqi[:, hh * IDX_DIM:(hh + 1) * IDX_DIM].astype(BF16)
    ki_ref[...] = seg(SEG_KI)[:, :IDX_DIM].astype(BF16)
    pwi = seg(SEG_WI)
    for c in range(tm // LANES):
        wit_ref[:, c * LANES:(c + 1) * LANES] = pwi[c * LANES:(c + 1) * LANES, :].T[:IDX_HEADS, :]
    gc_ref[...] = _sigmoid(seg(SEG_GC)).astype(BF16)
    ga_ref[...] = _sigmoid(seg(SEG_GA)).astype(BF16)


def _proj_call(x2, y2, g, w_pack):
    t, d = x2.shape
    tm = PROJ_TM
    nq = t // Q_BLOCK
    row = lambda i: (i, 0)
    has_y = y2 is not None
    out_shape = (jax.ShapeDtypeStruct((t, d), F32),) if has_y else ()
    out_specs = (pl.BlockSpec((tm, d), row),) if has_y else ()
    out_shape += (
        jax.ShapeDtypeStruct((t, 2 * CONV_CH), BF16),
        jax.ShapeDtypeStruct((N_HEADS, t, LANES), BF16),
        jax.ShapeDtypeStruct((t, LANES), BF16),
        jax.ShapeDtypeStruct((nq, LANES, Q_BLOCK), BF16),
        jax.ShapeDtypeStruct((IDX_HEADS, t, IDX_DIM), BF16),
        jax.ShapeDtypeStruct((t, IDX_DIM), BF16),
        jax.ShapeDtypeStruct((IDX_HEADS, t), F32),
        jax.ShapeDtypeStruct((t, d), BF16),
        jax.ShapeDtypeStruct((t, d), BF16),
    )
    out_specs += (
        pl.BlockSpec((tm, 2 * CONV_CH), row),
        pl.BlockSpec((N_HEADS, tm, LANES), lambda i: (0, i, 0)),
        pl.BlockSpec((tm, LANES), row),
        pl.BlockSpec((tm // Q_BLOCK, LANES, Q_BLOCK), lambda i: (i, 0, 0)),
        pl.BlockSpec((IDX_HEADS, tm, IDX_DIM), lambda i: (0, i, 0)),
        pl.BlockSpec((tm, IDX_DIM), row),
        pl.BlockSpec((IDX_HEADS, tm), lambda i: (0, i)),
        pl.BlockSpec((tm, d), row),
        pl.BlockSpec((tm, d), row),
    )
    streams = (x2, y2) if has_y else (x2,)
    return pl.pallas_call(
        functools.partial(_proj_kernel, has_y=has_y),
        grid=(t // tm,),
        in_specs=[pl.BlockSpec((tm, d), row) for _ in streams] + [
            pl.BlockSpec((1, d), lambda i: (0, 0)),
            pl.BlockSpec((d, PROJ_COLS), lambda i: (0, 0)),
        ],
        out_specs=out_specs,
        out_shape=out_shape,
        compiler_params=_cparams("parallel"),
        name="proj",
    )(*streams, g, w_pack)


def _pack_proj_weights(w_in):
    d = w_in.shape[0]
    splits = (2 * CONV_CH, ATTN_W, N_KV * HEAD_DIM, N_KV * HEAD_DIM, IDX_HEADS * IDX_DIM,
              IDX_DIM, IDX_HEADS, d, d)
    offs = np.concatenate([[0], np.cumsum(splits)])
    w_u, w_q, w_k, w_v, w_qi, w_ki, w_wi, w_gc, w_ga = [
        w_in[:, int(offs[i]):int(offs[i + 1])] for i in range(len(splits))]
    w_q = (w_q * (HEAD_DIM ** -0.5 * LOG2E)).reshape(d, N_KV, KV_REP, HEAD_DIM)
    q_pad = jnp.zeros((d, N_KV, KV_REP, N_KV, HEAD_DIM), w_in.dtype)
    for gidx in range(N_KV):
        q_pad = q_pad.at[:, gidx, :, gidx, :].set(w_q[:, gidx])
    q_pad = q_pad.reshape(d, N_HEADS * LANES)
    pad = lambda w: jnp.pad(w, ((0, 0), (0, LANES - w.shape[1])))
    packed = jnp.concatenate([w_u, q_pad, w_k, w_v, w_qi, pad(w_ki), pad(w_wi), w_gc, w_ga], axis=1)
    assert packed.shape[1] == PROJ_COLS
    return packed.astype(BF16)


def _conv_kernel(u_ref, halo_ref, cw_ref, cb_ref, lg_ref, lb_ref, o_ref, ext_ref, y_ref):
    i = pl.program_id(1)
    tl = u_ref.shape[1]

    def glu(u):
        u = u.astype(F32)
        return u[:, :CONV_CH] * _sigmoid(u[:, CONV_CH:])

    halo = glu(halo_ref[0])
    ext_ref[0:CONV_HALO, :] = jnp.where(i > 0, halo, 0.0)
    ext_ref[CONV_HALO:, :] = glu(u_ref[0])
    first = CONV_HALO - (CONV_WIDTH - 1)
    for c in range(CONV_CH // LANES):
        cs = slice(c * LANES, (c + 1) * LANES)
        for r0 in range(0, tl, CONV_ROWS):
            out = cb_ref[:, cs]
            for b in range(SUBLANES):
                win = CONV_ROWS + (SUBLANES if b else 0)
                part = jnp.zeros((win, LANES), F32)
                for j in range(CONV_WIDTH):
                    if (first + j) % SUBLANES == b:
                        lo = first + j - b + r0
                        part = part + cw_ref[j:j + 1, cs] * ext_ref[lo:lo + win, cs]
                out = out + part[b:b + CONV_ROWS]
            y_ref[r0:r0 + CONV_ROWS, cs] = out
    y = y_ref[...]
    mu = jnp.mean(y, axis=-1, keepdims=True)
    yc = y - mu
    var = jnp.mean(yc * yc, axis=-1, keepdims=True)
    yn = yc * lax.rsqrt(var + EPS) * lg_ref[...] + lb_ref[...]
    o_ref[0] = (yn * _sigmoid(yn)).astype(BF16)


def _conv_call(u3, conv_w, conv_b, ln_g, ln_b):
    b, l, _ = u3.shape
    tl = CONV_TL
    halo_blocks = tl // CONV_HALO
    const = lambda bi, i: (0, 0)
    return pl.pallas_call(
        _conv_kernel,
        grid=(b, l // tl),
        in_specs=[
            pl.BlockSpec((1, tl, 2 * CONV_CH), lambda bi, i: (bi, i, 0)),
            pl.BlockSpec((1, CONV_HALO, 2 * CONV_CH),
                         lambda bi, i: (bi, jnp.maximum(i * halo_blocks - 1, 0), 0)),
            pl.BlockSpec((CONV_WIDTH, CONV_CH), const),
            pl.BlockSpec((1, CONV_CH), const),
            pl.BlockSpec((1, CONV_CH), const),
            pl.BlockSpec((1, CONV_CH), const),
        ],
        out_specs=pl.BlockSpec((1, tl, CONV_CH), lambda bi, i: (bi, i, 0)),
        out_shape=jax.ShapeDtypeStruct((b, l, CONV_CH), BF16),
        scratch_shapes=[pltpu.VMEM((CONV_HALO + tl, CONV_CH), F32), pltpu.VMEM((tl, CONV_CH), F32)],
        compiler_params=_cparams("parallel", "parallel"),
        name="conv",
    )(u3, u3, conv_w, conv_b, ln_g, ln_b)


def _sortable_key(score):
    b = lax.bitcast_convert_type(score, jnp.int32)
    return b ^ ((b >> 31) & 0x7FFFFFFF)


KEY_CHUNK = 2 * Q_BLOCK
HALF16 = 1 << 15
ONES_ROWS = 16


def _tree_reduce(x, rows, op):
    parts = [x[j * rows:(j + 1) * rows] for j in range(x.shape[0] // rows)]
    while len(parts) > 1:
        parts = [op(a, b) for a, b in zip(parts[0::2], parts[1::2])]
    return parts[0]


def _attn_kernel(*refs, topk, max_chunks):
    nch = (pl.program_id(1) + 2) // 2
    lax.switch(nch - 1, [functools.partial(_attn_body, *refs, topk=topk, nch=n + 1)
                         for n in range(max_chunks)])


def _static_loop(lo, hi, body, carry):
    for c in range(lo, hi):
        carry = body(c, carry)
    return carry


def _attn_body(q_ref, qi_ref, wit_ref, k_ref, vt_ref, ki_ref, bias_ref, o_ref,
               key_ref, khi_ref, klo_ref, madd_ref, thr_ref, x_ref, p_ref, acc_ref, *, topk, nch):
    i = pl.program_id(1)
    qb = Q_BLOCK
    ch = KEY_CHUNK
    nkb = i + 1
    q_all = q_ref[...].reshape(N_HEADS * qb, LANES)
    qi_st = qi_ref[...].reshape(IDX_HEADS * qb, IDX_DIM)
    wit = wit_ref[...]
    t_loc = lax.broadcasted_iota(jnp.int32, (1, qb), 1)
    limit = ((i * qb + t_loc) // CHUNK + 1) * CHUNK
    idx_scale = (IDX_DIM ** -0.5) * (IDX_HEADS ** -0.5)

    def chunk_rows(c):
        return slice(c * ch, (c + 1) * ch)

    def score_body(c, carry):
        rows = chunk_rows(c)
        dots = _dot_nt(ki_ref[rows, :], qi_st)
        sc = jnp.zeros((ch, qb), F32)
        for hh in range(IDX_HEADS):
            sc = sc + jnp.maximum(dots[:, hh * qb:(hh + 1) * qb], 0.0) * wit[hh:hh + 1, :]
        key = _sortable_key(sc * idx_scale)
        s_glob = c * ch + lax.broadcasted_iota(jnp.int32, (ch, qb), 0)
        key = jnp.where(s_glob < limit, key, INT_MIN)
        key_ref[rows, :] = key
        khi_ref[rows, :] = (key >> 16).astype(jnp.int16)
        klo_ref[rows, :] = ((key & 0xFFFF) - HALF16).astype(jnp.int16)
        return carry

    _static_loop(0, nch, score_body, 0)

    def count32(pred):
        def body(c, acc):
            ind = jnp.where(pred(key_ref[chunk_rows(c), :]), 1, 0)
            return acc + _tree_reduce(ind, acc.shape[0], jnp.add)
        acc = _static_loop(0, nch, body, jnp.zeros((4 * SUBLANES, qb), jnp.int32))
        return jnp.sum(acc, axis=0, keepdims=True)

    def search(n):
        def count16(ref, pred):
            parts = []
            for c in range(n):
                ind = jnp.where(pred(ref[c * ch:(c + 1) * ch, :]), jnp.int16(1), jnp.int16(0))
                parts.append(_tree_reduce(ind, 4 * SUBLANES, jnp.add))
            while len(parts) > 1:
                pairs = [a + b for a, b in zip(parts[0::2], parts[1::2])]
                parts = pairs + parts[len(pairs) * 2:]
            return jnp.sum(parts[0].astype(jnp.int32), axis=0, keepdims=True)

        def kth_largest16(ref, want):
            def bit_body(it, lo):
                cand = lo + jnp.left_shift(jnp.int32(1), 15 - it)
                c16 = cand.astype(jnp.int16)
                return jnp.where(count16(ref, lambda v: v >= c16) >= want, cand, lo)
            return lax.fori_loop(0, 16, bit_body, jnp.full((1, qb), -HALF16, jnp.int32))

        t_hi = kth_largest16(khi_ref, topk)
        t_hi16 = t_hi.astype(jnp.int16)
        n_above = count16(khi_ref, lambda v: v > t_hi16)
        for c in range(n):
            rows = slice(c * ch, (c + 1) * ch)
            klo_ref[rows, :] = jnp.where(khi_ref[rows, :] == t_hi16, klo_ref[rows, :],
                                         jnp.int16(-HALF16))
        t_lo = kth_largest16(klo_ref, topk - n_above)
        thr_ref[0:1, :] = jnp.maximum(jnp.left_shift(t_hi, 16) | (t_lo + HALF16), KEY_MIN_FINITE)

    thr_ref[...] = jnp.full(thr_ref.shape, KEY_MIN_FINITE, jnp.int32)

    @pl.when(nkb * qb > topk)
    def _():
        search(nch)

    thr = thr_ref[0:1, :]
    n_ge = count32(lambda key: key >= thr)
    has_ties = jnp.max(n_ge) > topk

    @pl.when(jnp.logical_not(has_ties))
    def _():
        def body(c, carry):
            rows = chunk_rows(c)
            key = key_ref[rows, :]
            sel = jnp.logical_and(key >= thr, key <= KEY_MAX_FINITE)
            madd_ref[rows, :] = jnp.where(sel, 0.0, NEG_BIG)
            return carry
        _static_loop(0, nch, body, 0)

    @pl.when(has_ties)
    def _():
        n_gt = count32(lambda key: key > thr)
        need = (topk - n_gt).astype(F32)
        lower = (lax.broadcasted_iota(jnp.int32, (ch, ch), 1)
                 < lax.broadcasted_iota(jnp.int32, (ch, ch), 0)).astype(BF16)

        def body(c, seen):
            rows = chunk_rows(c)
            key = key_ref[rows, :]
            eq = key == thr
            eqf = jnp.where(eq, 1.0, 0.0)
            before = _dot(lower, eqf.astype(BF16)) + seen
            sel = jnp.logical_or(key > thr, jnp.logical_and(eq, before < need))
            sel = jnp.logical_and(sel, key <= KEY_MAX_FINITE)
            madd_ref[rows, :] = jnp.where(sel, 0.0, NEG_BIG)
            return seen + jnp.sum(eqf, axis=0, keepdims=True)
        _static_loop(0, nch, body, jnp.zeros((1, qb), F32))

    def mask_rows(rows, tile, mparts):
        madd = madd_ref[rows, :]
        lg = _dot_nt(k_ref[rows, :], q_all)
        out = []
        for hh in range(N_HEADS):
            cols = slice(hh * qb, (hh + 1) * qb)
            x = lg[:, cols] + madd
            if tile is not None:
                x = x + bias_ref[tile, hh]
            x_ref[rows, cols] = x
            out.append(jnp.maximum(mparts[hh], _tree_reduce(x, SUBLANES, jnp.maximum)))
        return tuple(out)

    def far_body(c, mparts):
        return mask_rows(chunk_rows(c), None, mparts)

    def near_body(c, mparts):
        for half in range(ch // qb):
            jb = c * (ch // qb) + half
            d = i - jb
            tile = jnp.where(d == 0, 0, jnp.where(d == 1, 1, 2))
            mparts = mask_rows(slice(jb * qb, (jb + 1) * qb), tile, mparts)
        return mparts

    n_far = max(nch - 2, 0)
    mparts = tuple(jnp.full((SUBLANES, qb), NEG_BIG, F32) for _ in range(N_HEADS))
    mparts = _static_loop(0, n_far, far_body, mparts)
    mparts = _static_loop(n_far, nch, near_body, mparts)
    m_rows = [jnp.max(mp, axis=0, keepdims=True) for mp in mparts]

    acc_ref[...] = jnp.zeros(acc_ref.shape, F32)
    ones = jnp.ones((ONES_ROWS, ch), BF16)

    def prob_chunk(c):
        rows = chunk_rows(c)
        for hh in range(N_HEADS):
            cols = slice(hh * qb, (hh + 1) * qb)
            p_ref[rows, cols] = jnp.exp2((x_ref[rows, cols] - m_rows[hh]).astype(BF16))

    def pv_chunk(c):
        rows = chunk_rows(c)
        vt2 = jnp.concatenate([vt_ref[2 * c], vt_ref[2 * c + 1]], axis=1)
        for g in range(N_KV):
            lhs = jnp.concatenate([vt2[g * HEAD_DIM:(g + 1) * HEAD_DIM, :], ones], axis=0)
            acc_ref[g] += _dot(lhs, p_ref[rows, g * KV_REP * qb:(g + 1) * KV_REP * qb])

    prob_chunk(0)

    def pv_body(c, carry):
        pv_chunk(c - 1)
        prob_chunk(c)
        return carry

    _static_loop(1, nch, pv_body, 0)
    pv_chunk(nch - 1)

    for hp in range(N_HEADS // 2):
        parts = []
        for hh in (2 * hp, 2 * hp + 1):
            g, r = hh // KV_REP, hh % KV_REP
            cols = slice(r * qb, (r + 1) * qb)
            inv_l = 1.0 / acc_ref[g, HEAD_DIM:HEAD_DIM + 1, cols]
            parts.append(acc_ref[g, 0:HEAD_DIM, cols] * inv_l)
        st = jnp.concatenate(parts, axis=0)
        o_ref[:, hp * LANES:(hp + 1) * LANES] = st.T.astype(BF16)


def _attn_call(q_hm, qi_hm, wit, k2, vt, ki2, bias_tiles, batch, seq):
    t = k2.shape[0]
    nblk = seq // Q_BLOCK
    topk = min(TOPK_MAX, seq // 4)
    qrow = lambda b, i: (0, b * nblk + i, 0)
    return pl.pallas_call(
        functools.partial(_attn_kernel, topk=topk, max_chunks=seq // KEY_CHUNK),
        grid=(batch, nblk),
        in_specs=[
            pl.BlockSpec((N_HEADS, Q_BLOCK, LANES), qrow),
            pl.BlockSpec((IDX_HEADS, Q_BLOCK, IDX_DIM), qrow),
            pl.BlockSpec((IDX_HEADS, Q_BLOCK), lambda b, i: (0, b * nblk + i)),
            pl.BlockSpec((seq, LANES), lambda b, i: (b, 0)),
            pl.BlockSpec((nblk, LANES, Q_BLOCK), lambda b, i: (b, 0, 0)),
            pl.BlockSpec((seq, IDX_DIM), lambda b, i: (b, 0)),
            pl.BlockSpec((3, N_HEADS, Q_BLOCK, Q_BLOCK), lambda b, i: (0, 0, 0, 0)),
        ],
        out_specs=pl.BlockSpec((Q_BLOCK, ATTN_W), lambda b, i: (b * nblk + i, 0)),
        out_shape=jax.ShapeDtypeStruct((t, ATTN_W), BF16),
        scratch_shapes=[
            pltpu.VMEM((seq, Q_BLOCK), jnp.int32),
            pltpu.VMEM((seq, Q_BLOCK), jnp.int16),
            pltpu.VMEM((seq, Q_BLOCK), jnp.int16),
            pltpu.VMEM((seq, Q_BLOCK), F32),
            pltpu.VMEM((SUBLANES, Q_BLOCK), jnp.int32),
            pltpu.VMEM((seq, N_HEADS * Q_BLOCK), F32),
            pltpu.VMEM((seq, N_HEADS * Q_BLOCK), BF16),
            pltpu.VMEM((N_KV, HEAD_DIM + ONES_ROWS, KV_REP * Q_BLOCK), F32),
        ],
        compiler_params=_cparams("parallel", "arbitrary"),
        name="attn",
    )(q_hm, qi_hm, wit, k2, vt, ki2, bias_tiles)


def _rel_bucket(rel):
    nb = REL_BUCKETS // 2
    max_exact = nb // 2
    ret = jnp.where(rel < 0, nb, 0)
    n = jnp.abs(rel)
    nf = jnp.maximum(n, 1).astype(jnp.float32)
    large = max_exact + (jnp.log(nf / max_exact) / math.log(REL_MAX_DIST / max_exact)
                         * (nb - max_exact)).astype(jnp.int32)
    large = jnp.minimum(large, nb - 1)
    return ret + jnp.where(n < max_exact, n, large)


def _bias_tiles(rel_bias):
    s = jnp.arange(Q_BLOCK, dtype=jnp.int32)[:, None]
    tq = jnp.arange(Q_BLOCK, dtype=jnp.int32)[None, :]
    d = jnp.arange(3, dtype=jnp.int32)[:, None, None]
    bucket = _rel_bucket(d * Q_BLOCK + tq - s)
    onehot = bucket[:, None, :, :, None] == jnp.arange(REL_BUCKETS, dtype=jnp.int32)
    tiles = jnp.sum(jnp.where(onehot, rel_bias.T[None, :, None, None, :], 0.0), axis=-1)
    return ((tiles - tiles[2:3, :, :1, :1]) * LOG2E).astype(F32)


AUG_ROWS = 48


def _merge_kernel(x_ref, vc_ref, at_ref, gc_ref, ga_ref, wco_ref, wao_ref, wo_ref, g_ref, wr_ref,
                  br_ref, xo_ref, ht_ref, gone_ref):
    y_conv = _dot(vc_ref[...], wco_ref[...])
    y_attn = _dot(at_ref[...], wao_ref[...])
    merged = gc_ref[...].astype(F32) * y_conv + ga_ref[...].astype(F32) * y_attn
    x = x_ref[...] + _dot(merged.astype(BF16), wo_ref[...])
    xo_ref[...] = x
    h = x * lax.rsqrt(jnp.mean(x * x, axis=-1, keepdims=True) + EPS) * g_ref[...]
    hb = h.astype(BF16)
    d = h.shape[1]
    ht_ref[0:d, :] = hb.astype(F32).T.astype(BF16)

    logits = _dot(hb, wr_ref[...]) + br_ref[...]
    lane = lax.broadcasted_iota(jnp.int32, logits.shape, 1)
    neg_inf = -jnp.inf
    gl = jnp.where(lane < N_GROUPS, logits, neg_inf)
    gmax = jnp.max(gl, axis=-1, keepdims=True)
    grp = jnp.min(jnp.where(gl == gmax, lane, LANES), axis=-1, keepdims=True)
    p_grp = 1.0 / jnp.sum(jnp.exp(gl - gmax), axis=-1, keepdims=True)
    e_lo = ROUTER_E_OFF + grp * EXPERTS_PER_GROUP
    in_grp = jnp.logical_and(lane >= e_lo, lane < e_lo + EXPERTS_PER_GROUP)
    el = jnp.where(in_grp, logits, neg_inf)
    v1 = jnp.max(el, axis=-1, keepdims=True)
    i1 = jnp.min(jnp.where(jnp.logical_and(in_grp, el == v1), lane, LANES), axis=-1, keepdims=True)
    rest = jnp.logical_and(in_grp, lane != i1)
    el2 = jnp.where(rest, logits, neg_inf)
    v2 = jnp.max(el2, axis=-1, keepdims=True)
    i2 = jnp.min(jnp.where(jnp.logical_and(rest, el2 == v2), lane, LANES), axis=-1, keepdims=True)
    e2 = jnp.exp(v2 - v1)
    p1 = p_grp / (1.0 + e2)
    p2 = p_grp * e2 / (1.0 + e2)
    gone_ref[...] = jnp.where(lane == grp, 1.0, 0.0).astype(BF16)
    w8 = jnp.where(lane == i1 - e_lo, p1, jnp.where(lane == i2 - e_lo, p2, 0.0))
    w8t = w8.T[0:16, :]
    hi = w8t.astype(BF16)
    r1 = w8t - hi.astype(F32)
    mid = r1.astype(BF16)
    lo = (r1 - mid.astype(F32)).astype(BF16)
    ht_ref[d:d + 16, :] = hi
    ht_ref[d + 16:d + 32, :] = mid
    ht_ref[d + 32:d + 48, :] = lo


def _merge_call(x2, vc, at, gc, ga, wco, wao, wo, g, wr, br):
    t, d = x2.shape
    tm = MERGE_TM
    row = lambda i: (i, 0)
    const = lambda i: (0, 0)
    return pl.pallas_call(
        _merge_kernel,
        grid=(t // tm,),
        in_specs=[
            pl.BlockSpec((tm, d), row),
            pl.BlockSpec((tm, CONV_CH), row),
            pl.BlockSpec((tm, ATTN_W), row),
            pl.BlockSpec((tm, d), row),
            pl.BlockSpec((tm, d), row),
            pl.BlockSpec((CONV_CH, d), const),
            pl.BlockSpec((ATTN_W, d), const),
            pl.BlockSpec((d, d), const),
            pl.BlockSpec((1, d), const),
            pl.BlockSpec((d, LANES), const),
            pl.BlockSpec((1, LANES), const),
        ],
        out_specs=(pl.BlockSpec((tm, d), row), pl.BlockSpec((d + AUG_ROWS, tm), lambda i: (0, i)),
                   pl.BlockSpec((tm, LANES), row)),
        out_shape=(jax.ShapeDtypeStruct((t, d), F32),
                   jax.ShapeDtypeStruct((d + AUG_ROWS, t), BF16),
                   jax.ShapeDtypeStruct((t, LANES), BF16)),
        compiler_params=_cparams("parallel"),
        name="merge",
    )(x2, vc, at, gc, ga, wco, wao, wo, g, wr, br)


def _pack_router(w_gr, b_gr, w_er, b_er):
    d = w_gr.shape[0]
    w = jnp.zeros((d, LANES), F32)
    w = w.at[:, :N_GROUPS].set(w_gr)
    w = w.at[:, ROUTER_E_OFF:ROUTER_E_OFF + N_EXPERTS].set(
        jnp.moveaxis(w_er, 0, 1).reshape(d, N_EXPERTS))
    b = jnp.zeros((1, LANES), F32)
    b = b.at[0, :N_GROUPS].set(b_gr)
    b = b.at[0, ROUTER_E_OFF:ROUTER_E_OFF + N_EXPERTS].set(b_er.reshape(N_EXPERTS))
    return w.astype(BF16), b


SLOT_CHUNK = 256
FFN_SPLIT = 8
MOE_MAX_CHUNKS = MOE_TM // SLOT_CHUNK + N_GROUPS - 1
MOE_STEPS = MOE_MAX_CHUNKS + 1


def _plan_kernel(gone_ref, ltri_ref, utri_ref, slot_ref, cnt_ref):
    gone = gone_ref[...]
    before = _dot(ltri_ref[...], gone)
    last = gone.shape[0] - 1
    cnt = before[last:last + 1, :] + gone[last:last + 1, :].astype(F32)
    nchunks = jnp.floor((cnt + (SLOT_CHUNK - 1)) * (1.0 / SLOT_CHUNK))
    start = _dot(jnp.broadcast_to(nchunks, (SUBLANES, LANES)).astype(BF16), utri_ref[...])[0:1, :]
    slot = jnp.sum(gone.astype(F32) * (before + start * SLOT_CHUNK), axis=-1, keepdims=True)
    slot_ref[...] = jnp.broadcast_to(slot.astype(jnp.int32), slot_ref.shape)
    cnt_ref[0] = jnp.broadcast_to(cnt, (SUBLANES, LANES)).astype(jnp.int32)


def _plan_call(gone):
    t = gone.shape[0]
    tm = MOE_TM
    nt = t // tm
    r = lax.broadcasted_iota(jnp.int32, (tm, tm), 0)
    c = lax.broadcasted_iota(jnp.int32, (tm, tm), 1)
    ltri = (c < r).astype(BF16)
    utri = (lax.broadcasted_iota(jnp.int32, (LANES, LANES), 0)
            < lax.broadcasted_iota(jnp.int32, (LANES, LANES), 1)).astype(BF16)
    return pl.pallas_call(
        _plan_kernel,
        grid=(nt,),
        in_specs=[pl.BlockSpec((tm, LANES), lambda i: (i, 0)),
                  pl.BlockSpec((tm, tm), lambda i: (0, 0)),
                  pl.BlockSpec((LANES, LANES), lambda i: (0, 0))],
        out_specs=(pl.BlockSpec((tm, LANES), lambda i: (i, 0)),
                   pl.BlockSpec((1, SUBLANES, LANES), lambda i: (i, 0, 0))),
        out_shape=(jax.ShapeDtypeStruct((t, LANES), jnp.int32),
                   jax.ShapeDtypeStruct((nt, SUBLANES, LANES), jnp.int32)),
        compiler_params=_cparams("parallel"),
        name="moe_plan",
    )(gone, ltri, utri)


def _moe_kernel(cg_ref, nc_ref, ht_ref, slot_ref, wgu_ref, wd_ref, o_ref,
                pt_ref, acc_ref, xg_ref, cw_ref, yt_ref):
    i = pl.program_id(0)
    j = pl.program_id(1)
    nc = nc_ref[i]
    tm = slot_ref.shape[0]
    d = acc_ref.shape[1]

    def gather(jn, slot_idx):
        xa = _dot(ht_ref[...], pt_ref[jn])
        xg_ref[slot_idx] = xa[0:d, :].astype(BF16)
        cw_ref[slot_idx] = xa[d:d + 16, :] + xa[d + 16:d + 32, :] + xa[d + 32:d + 48, :]

    def scatter(jp, slot_idx):
        acc_ref[...] += _dot_nt(pt_ref[jp], yt_ref[slot_idx])

    @pl.when(j == 0)
    def _():
        acc_ref[...] = jnp.zeros(acc_ref.shape, F32)
        yt_ref[...] = jnp.zeros(yt_ref.shape, BF16)
        slot = slot_ref[...]
        lane = lax.broadcasted_iota(jnp.int32, (tm, LANES), 1)
        for jj in range(MOE_MAX_CHUNKS):
            for half in range(SLOT_CHUNK // LANES):
                off = jj * SLOT_CHUNK + half * LANES
                pt_ref[jj, :, half * LANES:(half + 1) * LANES] = jnp.where(
                    slot == lane + off, 1.0, 0.0).astype(BF16)
        gather(0, 0)

    @pl.when(j < nc)
    def _():
        cur = j % 2
        scatter(jnp.maximum(j - 1, 0), 1 - cur)
        xg = xg_ref[cur]
        cw = cw_ref[cur]
        es = EXPERTS_PER_GROUP // FFN_SPLIT
        sf = es * EXPERT_FF
        yt = None
        for s in range(FFN_SPLIT):
            ab = _dot(wgu_ref[0, 2 * sf * s:2 * sf * (s + 1), :], xg)
            mids = []
            for e in range(es):
                a = ab[e * EXPERT_FF:(e + 1) * EXPERT_FF, :]
                b = ab[sf + e * EXPERT_FF:sf + (e + 1) * EXPERT_FF, :]
                c = cw[s * es + e:s * es + e + 1, :]
                mids.append((a * _sigmoid(a) * b * c).astype(BF16))
            part = _dot(wd_ref[0, :, sf * s:sf * (s + 1)], jnp.concatenate(mids, axis=0))
            yt = part if yt is None else yt + part
        yt_ref[cur] = yt.astype(BF16)
        gather(jnp.minimum(j + 1, MOE_MAX_CHUNKS - 1), 1 - cur)

    @pl.when(j == nc)
    def _():
        scatter(nc - 1, (nc - 1) % 2)

    @pl.when(j == pl.num_programs(1) - 1)
    def _():
        o_ref[...] = acc_ref[...].astype(BF16)


def _moe_call(ht_aug, slot, chunk_group, n_chunks, wgu_t, wd_t):
    da, t = ht_aug.shape
    d = da - AUG_ROWS
    tm = MOE_TM
    nt = t // tm
    ff = EXPERTS_PER_GROUP * EXPERT_FF
    wmap = lambda i, j, cg, nc: (cg[i * MOE_STEPS + j], 0, 0)
    grid_spec = pltpu.PrefetchScalarGridSpec(
        num_scalar_prefetch=2,
        grid=(nt, MOE_STEPS),
        in_specs=[
            pl.BlockSpec((da, tm), lambda i, j, cg, nc: (0, i)),
            pl.BlockSpec((tm, LANES), lambda i, j, cg, nc: (i, 0)),
            pl.BlockSpec((1, 2 * ff, d), wmap),
            pl.BlockSpec((1, d, ff), wmap),
        ],
        out_specs=pl.BlockSpec((tm, d), lambda i, j, cg, nc: (i, 0)),
        scratch_shapes=[pltpu.VMEM((MOE_MAX_CHUNKS, tm, SLOT_CHUNK), BF16),
                        pltpu.VMEM((tm, d), F32),
                        pltpu.VMEM((2, d, SLOT_CHUNK), BF16),
                        pltpu.VMEM((2, 16, SLOT_CHUNK), F32),
                        pltpu.VMEM((2, d, SLOT_CHUNK), BF16)],
    )
    return pl.pallas_call(
        _moe_kernel,
        grid_spec=grid_spec,
        out_shape=jax.ShapeDtypeStruct((t, d), BF16),
        compiler_params=_cparams("parallel", "arbitrary"),
        name="moe",
    )(chunk_group, n_chunks, ht_aug, slot, wgu_t, wd_t)


def _chunk_tables(cnt):
    per_group = (cnt[:, 0, :N_GROUPS] + SLOT_CHUNK - 1) // SLOT_CHUNK
    ends = jnp.cumsum(per_group, axis=1)
    n_chunks = ends[:, -1]
    j = jnp.arange(MOE_STEPS, dtype=jnp.int32)[None, :]
    jj = jnp.minimum(j, n_chunks[:, None] - 1)
    group = jnp.sum((jj[:, :, None] >= ends[:, None, :]).astype(jnp.int32), axis=-1)
    return group.reshape(-1).astype(jnp.int32), n_chunks.astype(jnp.int32)


def _pack_expert_weights(w_gate, w_up, w_down):
    ne, d, f = w_gate.shape
    tr = lambda w: jnp.swapaxes(w, 1, 2).reshape(N_GROUPS, FFN_SPLIT, -1, d)
    wgu = jnp.concatenate([tr(w_gate), tr(w_up)], axis=2).reshape(
        N_GROUPS, 2 * EXPERTS_PER_GROUP * f, d).astype(BF16)
    wd = jnp.swapaxes(w_down.reshape(N_GROUPS, EXPERTS_PER_GROUP * f, d), 1, 2).astype(BF16)
    return wgu, wd


def _norm_kernel(x_ref, y_ref, g_ref, o_ref):
    x = x_ref[...] + y_ref[...].astype(F32)
    o_ref[...] = x * lax.rsqrt(jnp.mean(x * x, axis=-1, keepdims=True) + EPS) * g_ref[...]


def _norm_call(x2, y2, g):
    t, d = x2.shape
    tm = NORM_TM
    row = lambda i: (i, 0)
    return pl.pallas_call(
        _norm_kernel,
        grid=(t // tm,),
        in_specs=[pl.BlockSpec((tm, d), row), pl.BlockSpec((tm, d), row),
                  pl.BlockSpec((1, d), lambda i: (0, 0))],
        out_specs=pl.BlockSpec((tm, d), row),
        out_shape=jax.ShapeDtypeStruct((t, d), F32),
        compiler_params=_cparams("parallel"),
        name="final_norm",
    )(x2, y2, g)


def kernel(x, g_mix, w_in, conv_w, conv_b, conv_ln_g, conv_ln_b, w_conv_out, w_attn_out, w_out, rel_bias, g_ffn, w_group_router, b_group_router, w_expert_router, b_expert_router, w_e_gate, w_e_up, w_e_down, g_final):
    batch, seq, d = x.shape
    depth = g_mix.shape[0]
    t = batch * seq
    assert seq % CONV_TL == 0 and seq % KEY_CHUNK == 0 and t % MOE_TM == 0
    x2 = x.reshape(t, d)
    y2 = None
    bias_tiles = _bias_tiles(rel_bias)
    for l in range(depth):
        w_pack = _pack_proj_weights(w_in[l])
        outs = _proj_call(x2, y2, g_mix[l][None, :], w_pack)
        if y2 is not None:
            x2, outs = outs[0], outs[1:]
        u, q_hm, k2, vt, qi_hm, ki2, wit, gc, ga = outs
        vc = _conv_call(u.reshape(batch, seq, 2 * CONV_CH), conv_w[l], conv_b[l][None, :],
                        conv_ln_g[l][None, :], conv_ln_b[l][None, :]).reshape(t, CONV_CH)
        at = _attn_call(q_hm, qi_hm, wit, k2, vt, ki2, bias_tiles, batch, seq)
        wr, br = _pack_router(w_group_router[l], b_group_router[l], w_expert_router[l],
                              b_expert_router[l])
        x2, ht_aug, gone = _merge_call(x2, vc, at, gc, ga, w_conv_out[l].astype(BF16),
                                       w_attn_out[l].astype(BF16), w_out[l].astype(BF16),
                                       g_ffn[l][None, :], wr, br)
        slot, cnt = _plan_call(gone)
        chunk_group, n_chunks = _chunk_tables(cnt)
        wgu_t, wd_t = _pack_expert_weights(w_e_gate[l], w_e_up[l], w_e_down[l])
        y2 = _moe_call(ht_aug, slot, chunk_group, n_chunks, wgu_t, wd_t)
    return _norm_call(x2, y2, g_final[None, :]).reshape(batch, seq, d)
```

```python
import functools
import math

import jax
import jax.numpy as jnp
import numpy as np
from jax import lax
from jax.experimental import pallas as pl
from jax.experimental.pallas import tpu as pltpu

CHUNK = 64
CONV_CH = 512
CONV_WIDTH = 31
N_HEADS = 8
HEAD_DIM = 64
N_KV = 2
KV_REP = N_HEADS // N_KV
ATTN_W = N_HEADS * HEAD_DIM
IDX_HEADS = 8
IDX_DIM = 32
TOPK_MAX = 256
Q_BLOCK = 128
REL_BUCKETS = 32
REL_MAX_DIST = 128
N_GROUPS = 4
EXPERTS_PER_GROUP = 8
N_EXPERTS = N_GROUPS * EXPERTS_PER_GROUP
EXPERT_FF = 256
EPS = 1e-6

LANES = 128
SUBLANES = 8
VMEM_LIMIT_BYTES = 56 * 1024 * 1024

LOG2E = math.log2(math.e)
NEG_BIG = -1e30
INT_MIN = -(2 ** 31)
KEY_MIN_FINITE = -0x7F800000
KEY_MAX_FINITE = 0x7F7FFFFF

BF16 = jnp.bfloat16
F32 = jnp.float32

PROJ_TM = 512
CONV_TL = 256
CONV_HALO = 32
CONV_ROWS = 128
MERGE_TM = 512
MOE_TM = 1024
MOE_EB = 4
NORM_TM = 1024

ROUTER_E_OFF = 32


def _cparams(*sem):
    return pltpu.CompilerParams(dimension_semantics=sem, vmem_limit_bytes=VMEM_LIMIT_BYTES)


def _sigmoid(x):
    return 1.0 / (1.0 + jnp.exp(-x))


def _dot(a, b):
    return jnp.dot(a, b, preferred_element_type=F32)


def _dot_nt(a, b):
    return lax.dot_general(a, b, (((1,), (1,)), ((), ())), preferred_element_type=F32)


def _dot_tn(a, b):
    return lax.dot_general(a, b, (((0,), (0,)), ((), ())), preferred_element_type=F32)


SEG_U = (0, 2 * CONV_CH)
SEG_Q = (SEG_U[1], SEG_U[1] + N_HEADS * LANES)
SEG_K = (SEG_Q[1], SEG_Q[1] + LANES)
SEG_V = (SEG_K[1], SEG_K[1] + LANES)
SEG_QI = (SEG_V[1], SEG_V[1] + IDX_HEADS * IDX_DIM)
SEG_KI = (SEG_QI[1], SEG_QI[1] + LANES)
SEG_WI = (SEG_KI[1], SEG_KI[1] + LANES)
SEG_GC = (SEG_WI[1], SEG_WI[1] + 1024)
SEG_GA = (SEG_GC[1], SEG_GC[1] + 1024)
PROJ_COLS = SEG_GA[1]


def _proj_kernel(*refs, has_y):
    if has_y:
        x_ref, y_ref, g_ref, w_ref, xo_ref = refs[:5]
        x = x_ref[...] + y_ref[...].astype(F32)
        xo_ref[...] = x
    else:
        x_ref, g_ref, w_ref = refs[:3]
        x = x_ref[...]
    u_ref, q_ref, k_ref, vt_ref, qi_ref, ki_ref, wit_ref, gc_ref, ga_ref = refs[-9:]
    h = x * lax.rsqrt(jnp.mean(x * x, axis=-1, keepdims=True) + EPS) * g_ref[...]
    hb = h.astype(BF16)

    def seg(s):
        return _dot(hb, w_ref[:, s[0]:s[1]])

    u_ref[...] = seg(SEG_U).astype(BF16)
    pq = seg(SEG_Q)
    for hh in range(N_HEADS):
        q_ref[hh] = pq[:, hh * LANES:(hh + 1) * LANES].astype(BF16)
    k_ref[...] = seg(SEG_K).astype(BF16)
    pv = seg(SEG_V)
    tm = pv.shape[0]
    for c in range(tm // LANES):
        vt_ref[c] = pv[c * LANES:(c + 1) * LANES, :].T.astype(BF16)
    pqi = seg(SEG_QI)
    for hh in range(IDX_HEADS):
        qi_ref[hh] = pqi[:, hh * IDX_DIM:(hh + 1) * IDX_DIM].astype(BF16)
    ki_ref[...] = seg(SEG_KI)[:, :IDX_DIM].astype(BF16)
    pwi = seg(SEG_WI)
    for c in range(tm // LANES):
        wit_ref[:, c * LANES:(c + 1) * LANES] = pwi[c * LANES:(c + 1) * LANES, :].T[:IDX_HEADS, :]
    gc_ref[...] = _sigmoid(seg(SEG_GC)).astype(BF16)
    ga_ref[...] = _sigmoid(seg(SEG_GA)).astype(BF16)


def _proj_call(x2, y2, g, w_pack):
    t, d = x2.shape
    tm = PROJ_TM
    nq = t // Q_BLOCK
    row = lambda i: (i, 0)
    has_y = y2 is not None
    out_shape = (jax.ShapeDtypeStruct((t, d), F32),) if has_y else ()
    out_specs = (pl.BlockSpec((tm, d), row),) if has_y else ()
    out_shape += (
        jax.ShapeDtypeStruct((t, 2 * CONV_CH), BF16),
        jax.ShapeDtypeStruct((N_HEADS, t, LANES), BF16),
        jax.ShapeDtypeStruct((t, LANES), BF16),
        jax.ShapeDtypeStruct((nq, LANES, Q_BLOCK), BF16),
        jax.ShapeDtypeStruct((IDX_HEADS, t, IDX_DIM), BF16),
        jax.ShapeDtypeStruct((t, IDX_DIM), BF16),
        jax.ShapeDtypeStruct((IDX_HEADS, t), F32),
        jax.ShapeDtypeStruct((t, d), BF16),
        jax.ShapeDtypeStruct((t, d), BF16),
    )
    out_specs += (
        pl.BlockSpec((tm, 2 * CONV_CH), row),
        pl.BlockSpec((N_HEADS, tm, LANES), lambda i: (0, i, 0)),
        pl.BlockSpec((tm, LANES), row),
        pl.BlockSpec((tm // Q_BLOCK, LANES, Q_BLOCK), lambda i: (i, 0, 0)),
        pl.BlockSpec((IDX_HEADS, tm, IDX_DIM), lambda i: (0, i, 0)),
        pl.BlockSpec((tm, IDX_DIM), row),
        pl.BlockSpec((IDX_HEADS, tm), lambda i: (0, i)),
        pl.BlockSpec((tm, d), row),
        pl.BlockSpec((tm, d), row),
    )
    streams = (x2, y2) if has_y else (x2,)
    return pl.pallas_call(
        functools.partial(_proj_kernel, has_y=has_y),
        grid=(t // tm,),
        in_specs=[pl.BlockSpec((tm, d), row) for _ in streams] + [
            pl.BlockSpec((1, d), lambda i: (0, 0)),
            pl.BlockSpec((d, PROJ_COLS), lambda i: (0, 0)),
        ],
        out_specs=out_specs,
        out_shape=out_shape,
        compiler_params=_cparams("parallel"),
        name="proj",
    )(*streams, g, w_pack)


def _pack_proj_weights(w_in):
    d = w_in.shape[0]
    splits = (2 * CONV_CH, ATTN_W, N_KV * HEAD_DIM, N_KV * HEAD_DIM, IDX_HEADS * IDX_DIM,
              IDX_DIM, IDX_HEADS, d, d)
    offs = np.concatenate([[0], np.cumsum(splits)])
    w_u, w_q, w_k, w_v, w_qi, w_ki, w_wi, w_gc, w_ga = [
        w_in[:, int(offs[i]):int(offs[i + 1])] for i in range(len(splits))]
    w_q = (w_q * (HEAD_DIM ** -0.5 * LOG2E)).reshape(d, N_KV, KV_REP, HEAD_DIM)
    q_pad = jnp.zeros((d, N_KV, KV_REP, N_KV, HEAD_DIM), w_in.dtype)
    for gidx in range(N_KV):
        q_pad = q_pad.at[:, gidx, :, gidx, :].set(w_q[:, gidx])
    q_pad = q_pad.reshape(d, N_HEADS * LANES)
    pad = lambda w: jnp.pad(w, ((0, 0), (0, LANES - w.shape[1])))
    packed = jnp.concatenate([w_u, q_pad, w_k, w_v, w_qi, pad(w_ki), pad(w_wi), w_gc, w_ga], axis=1)
    assert packed.shape[1] == PROJ_COLS
    return packed.astype(BF16)


def _conv_kernel(u_ref, halo_ref, cw_ref, cb_ref, lg_ref, lb_ref, o_ref, ext_ref, y_ref):
    i = pl.program_id(1)
    tl = u_ref.shape[1]

    def glu(u):
        u = u.astype(F32)
        return u[:, :CONV_CH] * _sigmoid(u[:, CONV_CH:])

    halo = glu(halo_ref[0])
    ext_ref[0:CONV_HALO, :] = jnp.where(i > 0, halo, 0.0)
    ext_ref[CONV_HALO:, :] = glu(u_ref[0])
    first = CONV_HALO - (CONV_WIDTH - 1)
    for c in range(CONV_CH // LANES):
        cs = slice(c * LANES, (c + 1) * LANES)
        for r0 in range(0, tl, CONV_ROWS):
            out = cb_ref[:, cs]
            for b in range(SUBLANES):
                win = CONV_ROWS + (SUBLANES if b else 0)
                part = jnp.zeros((win, LANES), F32)
                for j in range(CONV_WIDTH):
                    if (first + j) % SUBLANES == b:
                        lo = first + j - b + r0
                        part = part + cw_ref[j:j + 1, cs] * ext_ref[lo:lo + win, cs]
                out = out + part[b:b + CONV_ROWS]
            y_ref[r0:r0 + CONV_ROWS, cs] = out
    y = y_ref[...]
    mu = jnp.mean(y, axis=-1, keepdims=True)
    yc = y - mu
    var = jnp.mean(yc * yc, axis=-1, keepdims=True)
    yn = yc * lax.rsqrt(var + EPS) * lg_ref[...] + lb_ref[...]
    o_ref[0] = (yn * _sigmoid(yn)).astype(BF16)


def _conv_call(u3, conv_w, conv_b, ln_g, ln_b):
    b, l, _ = u3.shape
    tl = CONV_TL
    halo_blocks = tl // CONV_HALO
    const = lambda bi, i: (0, 0)
    return pl.pallas_call(
        _conv_kernel,
        grid=(b, l // tl),
        in_specs=[
            pl.BlockSpec((1, tl, 2 * CONV_CH), lambda bi, i: (bi, i, 0)),
            pl.BlockSpec((1, CONV_HALO, 2 * CONV_CH),
                         lambda bi, i: (bi, jnp.maximum(i * halo_blocks - 1, 0), 0)),
            pl.BlockSpec((CONV_WIDTH, CONV_CH), const),
            pl.BlockSpec((1, CONV_CH), const),
            pl.BlockSpec((1, CONV_CH), const),
            pl.BlockSpec((1, CONV_CH), const),
        ],
        out_specs=pl.BlockSpec((1, tl, CONV_CH), lambda bi, i: (bi, i, 0)),
        out_shape=jax.ShapeDtypeStruct((b, l, CONV_CH), BF16),
        scratch_shapes=[pltpu.VMEM((CONV_HALO + tl, CONV_CH), F32), pltpu.VMEM((tl, CONV_CH), F32)],
        compiler_params=_cparams("parallel", "parallel"),
        name="conv",
    )(u3, u3, conv_w, conv_b, ln_g, ln_b)


def _sortable_key(score):
    b = lax.bitcast_convert_type(score, jnp.int32)
    return b ^ ((b >> 31) & 0x7FFFFFFF)


KEY_CHUNK = 2 * Q_BLOCK
ATTN_QB = KEY_CHUNK
HALF16 = 1 << 15
ONES_ROWS = 16


def _tree_reduce(x, rows, op):
    parts = [x[j * rows:(j + 1) * rows] for j in range(x.shape[0] // rows)]
    while len(parts) > 1:
        parts = [op(a, b) for a, b in zip(parts[0::2], parts[1::2])]
    return parts[0]


def _attn_kernel(*refs, topk, max_chunks):
    lax.switch(pl.program_id(1), [functools.partial(_attn_body, *refs, topk=topk, nch=n + 1)
                                  for n in range(max_chunks)])


def _static_loop(lo, hi, body, carry):
    for c in range(lo, hi):
        carry = body(c, carry)
    return carry


def _attn_body(q_ref, qi_ref, wit_ref, k_ref, vt_ref, ki_ref, bias_ref, o_ref,
               key_ref, khi_ref, klo_ref, madd_ref, thr_ref, x_ref, p_ref, acc_ref, *, topk, nch):
    i = nch - 1
    qb = ATTN_QB
    ch = KEY_CHUNK
    halves = qb // Q_BLOCK
    nkb = halves * nch
    q_all = q_ref[...].reshape(N_HEADS * qb, LANES)
    qi_st = qi_ref[...].reshape(IDX_HEADS * qb, IDX_DIM)
    wit = wit_ref[...]
    t_loc = lax.broadcasted_iota(jnp.int32, (1, qb), 1)
    limit = ((i * qb + t_loc) // CHUNK + 1) * CHUNK
    idx_scale = (IDX_DIM ** -0.5) * (IDX_HEADS ** -0.5)

    def chunk_rows(c):
        return slice(c * ch, (c + 1) * ch)

    def score_body(c, carry):
        rows = chunk_rows(c)
        dots = _dot_nt(ki_ref[rows, :], qi_st)
        sc = jnp.zeros((ch, qb), F32)
        for hh in range(IDX_HEADS):
            sc = sc + jnp.maximum(dots[:, hh * qb:(hh + 1) * qb], 0.0) * wit[hh:hh + 1, :]
        key = _sortable_key(sc * idx_scale)
        s_glob = c * ch + lax.broadcasted_iota(jnp.int32, (ch, qb), 0)
        key = jnp.where(s_glob < limit, key, INT_MIN)
        key_ref[rows, :] = key
        khi_ref[rows, :] = (key >> 16).astype(jnp.int16)
        klo_ref[rows, :] = ((key & 0xFFFF) - HALF16).astype(jnp.int16)
        return carry

    _static_loop(0, nch, score_body, 0)

    def count32(pred):
        def body(c, acc):
            ind = jnp.where(pred(key_ref[chunk_rows(c), :]), 1, 0)
            return acc + _tree_reduce(ind, acc.shape[0], jnp.add)
        acc = _static_loop(0, nch, body, jnp.zeros((4 * SUBLANES, qb), jnp.int32))
        return jnp.sum(acc, axis=0, keepdims=True)

    def search(n):
        def count16(ref, pred):
            parts = []
            for c in range(n):
                ind = jnp.where(pred(ref[c * ch:(c + 1) * ch, :]), jnp.int16(1), jnp.int16(0))
                parts.append(_tree_reduce(ind, 4 * SUBLANES, jnp.add))
            while len(parts) > 1:
                pairs = [a + b for a, b in zip(parts[0::2], parts[1::2])]
                parts = pairs + parts[len(pairs) * 2:]
            return jnp.sum(parts[0].astype(jnp.int32), axis=0, keepdims=True)

        def kth_largest16(ref, want):
            def bit_body(it, lo):
                cand = lo + jnp.left_shift(jnp.int32(1), 15 - it)
                c16 = cand.astype(jnp.int16)
                return jnp.where(count16(ref, lambda v: v >= c16) >= want, cand, lo)
            return lax.fori_loop(0, 16, bit_body, jnp.full((1, qb), -HALF16, jnp.int32))

        t_hi = kth_largest16(khi_ref, topk)
        t_hi16 = t_hi.astype(jnp.int16)
        n_above = count16(khi_ref, lambda v: v > t_hi16)
        for c in range(n):
            rows = slice(c * ch, (c + 1) * ch)
            klo_ref[rows, :] = jnp.where(khi_ref[rows, :] == t_hi16, klo_ref[rows, :],
                                         jnp.int16(-HALF16))
        t_lo = kth_largest16(klo_ref, topk - n_above)
        thr_ref[0:1, :] = jnp.maximum(jnp.left_shift(t_hi, 16) | (t_lo + HALF16), KEY_MIN_FINITE)

    thr_ref[...] = jnp.full(thr_ref.shape, KEY_MIN_FINITE, jnp.int32)

    @pl.when(nkb * Q_BLOCK > topk)
    def _():
        search(nch)

    thr = thr_ref[0:1, :]
    n_ge = count32(lambda key: key >= thr)
    has_ties = jnp.max(n_ge) > topk

    @pl.when(jnp.logical_not(has_ties))
    def _():
        def body(c, carry):
            rows = chunk_rows(c)
            key = key_ref[rows, :]
            sel = jnp.logical_and(key >= thr, key <= KEY_MAX_FINITE)
            madd_ref[rows, :] = jnp.where(sel, 0.0, NEG_BIG)
            return carry
        _static_loop(0, nch, body, 0)

    @pl.when(has_ties)
    def _():
        n_gt = count32(lambda key: key > thr)
        need = (topk - n_gt).astype(F32)
        lower = (lax.broadcasted_iota(jnp.int32, (ch, ch), 1)
                 < lax.broadcasted_iota(jnp.int32, (ch, ch), 0)).astype(BF16)

        def body(c, seen):
            rows = chunk_rows(c)
            key = key_ref[rows, :]
            eq = key == thr
            eqf = jnp.where(eq, 1.0, 0.0)
            before = _dot(lower, eqf.astype(BF16)) + seen
            sel = jnp.logical_or(key > thr, jnp.logical_and(eq, before < need))
            sel = jnp.logical_and(sel, key <= KEY_MAX_FINITE)
            madd_ref[rows, :] = jnp.where(sel, 0.0, NEG_BIG)
            return seen + jnp.sum(eqf, axis=0, keepdims=True)
        _static_loop(0, nch, body, jnp.zeros((1, qb), F32))

    col_groups = [(hh, qh, slice(hh * qb + qh * Q_BLOCK, hh * qb + (qh + 1) * Q_BLOCK))
                  for hh in range(N_HEADS) for qh in range(halves)]

    def mask_rows(rows, jb, mparts):
        madd = madd_ref[rows, :]
        lg = _dot_nt(k_ref[rows, :], q_all)
        out = []
        for g, (hh, qh, cols) in enumerate(col_groups):
            x = lg[:, cols] + madd[:, qh * Q_BLOCK:(qh + 1) * Q_BLOCK]
            d = None if jb is None else halves * i + qh - jb
            if d in (0, 1):
                x = x + bias_ref[d, hh]
            x_ref[rows, cols] = x
            out.append(jnp.maximum(mparts[g], _tree_reduce(x, SUBLANES, jnp.maximum)))
        return tuple(out)

    def far_body(c, mparts):
        return mask_rows(chunk_rows(c), None, mparts)

    def near_body(c, mparts):
        for jb in range(c * (ch // Q_BLOCK), (c + 1) * (ch // Q_BLOCK)):
            mparts = mask_rows(slice(jb * Q_BLOCK, (jb + 1) * Q_BLOCK), jb, mparts)
        return mparts

    n_far = max(nch - 2, 0)
    mparts = tuple(jnp.full((SUBLANES, Q_BLOCK), NEG_BIG, F32) for _ in col_groups)
    mparts = _static_loop(0, n_far, far_body, mparts)
    mparts = _static_loop(n_far, nch, near_body, mparts)
    m_rows = [jnp.max(mp, axis=0, keepdims=True) for mp in mparts]

    acc_ref[...] = jnp.zeros(acc_ref.shape, F32)
    ones = jnp.ones((ONES_ROWS, ch), BF16)

    def prob_chunk(c):
        rows = chunk_rows(c)
        for g, (_, _, cols) in enumerate(col_groups):
            p_ref[rows, cols] = jnp.exp2((x_ref[rows, cols] - m_rows[g]).astype(BF16))

    def pv_chunk(c):
        rows = chunk_rows(c)
        vt2 = jnp.concatenate([vt_ref[2 * c], vt_ref[2 * c + 1]], axis=1)
        for g in range(N_KV):
            lhs = jnp.concatenate([vt2[g * HEAD_DIM:(g + 1) * HEAD_DIM, :], ones], axis=0)
            acc_ref[g] += _dot(lhs, p_ref[rows, g * KV_REP * qb:(g + 1) * KV_REP * qb])

    prob_chunk(0)

    def pv_body(c, carry):
        pv_chunk(c - 1)
        prob_chunk(c)
        return carry

    _static_loop(1, nch, pv_body, 0)
    pv_chunk(nch - 1)

    for hp in range(N_HEADS // 2):
        parts = []
        for hh in (2 * hp, 2 * hp + 1):
            g, r = hh // KV_REP, hh % KV_REP
            cols = slice(r * qb, (r + 1) * qb)
            inv_l = 1.0 / acc_ref[g, HEAD_DIM:HEAD_DIM + 1, cols]
            parts.append(acc_ref[g, 0:HEAD_DIM, cols] * inv_l)
        st = jnp.concatenate(parts, axis=0)
        o_ref[:, hp * LANES:(hp + 1) * LANES] = st.T.astype(BF16)


def _attn_call(q_hm, qi_hm, wit, k2, vt, ki2, bias_tiles, batch, seq):
    t = k2.shape[0]
    nblk = seq // Q_BLOCK
    nstep = seq // ATTN_QB
    qb = ATTN_QB
    topk = min(TOPK_MAX, seq // 4)
    qrow = lambda b, i: (0, b * nstep + i, 0)
    return pl.pallas_call(
        functools.partial(_attn_kernel, topk=topk, max_chunks=nstep),
        grid=(batch, nstep),
        in_specs=[
            pl.BlockSpec((N_HEADS, qb, LANES), qrow),
            pl.BlockSpec((IDX_HEADS, qb, IDX_DIM), qrow),
            pl.BlockSpec((IDX_HEADS, qb), lambda b, i: (0, b * nstep + i)),
            pl.BlockSpec((seq, LANES), lambda b, i: (b, 0)),
            pl.BlockSpec((nblk, LANES, Q_BLOCK), lambda b, i: (b, 0, 0)),
            pl.BlockSpec((seq, IDX_DIM), lambda b, i: (b, 0)),
            pl.BlockSpec((3, N_HEADS, Q_BLOCK, Q_BLOCK), lambda b, i: (0, 0, 0, 0)),
        ],
        out_specs=pl.BlockSpec((qb, ATTN_W), lambda b, i: (b * nstep + i, 0)),
        out_shape=jax.ShapeDtypeStruct((t, ATTN_W), BF16),
        scratch_shapes=[
            pltpu.VMEM((seq, qb), jnp.int32),
            pltpu.VMEM((seq, qb), jnp.int16),
            pltpu.VMEM((seq, qb), jnp.int16),
            pltpu.VMEM((seq, qb), F32),
            pltpu.VMEM((SUBLANES, qb), jnp.int32),
            pltpu.VMEM((seq, N_HEADS * qb), F32),
            pltpu.VMEM((seq, N_HEADS * qb), BF16),
            pltpu.VMEM((N_KV, HEAD_DIM + ONES_ROWS, KV_REP * qb), F32),
        ],
        compiler_params=_cparams("parallel", "arbitrary"),
        name="attn",
    )(q_hm, qi_hm, wit, k2, vt, ki2, bias_tiles)


def _rel_bucket(rel):
    nb = REL_BUCKETS // 2
    max_exact = nb // 2
    ret = jnp.where(rel < 0, nb, 0)
    n = jnp.abs(rel)
    nf = jnp.maximum(n, 1).astype(jnp.float32)
    large = max_exact + (jnp.log(nf / max_exact) / math.log(REL_MAX_DIST / max_exact)
                         * (nb - max_exact)).astype(jnp.int32)
    large = jnp.minimum(large, nb - 1)
    return ret + jnp.where(n < max_exact, n, large)


def _bias_tiles(rel_bias):
    s = jnp.arange(Q_BLOCK, dtype=jnp.int32)[:, None]
    tq = jnp.arange(Q_BLOCK, dtype=jnp.int32)[None, :]
    d = jnp.arange(3, dtype=jnp.int32)[:, None, None]
    bucket = _rel_bucket(d * Q_BLOCK + tq - s)
    onehot = bucket[:, None, :, :, None] == jnp.arange(REL_BUCKETS, dtype=jnp.int32)
    tiles = jnp.sum(jnp.where(onehot, rel_bias.T[None, :, None, None, :], 0.0), axis=-1)
    return ((tiles - tiles[2:3, :, :1, :1]) * LOG2E).astype(F32)


AUG_ROWS = 48


def _merge_kernel(x_ref, vc_ref, at_ref, gc_ref, ga_ref, wco_ref, wao_ref, wo_ref, g_ref, wr_ref,
                  br_ref, xo_ref, ht_ref, gone_ref):
    y_conv = _dot(vc_ref[...], wco_ref[...])
    y_attn = _dot(at_ref[...], wao_ref[...])
    merged = gc_ref[...].astype(F32) * y_conv + ga_ref[...].astype(F32) * y_attn
    x = x_ref[...] + _dot(merged.astype(BF16), wo_ref[...])
    xo_ref[...] = x
    h = x * lax.rsqrt(jnp.mean(x * x, axis=-1, keepdims=True) + EPS) * g_ref[...]
    hb = h.astype(BF16)
    d = h.shape[1]
    ht_ref[0:d, :] = hb.astype(F32).T.astype(BF16)

    logits = _dot(hb, wr_ref[...]) + br_ref[...]
    lane = lax.broadcasted_iota(jnp.int32, logits.shape, 1)
    neg_inf = -jnp.inf
    gl = jnp.where(lane < N_GROUPS, logits, neg_inf)
    gmax = jnp.max(gl, axis=-1, keepdims=True)
    grp = jnp.min(jnp.where(gl == gmax, lane, LANES), axis=-1, keepdims=True)
    p_grp = 1.0 / jnp.sum(jnp.exp(gl - gmax), axis=-1, keepdims=True)
    e_lo = ROUTER_E_OFF + grp * EXPERTS_PER_GROUP
    in_grp = jnp.logical_and(lane >= e_lo, lane < e_lo + EXPERTS_PER_GROUP)
    el = jnp.where(in_grp, logits, neg_inf)
    v1 = jnp.max(el, axis=-1, keepdims=True)
    i1 = jnp.min(jnp.where(jnp.logical_and(in_grp, el == v1), lane, LANES), axis=-1, keepdims=True)
    rest = jnp.logical_and(in_grp, lane != i1)
    el2 = jnp.where(rest, logits, neg_inf)
    v2 = jnp.max(el2, axis=-1, keepdims=True)
    i2 = jnp.min(jnp.where(jnp.logical_and(rest, el2 == v2), lane, LANES), axis=-1, keepdims=True)
    e2 = jnp.exp(v2 - v1)
    p1 = p_grp / (1.0 + e2)
    p2 = p_grp * e2 / (1.0 + e2)
    gone_ref[...] = jnp.where(lane == grp, 1.0, 0.0).astype(BF16)
    w8 = jnp.where(lane == i1 - e_lo, p1, jnp.where(lane == i2 - e_lo, p2, 0.0))
    w8t = w8.T[0:16, :]
    hi = w8t.astype(BF16)
    r1 = w8t - hi.astype(F32)
    mid = r1.astype(BF16)
    lo = (r1 - mid.astype(F32)).astype(BF16)
    ht_ref[d:d + 16, :] = hi
    ht_ref[d + 16:d + 32, :] = mid
    ht_ref[d + 32:d + 48, :] = lo


def _merge_call(x2, vc, at, gc, ga, wco, wao, wo, g, wr, br):
    t, d = x2.shape
    tm = MERGE_TM
    row = lambda i: (i, 0)
    const = lambda i: (0, 0)
    return pl.pallas_call(
        _merge_kernel,
        grid=(t // tm,),
        in_specs=[
            pl.BlockSpec((tm, d), row),
            pl.BlockSpec((tm, CONV_CH), row),
            pl.BlockSpec((tm, ATTN_W), row),
            pl.BlockSpec((tm, d), row),
            pl.BlockSpec((tm, d), row),
            pl.BlockSpec((CONV_CH, d), const),
            pl.BlockSpec((ATTN_W, d), const),
            pl.BlockSpec((d, d), const),
            pl.BlockSpec((1, d), const),
            pl.BlockSpec((d, LANES), const),
            pl.BlockSpec((1, LANES), const),
        ],
        out_specs=(pl.BlockSpec((tm, d), row), pl.BlockSpec((d + AUG_ROWS, tm), lambda i: (0, i)),
                   pl.BlockSpec((tm, LANES), row)),
        out_shape=(jax.ShapeDtypeStruct((t, d), F32),
                   jax.ShapeDtypeStruct((d + AUG_ROWS, t), BF16),
                   jax.ShapeDtypeStruct((t, LANES), BF16)),
        compiler_params=_cparams("parallel"),
        name="merge",
    )(x2, vc, at, gc, ga, wco, wao, wo, g, wr, br)


def _pack_router(w_gr, b_gr, w_er, b_er):
    d = w_gr.shape[0]
    w = jnp.zeros((d, LANES), F32)
    w = w.at[:, :N_GROUPS].set(w_gr)
    w = w.at[:, ROUTER_E_OFF:ROUTER_E_OFF + N_EXPERTS].set(
        jnp.moveaxis(w_er, 0, 1).reshape(d, N_EXPERTS))
    b = jnp.zeros((1, LANES), F32)
    b = b.at[0, :N_GROUPS].set(b_gr)
    b = b.at[0, ROUTER_E_OFF:ROUTER_E_OFF + N_EXPERTS].set(b_er.reshape(N_EXPERTS))
    return w.astype(BF16), b


SLOT_CHUNK = 256
FFN_SPLIT = 8
MOE_MAX_CHUNKS = MOE_TM // SLOT_CHUNK + N_GROUPS - 1
MOE_STEPS = MOE_MAX_CHUNKS + 1


def _plan_kernel(gone_ref, ltri_ref, utri_ref, slot_ref, cnt_ref):
    gone = gone_ref[...]
    before = _dot(ltri_ref[...], gone)
    last = gone.shape[0] - 1
    cnt = before[last:last + 1, :] + gone[last:last + 1, :].astype(F32)
    nchunks = jnp.floor((cnt + (SLOT_CHUNK - 1)) * (1.0 / SLOT_CHUNK))
    start = _dot(jnp.broadcast_to(nchunks, (SUBLANES, LANES)).astype(BF16), utri_ref[...])[0:1, :]
    slot = jnp.sum(gone.astype(F32) * (before + start * SLOT_CHUNK), axis=-1, keepdims=True)
    slot_ref[...] = jnp.broadcast_to(slot.astype(jnp.int32), slot_ref.shape)
    cnt_ref[0] = jnp.broadcast_to(cnt, (SUBLANES, LANES)).astype(jnp.int32)


def _plan_call(gone):
    t = gone.shape[0]
    tm = MOE_TM
    nt = t // tm
    r = lax.broadcasted_iota(jnp.int32, (tm, tm), 0)
    c = lax.broadcasted_iota(jnp.int32, (tm, tm), 1)
    ltri = (c < r).astype(BF16)
    utri = (lax.broadcasted_iota(jnp.int32, (LANES, LANES), 0)
            < lax.broadcasted_iota(jnp.int32, (LANES, LANES), 1)).astype(BF16)
    return pl.pallas_call(
        _plan_kernel,
        grid=(nt,),
        in_specs=[pl.BlockSpec((tm, LANES), lambda i: (i, 0)),
                  pl.BlockSpec((tm, tm), lambda i: (0, 0)),
                  pl.BlockSpec((LANES, LANES), lambda i: (0, 0))],
        out_specs=(pl.BlockSpec((tm, LANES), lambda i: (i, 0)),
                   pl.BlockSpec((1, SUBLANES, LANES), lambda i: (i, 0, 0))),
        out_shape=(jax.ShapeDtypeStruct((t, LANES), jnp.int32),
                   jax.ShapeDtypeStruct((nt, SUBLANES, LANES), jnp.int32)),
        compiler_params=_cparams("parallel"),
        name="moe_plan",
    )(gone, ltri, utri)


def _moe_kernel(cg_ref, nc_ref, ht_ref, slot_ref, wgu_ref, wd_ref, o_ref,
                pt_ref, acc_ref, xg_ref, cw_ref, yt_ref):
    i = pl.program_id(0)
    j = pl.program_id(1)
    nc = nc_ref[i]
    tm = slot_ref.shape[0]
    d = acc_ref.shape[1]

    def gather(jn, slot_idx):
        xa = _dot(ht_ref[...], pt_ref[jn])
        xg_ref[slot_idx] = xa[0:d, :].astype(BF16)
        cw_ref[slot_idx] = xa[d:d + 16, :] + xa[d + 16:d + 32, :] + xa[d + 32:d + 48, :]

    def scatter(jp, slot_idx):
        acc_ref[...] += _dot_nt(pt_ref[jp], yt_ref[slot_idx])

    @pl.when(j == 0)
    def _():
        acc_ref[...] = jnp.zeros(acc_ref.shape, F32)
        yt_ref[...] = jnp.zeros(yt_ref.shape, BF16)
        slot = slot_ref[...]
        lane = lax.broadcasted_iota(jnp.int32, (tm, LANES), 1)
        for jj in range(MOE_MAX_CHUNKS):
            for half in range(SLOT_CHUNK // LANES):
                off = jj * SLOT_CHUNK + half * LANES
                pt_ref[jj, :, half * LANES:(half + 1) * LANES] = jnp.where(
                    slot == lane + off, 1.0, 0.0).astype(BF16)
        gather(0, 0)

    @pl.when(j < nc)
    def _():
        cur = j % 2
        scatter(jnp.maximum(j - 1, 0), 1 - cur)
        xg = xg_ref[cur]
        cw = cw_ref[cur]
        es = EXPERTS_PER_GROUP // FFN_SPLIT
        sf = es * EXPERT_FF
        yt = None
        for s in range(FFN_SPLIT):
            ab = _dot(wgu_ref[0, 2 * sf * s:2 * sf * (s + 1), :], xg)
            mids = []
            for e in range(es):
                a = ab[e * EXPERT_FF:(e + 1) * EXPERT_FF, :]
                b = ab[sf + e * EXPERT_FF:sf + (e + 1) * EXPERT_FF, :]
                c = cw[s * es + e:s * es + e + 1, :]
                mids.append((a * _sigmoid(a) * b * c).astype(BF16))
            part = _dot(wd_ref[0, :, sf * s:sf * (s + 1)], jnp.concatenate(mids, axis=0))
            yt = part if yt is None else yt + part
        yt_ref[cur] = yt.astype(BF16)
        gather(jnp.minimum(j + 1, MOE_MAX_CHUNKS - 1), 1 - cur)

    @pl.when(j == nc)
    def _():
        scatter(nc - 1, (nc - 1) % 2)

    @pl.when(j == pl.num_programs(1) - 1)
    def _():
        o_ref[...] = acc_ref[...].astype(BF16)


def _moe_call(ht_aug, slot, chunk_group, n_chunks, wgu_t, wd_t):
    da, t = ht_aug.shape
    d = da - AUG_ROWS
    tm = MOE_TM
    nt = t // tm
    ff = EXPERTS_PER_GROUP * EXPERT_FF
    wmap = lambda i, j, cg, nc: (cg[i * MOE_STEPS + j], 0, 0)
    grid_spec = pltpu.PrefetchScalarGridSpec(
        num_scalar_prefetch=2,
        grid=(nt, MOE_STEPS),
        in_specs=[
            pl.BlockSpec((da, tm), lambda i, j, cg, nc: (0, i)),
            pl.BlockSpec((tm, LANES), lambda i, j, cg, nc: (i, 0)),
            pl.BlockSpec((1, 2 * ff, d), wmap),
            pl.BlockSpec((1, d, ff), wmap),
        ],
        out_specs=pl.BlockSpec((tm, d), lambda i, j, cg, nc: (i, 0)),
        scratch_shapes=[pltpu.VMEM((MOE_MAX_CHUNKS, tm, SLOT_CHUNK), BF16),
                        pltpu.VMEM((tm, d), F32),
                        pltpu.VMEM((2, d, SLOT_CHUNK), BF16),
                        pltpu.VMEM((2, 16, SLOT_CHUNK), F32),
                        pltpu.VMEM((2, d, SLOT_CHUNK), BF16)],
    )
    return pl.pallas_call(
        _moe_kernel,
        grid_spec=grid_spec,
        out_shape=jax.ShapeDtypeStruct((t, d), BF16),
        compiler_params=_cparams("parallel", "arbitrary"),
        name="moe",
    )(chunk_group, n_chunks, ht_aug, slot, wgu_t, wd_t)


def _chunk_tables(cnt):
    per_group = (cnt[:, 0, :N_GROUPS] + SLOT_CHUNK - 1) // SLOT_CHUNK
    ends = jnp.cumsum(per_group, axis=1)
    n_chunks = ends[:, -1]
    j = jnp.arange(MOE_STEPS, dtype=jnp.int32)[None, :]
    jj = jnp.minimum(j, n_chunks[:, None] - 1)
    group = jnp.sum((jj[:, :, None] >= ends[:, None, :]).astype(jnp.int32), axis=-1)
    return group.reshape(-1).astype(jnp.int32), n_chunks.astype(jnp.int32)


def _pack_expert_weights(w_gate, w_up, w_down):
    ne, d, f = w_gate.shape
    tr = lambda w: jnp.swapaxes(w, 1, 2).reshape(N_GROUPS, FFN_SPLIT, -1, d)
    wgu = jnp.concatenate([tr(w_gate), tr(w_up)], axis=2).reshape(
        N_GROUPS, 2 * EXPERTS_PER_GROUP * f, d).astype(BF16)
    wd = jnp.swapaxes(w_down.reshape(N_GROUPS, EXPERTS_PER_GROUP * f, d), 1, 2).astype(BF16)
    return wgu, wd


def _norm_kernel(x_ref, y_ref, g_ref, o_ref):
    x = x_ref[...] + y_ref[...].astype(F32)
    o_ref[...] = x * lax.rsqrt(jnp.mean(x * x, axis=-1, keepdims=True) + EPS) * g_ref[...]


def _norm_call(x2, y2, g):
    t, d = x2.shape
    tm = NORM_TM
    row = lambda i: (i, 0)
    return pl.pallas_call(
        _norm_kernel,
        grid=(t // tm,),
        in_specs=[pl.BlockSpec((tm, d), row), pl.BlockSpec((tm, d), row),
                  pl.BlockSpec((1, d), lambda i: (0, 0))],
        out_specs=pl.BlockSpec((tm, d), row),
        out_shape=jax.ShapeDtypeStruct((t, d), F32),
        compiler_params=_cparams("parallel"),
        name="final_norm",
    )(x2, y2, g)


def kernel(x, g_mix, w_in, conv_w, conv_b, conv_ln_g, conv_ln_b, w_conv_out, w_attn_out, w_out, rel_bias, g_ffn, w_group_router, b_group_router, w_expert_router, b_expert_router, w_e_gate, w_e_up, w_e_down, g_final):
    batch, seq, d = x.shape
    depth = g_mix.shape[0]
    t = batch * seq
    assert seq % CONV_TL == 0 and seq % KEY_CHUNK == 0 and t % MOE_TM == 0
    x2 = x.reshape(t, d)
    y2 = None
    bias_tiles = _bias_tiles(rel_bias)
    for l in range(depth):
        w_pack = _pack_proj_weights(w_in[l])
        outs = _proj_call(x2, y2, g_mix[l][None, :], w_pack)
        if y2 is not None:
            x2, outs = outs[0], outs[1:]
        u, q_hm, k2, vt, qi_hm, ki2, wit, gc, ga = outs
        vc = _conv_call(u.reshape(batch, seq, 2 * CONV_CH), conv_w[l], conv_b[l][None, :],
                        conv_ln_g[l][None, :], conv_ln_b[l][None, :]).reshape(t, CONV_CH)
        at = _attn_call(q_hm, qi_hm, wit, k2, vt, ki2, bias_tiles, batch, seq)
        wr, br = _pack_router(w_group_router[l], b_group_router[l], w_expert_router[l],
                              b_expert_router[l])
        x2, ht_aug, gone = _merge_call(x2, vc, at, gc, ga, w_conv_out[l].astype(BF16),
                                       w_attn_out[l].astype(BF16), w_out[l].astype(BF16),
                                       g_ffn[l][None, :], wr, br)
        slot, cnt = _plan_call(gone)
        chunk_group, n_chunks = _chunk_tables(cnt)
        wgu_t, wd_t = _pack_expert_weights(w_e_gate[l], w_e_up[l], w_e_down[l])
        y2 = _moe_call(ht_aug, slot, chunk_group, n_chunks, wgu_t, wd_t)
    return _norm_call(x2, y2, g_final[None, :]).reshape(batch, seq, d)
```

```python
import functools
import math

import jax
import jax.numpy as jnp
import numpy as np
from jax import lax
from jax.experimental import pallas as pl
from jax.experimental.pallas import tpu as pltpu

CHUNK = 64
CONV_CH = 512
CONV_WIDTH = 31
N_HEADS = 8
HEAD_DIM = 64
N_KV = 2
KV_REP = N_HEADS // N_KV
ATTN_W = N_HEADS * HEAD_DIM
IDX_HEADS = 8
IDX_DIM = 32
TOPK_MAX = 256
Q_BLOCK = 128
REL_BUCKETS = 32
REL_MAX_DIST = 128
N_GROUPS = 4
EXPERTS_PER_GROUP = 8
N_EXPERTS = N_GROUPS * EXPERTS_PER_GROUP
EXPERT_FF = 256
EPS = 1e-6

LANES = 128
SUBLANES = 8
VMEM_LIMIT_BYTES = 56 * 1024 * 1024

LOG2E = math.log2(math.e)
NEG_BIG = -1e30
INT_MIN = -(2 ** 31)
KEY_MIN_FINITE = -0x7F800000
KEY_MAX_FINITE = 0x7F7FFFFF

BF16 = jnp.bfloat16
F32 = jnp.float32

PROJ_TM = 512
CONV_TL = 256
CONV_HALO = 32
CONV_ROWS = 128
MERGE_TM = 512
MOE_TM = 1024
NORM_TM = 1024

ROUTER_E_OFF = 32


def _cparams(*sem):
    return pltpu.CompilerParams(dimension_semantics=sem, vmem_limit_bytes=VMEM_LIMIT_BYTES)


def _sigmoid(x):
    return 1.0 / (1.0 + jnp.exp(-x))


def _dot(a, b):
    return jnp.dot(a, b, preferred_element_type=F32)


def _dot_nt(a, b):
    return lax.dot_general(a, b, (((1,), (1,)), ((), ())), preferred_element_type=F32)


SEG_U = (0, 2 * CONV_CH)
SEG_Q = (SEG_U[1], SEG_U[1] + N_HEADS * LANES)
SEG_K = (SEG_Q[1], SEG_Q[1] + LANES)
SEG_V = (SEG_K[1], SEG_K[1] + LANES)
SEG_QI = (SEG_V[1], SEG_V[1] + IDX_HEADS * IDX_DIM)
SEG_KI = (SEG_QI[1], SEG_QI[1] + LANES)
SEG_WI = (SEG_KI[1], SEG_KI[1] + LANES)
SEG_GC = (SEG_WI[1], SEG_WI[1] + 1024)
SEG_GA = (SEG_GC[1], SEG_GC[1] + 1024)
PROJ_COLS = SEG_GA[1]


def _proj_kernel(*refs, has_y):
    if has_y:
        x_ref, y_ref, g_ref, w_ref, xo_ref = refs[:5]
        x = x_ref[...] + y_ref[...].astype(F32)
        xo_ref[...] = x
    else:
        x_ref, g_ref, w_ref = refs[:3]
        x = x_ref[...]
    u_ref, q_ref, k_ref, vt_ref, qi_ref, ki_ref, wit_ref, gc_ref, ga_ref = refs[-9:]
    h = x * lax.rsqrt(jnp.mean(x * x, axis=-1, keepdims=True) + EPS) * g_ref[...]
    hb = h.astype(BF16)

    def seg(s):
        return _dot(hb, w_ref[:, s[0]:s[1]])

    u_ref[...] = seg(SEG_U).astype(BF16)
    pq = seg(SEG_Q)
    for hh in range(N_HEADS):
        q_ref[hh] = pq[:, hh * LANES:(hh + 1) * LANES].astype(BF16)
    k_ref[...] = seg(SEG_K).astype(BF16)
    pv = seg(SEG_V)
    tm = pv.shape[0]
    for c in range(tm // LANES):
        vt_ref[c] = pv[c * LANES:(c + 1) * LANES, :].T.astype(BF16)
    pqi = seg(SEG_QI)
    for hh in range(IDX_HEADS):
        qi_ref[hh] = pqi[:, hh * IDX_DIM:(hh + 1) * IDX_DIM].astype(BF16)
    ki_ref[...] = seg(SEG_KI)[:, :IDX_DIM].astype(BF16)
    pwi = seg(SEG_WI)
    for c in range(tm // LANES):
        wit_ref[:, c * LANES:(c + 1) * LANES] = pwi[c * LANES:(c + 1) * LANES, :].T[:IDX_HEADS, :]
    gc_ref[...] = _sigmoid(seg(SEG_GC)).astype(BF16)
    ga_ref[...] = _sigmoid(seg(SEG_GA)).astype(BF16)


def _proj_call(x2, y2, g, w_pack):
    t, d = x2.shape
    tm = PROJ_TM
    nq = t // Q_BLOCK
    row = lambda i: (i, 0)
    has_y = y2 is not None
    out_shape = (jax.ShapeDtypeStruct((t, d), F32),) if has_y else ()
    out_specs = (pl.BlockSpec((tm, d), row),) if has_y else ()
    out_shape += (
        jax.ShapeDtypeStruct((t, 2 * CONV_CH), BF16),
        jax.ShapeDtypeStruct((N_HEADS, t, LANES), BF16),
        jax.ShapeDtypeStruct((t, LANES), BF16),
        jax.ShapeDtypeStruct((nq, LANES, Q_BLOCK), BF16),
        jax.ShapeDtypeStruct((IDX_HEADS, t, IDX_DIM), BF16),
        jax.ShapeDtypeStruct((t, IDX_DIM), BF16),
        jax.ShapeDtypeStruct((IDX_HEADS, t), F32),
        jax.ShapeDtypeStruct((t, d), BF16),
        jax.ShapeDtypeStruct((t, d), BF16),
    )
    out_specs += (
        pl.BlockSpec((tm, 2 * CONV_CH), row),
        pl.BlockSpec((N_HEADS, tm, LANES), lambda i: (0, i, 0)),
        pl.BlockSpec((tm, LANES), row),
        pl.BlockSpec((tm // Q_BLOCK, LANES, Q_BLOCK), lambda i: (i, 0, 0)),
        pl.BlockSpec((IDX_HEADS, tm, IDX_DIM), lambda i: (0, i, 0)),
        pl.BlockSpec((tm, IDX_DIM), row),
        pl.BlockSpec((IDX_HEADS, tm), lambda i: (0, i)),
        pl.BlockSpec((tm, d), row),
        pl.BlockSpec((tm, d), row),
    )
    streams = (x2, y2) if has_y else (x2,)
    return pl.pallas_call(
        functools.partial(_proj_kernel, has_y=has_y),
        grid=(t // tm,),
        in_specs=[pl.BlockSpec((tm, d), row) for _ in streams] + [
            pl.BlockSpec((1, d), lambda i: (0, 0)),
            pl.BlockSpec((d, PROJ_COLS), lambda i: (0, 0)),
        ],
        out_specs=out_specs,
        out_shape=out_shape,
        compiler_params=_cparams("parallel"),
        name="proj",
    )(*streams, g, w_pack)


def _pack_proj_weights(w_in):
    d = w_in.shape[0]
    splits = (2 * CONV_CH, ATTN_W, N_KV * HEAD_DIM, N_KV * HEAD_DIM, IDX_HEADS * IDX_DIM,
              IDX_DIM, IDX_HEADS, d, d)
    offs = np.concatenate([[0], np.cumsum(splits)])
    w_u, w_q, w_k, w_v, w_qi, w_ki, w_wi, w_gc, w_ga = [
        w_in[:, int(offs[i]):int(offs[i + 1])] for i in range(len(splits))]
    w_q = (w_q * (HEAD_DIM ** -0.5 * LOG2E)).reshape(d, N_KV, KV_REP, HEAD_DIM)
    q_pad = jnp.zeros((d, N_KV, KV_REP, N_KV, HEAD_DIM), w_in.dtype)
    for gidx in range(N_KV):
        q_pad = q_pad.at[:, gidx, :, gidx, :].set(w_q[:, gidx])
    q_pad = q_pad.reshape(d, N_HEADS * LANES)
    pad = lambda w: jnp.pad(w, ((0, 0), (0, LANES - w.shape[1])))
    packed = jnp.concatenate([w_u, q_pad, w_k, w_v, w_qi, pad(w_ki), pad(w_wi), w_gc, w_ga], axis=1)
    assert packed.shape[1] == PROJ_COLS
    return packed.astype(BF16)


def _conv_kernel(u_ref, halo_ref, cw_ref, cb_ref, lg_ref, lb_ref, o_ref, ext_ref, y_ref):
    i = pl.program_id(1)
    tl = u_ref.shape[1]

    def glu(u):
        u = u.astype(F32)
        return u[:, :CONV_CH] * _sigmoid(u[:, CONV_CH:])

    halo = glu(halo_ref[0])
    ext_ref[0:CONV_HALO, :] = jnp.where(i > 0, halo, 0.0)
    ext_ref[CONV_HALO:, :] = glu(u_ref[0])
    first = CONV_HALO - (CONV_WIDTH - 1)
    for c in range(CONV_CH // LANES):
        cs = slice(c * LANES, (c + 1) * LANES)
        for r0 in range(0, tl, CONV_ROWS):
            out = cb_ref[:, cs]
            for b in range(SUBLANES):
                win = CONV_ROWS + (SUBLANES if b else 0)
                part = jnp.zeros((win, LANES), F32)
                for j in range(CONV_WIDTH):
                    if (first + j) % SUBLANES == b:
                        lo = first + j - b + r0
                        part = part + cw_ref[j:j + 1, cs] * ext_ref[lo:lo + win, cs]
                out = out + part[b:b + CONV_ROWS]
            y_ref[r0:r0 + CONV_ROWS, cs] = out
    y = y_ref[...]
    mu = jnp.mean(y, axis=-1, keepdims=True)
    yc = y - mu
    var = jnp.mean(yc * yc, axis=-1, keepdims=True)
    yn = yc * lax.rsqrt(var + EPS) * lg_ref[...] + lb_ref[...]
    o_ref[0] = (yn * _sigmoid(yn)).astype(BF16)


def _conv_call(u3, conv_w, conv_b, ln_g, ln_b):
    b, l, _ = u3.shape
    tl = CONV_TL
    halo_blocks = tl // CONV_HALO
    const = lambda bi, i: (0, 0)
    return pl.pallas_call(
        _conv_kernel,
        grid=(b, l // tl),
        in_specs=[
            pl.BlockSpec((1, tl, 2 * CONV_CH), lambda bi, i: (bi, i, 0)),
            pl.BlockSpec((1, CONV_HALO, 2 * CONV_CH),
                         lambda bi, i: (bi, jnp.maximum(i * halo_blocks - 1, 0), 0)),
            pl.BlockSpec((CONV_WIDTH, CONV_CH), const),
            pl.BlockSpec((1, CONV_CH), const),
            pl.BlockSpec((1, CONV_CH), const),
            pl.BlockSpec((1, CONV_CH), const),
        ],
        out_specs=pl.BlockSpec((1, tl, CONV_CH), lambda bi, i: (bi, i, 0)),
        out_shape=jax.ShapeDtypeStruct((b, l, CONV_CH), BF16),
        scratch_shapes=[pltpu.VMEM((CONV_HALO + tl, CONV_CH), F32), pltpu.VMEM((tl, CONV_CH), F32)],
        compiler_params=_cparams("parallel", "parallel"),
        name="conv",
    )(u3, u3, conv_w, conv_b, ln_g, ln_b)


def _sortable_key(score):
    b = lax.bitcast_convert_type(score, jnp.int32)
    return b ^ ((b >> 31) & 0x7FFFFFFF)


KEY_CHUNK = 2 * Q_BLOCK
HALF16 = 1 << 15
ONES_ROWS = 16


def _tree_reduce(x, rows, op):
    parts = [x[j * rows:(j + 1) * rows] for j in range(x.shape[0] // rows)]
    while len(parts) > 1:
        parts = [op(a, b) for a, b in zip(parts[0::2], parts[1::2])]
    return parts[0]


def _attn_kernel(*refs, topk, max_chunks):
    nch = (pl.program_id(1) + 2) // 2
    lax.switch(nch - 1, [functools.partial(_attn_body, *refs, topk=topk, nch=n + 1)
                         for n in range(max_chunks)])


def _static_loop(lo, hi, body, carry):
    for c in range(lo, hi):
        carry = body(c, carry)
    return carry


def _attn_body(q_ref, qi_ref, wit_ref, k_ref, vt_ref, ki_ref, bias_ref, o_ref,
               key_ref, khi_ref, klo_ref, madd_ref, thr_ref, x_ref, p_ref, acc_ref, *, topk, nch):
    i = pl.program_id(1)
    qb = Q_BLOCK
    ch = KEY_CHUNK
    nkb = i + 1
    q_all = q_ref[...].reshape(N_HEADS * qb, LANES)
    qi_st = qi_ref[...].reshape(IDX_HEADS * qb, IDX_DIM)
    wit = wit_ref[...]
    t_loc = lax.broadcasted_iota(jnp.int32, (1, qb), 1)
    limit = ((i * qb + t_loc) // CHUNK + 1) * CHUNK
    idx_scale = (IDX_DIM ** -0.5) * (IDX_HEADS ** -0.5)

    def chunk_rows(c):
        return slice(c * ch, (c + 1) * ch)

    def score_body(c, carry):
        rows = chunk_rows(c)
        dots = _dot_nt(ki_ref[rows, :], qi_st)
        sc = jnp.zeros((ch, qb), F32)
        for hh in range(IDX_HEADS):
            sc = sc + jnp.maximum(dots[:, hh * qb:(hh + 1) * qb], 0.0) * wit[hh:hh + 1, :]
        key = _sortable_key(sc * idx_scale)
        s_glob = c * ch + lax.broadcasted_iota(jnp.int32, (ch, qb), 0)
        key = jnp.where(s_glob < limit, key, INT_MIN)
        key_ref[rows, :] = key
        khi_ref[rows, :] = (key >> 16).astype(jnp.int16)
        klo_ref[rows, :] = ((key & 0xFFFF) - HALF16).astype(jnp.int16)
        return carry

    _static_loop(0, nch, score_body, 0)

    def count32(pred):
        def body(c, acc):
            ind = jnp.where(pred(key_ref[chunk_rows(c), :]), 1, 0)
            return acc + _tree_reduce(ind, acc.shape[0], jnp.add)
        acc = _static_loop(0, nch, body, jnp.zeros((4 * SUBLANES, qb), jnp.int32))
        return jnp.sum(acc, axis=0, keepdims=True)

    def search(n):
        def count16(ref, pred):
            parts = []
            for c in range(n):
                ind = jnp.where(pred(ref[c * ch:(c + 1) * ch, :]), jnp.int16(1), jnp.int16(0))
                parts.append(_tree_reduce(ind, 4 * SUBLANES, jnp.add))
            while len(parts) > 1:
                pairs = [a + b for a, b in zip(parts[0::2], parts[1::2])]
                parts = pairs + parts[len(pairs) * 2:]
            return jnp.sum(parts[0].astype(jnp.int32), axis=0, keepdims=True)

        def kth_largest16(ref, want):
            def bit_body(it, lo):
                cand = lo + jnp.left_shift(jnp.int32(1), 15 - it)
                c16 = cand.astype(jnp.int16)
                return jnp.where(count16(ref, lambda v: v >= c16) >= want, cand, lo)
            return lax.fori_loop(0, 16, bit_body, jnp.full((1, qb), -HALF16, jnp.int32))

        t_hi = kth_largest16(khi_ref, topk)
        t_hi16 = t_hi.astype(jnp.int16)
        n_above = count16(khi_ref, lambda v: v > t_hi16)
        for c in range(n):
            rows = slice(c * ch, (c + 1) * ch)
            klo_ref[rows, :] = jnp.where(khi_ref[rows, :] == t_hi16, klo_ref[rows, :],
                                         jnp.int16(-HALF16))
        t_lo = kth_largest16(klo_ref, topk - n_above)
        thr_ref[0:1, :] = jnp.maximum(jnp.left_shift(t_hi, 16) | (t_lo + HALF16), KEY_MIN_FINITE)

    thr_ref[...] = jnp.full(thr_ref.shape, KEY_MIN_FINITE, jnp.int32)

    @pl.when(nkb * qb > topk)
    def _():
        search(nch)

    thr = thr_ref[0:1, :]
    n_ge = count32(lambda key: key >= thr)
    has_ties = jnp.max(n_ge) > topk

    @pl.when(jnp.logical_not(has_ties))
    def _():
        def body(c, carry):
            rows = chunk_rows(c)
            key = key_ref[rows, :]
            sel = jnp.logical_and(key >= thr, key <= KEY_MAX_FINITE)
            madd_ref[rows, :] = jnp.where(sel, 0.0, NEG_BIG)
            return carry
        _static_loop(0, nch, body, 0)

    @pl.when(has_ties)
    def _():
        n_gt = count32(lambda key: key > thr)
        need = (topk - n_gt).astype(F32)
        lower = (lax.broadcasted_iota(jnp.int32, (ch, ch), 1)
                 < lax.broadcasted_iota(jnp.int32, (ch, ch), 0)).astype(BF16)

        def body(c, seen):
            rows = chunk_rows(c)
            key = key_ref[rows, :]
            eq = key == thr
            eqf = jnp.where(eq, 1.0, 0.0)
            before = _dot(lower, eqf.astype(BF16)) + seen
            sel = jnp.logical_or(key > thr, jnp.logical_and(eq, before < need))
            sel = jnp.logical_and(sel, key <= KEY_MAX_FINITE)
            madd_ref[rows, :] = jnp.where(sel, 0.0, NEG_BIG)
            return seen + jnp.sum(eqf, axis=0, keepdims=True)
        _static_loop(0, nch, body, jnp.zeros((1, qb), F32))

    def mask_rows(rows, tile, mparts):
        madd = madd_ref[rows, :]
        lg = _dot_nt(k_ref[rows, :], q_all)
        out = []
        for hh in range(N_HEADS):
            cols = slice(hh * qb, (hh + 1) * qb)
            x = lg[:, cols] + madd
            if tile is not None:
                x = x + bias_ref[tile, hh]
            x_ref[rows, cols] = x
            out.append(jnp.maximum(mparts[hh], _tree_reduce(x, SUBLANES, jnp.maximum)))
        return tuple(out)

    def far_body(c, mparts):
        return mask_rows(chunk_rows(c), None, mparts)

    def near_body(c, mparts):
        for half in range(ch // qb):
            jb = c * (ch // qb) + half
            d = i - jb
            tile = jnp.where(d == 0, 0, jnp.where(d == 1, 1, 2))
            mparts = mask_rows(slice(jb * qb, (jb + 1) * qb), tile, mparts)
        return mparts

    n_far = max(nch - 2, 0)
    mparts = tuple(jnp.full((SUBLANES, qb), NEG_BIG, F32) for _ in range(N_HEADS))
    mparts = _static_loop(0, n_far, far_body, mparts)
    mparts = _static_loop(n_far, nch, near_body, mparts)
    m_rows = [jnp.max(mp, axis=0, keepdims=True) for mp in mparts]

    acc_ref[...] = jnp.zeros(acc_ref.shape, F32)
    ones = jnp.ones((ONES_ROWS, ch), BF16)

    def prob_chunk(c):
        rows = chunk_rows(c)
        for hh in range(N_HEADS):
            cols = slice(hh * qb, (hh + 1) * qb)
            p_ref[rows, cols] = jnp.exp2((x_ref[rows, cols] - m_rows[hh]).astype(BF16))

    def pv_chunk(c):
        rows = chunk_rows(c)
        vt2 = jnp.concatenate([vt_ref[2 * c], vt_ref[2 * c + 1]], axis=1)
        for g in range(N_KV):
            lhs = jnp.concatenate([vt2[g * HEAD_DIM:(g + 1) * HEAD_DIM, :], ones], axis=0)
            acc_ref[g] += _dot(lhs, p_ref[rows, g * KV_REP * qb:(g + 1) * KV_REP * qb])

    prob_chunk(0)

    def pv_body(c, carry):
        pv_chunk(c - 1)
        prob_chunk(c)
        return carry

    _static_loop(1, nch, pv_body, 0)
    pv_chunk(nch - 1)

    for hp in range(N_HEADS // 2):
        parts = []
        for hh in (2 * hp, 2 * hp + 1):
            g, r = hh // KV_REP, hh % KV_REP
            cols = slice(r * qb, (r + 1) * qb)
            inv_l = 1.0 / acc_ref[g, HEAD_DIM:HEAD_DIM + 1, cols]
            parts.append(acc_ref[g, 0:HEAD_DIM, cols] * inv_l)
        st = jnp.concatenate(parts, axis=0)
        o_ref[:, hp * LANES:(hp + 1) * LANES] = st.T.astype(BF16)


def _attn_call(q_hm, qi_hm, wit, k2, vt, ki2, bias_tiles, batch, seq):
    t = k2.shape[0]
    nblk = seq // Q_BLOCK
    topk = min(TOPK_MAX, seq // 4)
    qrow = lambda b, i: (0, b * nblk + i, 0)
    return pl.pallas_call(
        functools.partial(_attn_kernel, topk=topk, max_chunks=seq // KEY_CHUNK),
        grid=(batch, nblk),
        in_specs=[
            pl.BlockSpec((N_HEADS, Q_BLOCK, LANES), qrow),
            pl.BlockSpec((IDX_HEADS, Q_BLOCK, IDX_DIM), qrow),
            pl.BlockSpec((IDX_HEADS, Q_BLOCK), lambda b, i: (0, b * nblk + i)),
            pl.BlockSpec((seq, LANES), lambda b, i: (b, 0)),
            pl.BlockSpec((nblk, LANES, Q_BLOCK), lambda b, i: (b, 0, 0)),
            pl.BlockSpec((seq, IDX_DIM), lambda b, i: (b, 0)),
            pl.BlockSpec((3, N_HEADS, Q_BLOCK, Q_BLOCK), lambda b, i: (0, 0, 0, 0)),
        ],
        out_specs=pl.BlockSpec((Q_BLOCK, ATTN_W), lambda b, i: (b * nblk + i, 0)),
        out_shape=jax.ShapeDtypeStruct((t, ATTN_W), BF16),
        scratch_shapes=[
            pltpu.VMEM((seq, Q_BLOCK), jnp.int32),
            pltpu.VMEM((seq, Q_BLOCK), jnp.int16),
            pltpu.VMEM((seq, Q_BLOCK), jnp.int16),
            pltpu.VMEM((seq, Q_BLOCK), F32),
            pltpu.VMEM((SUBLANES, Q_BLOCK), jnp.int32),
            pltpu.VMEM((seq, N_HEADS * Q_BLOCK), F32),
            pltpu.VMEM((seq, N_HEADS * Q_BLOCK), BF16),
            pltpu.VMEM((N_KV, HEAD_DIM + ONES_ROWS, KV_REP * Q_BLOCK), F32),
        ],
        compiler_params=_cparams("parallel", "arbitrary"),
        name="attn",
    )(q_hm, qi_hm, wit, k2, vt, ki2, bias_tiles)


def _rel_bucket(rel):
    nb = REL_BUCKETS // 2
    max_exact = nb // 2
    ret = jnp.where(rel < 0, nb, 0)
    n = jnp.abs(rel)
    nf = jnp.maximum(n, 1).astype(jnp.float32)
    large = max_exact + (jnp.log(nf / max_exact) / math.log(REL_MAX_DIST / max_exact)
                         * (nb - max_exact)).astype(jnp.int32)
    large = jnp.minimum(large, nb - 1)
    return ret + jnp.where(n < max_exact, n, large)


def _bias_tiles(rel_bias):
    s = jnp.arange(Q_BLOCK, dtype=jnp.int32)[:, None]
    tq = jnp.arange(Q_BLOCK, dtype=jnp.int32)[None, :]
    d = jnp.arange(3, dtype=jnp.int32)[:, None, None]
    bucket = _rel_bucket(d * Q_BLOCK + tq - s)
    onehot = bucket[:, None, :, :, None] == jnp.arange(REL_BUCKETS, dtype=jnp.int32)
    tiles = jnp.sum(jnp.where(onehot, rel_bias.T[None, :, None, None, :], 0.0), axis=-1)
    return ((tiles - tiles[2:3, :, :1, :1]) * LOG2E).astype(F32)


AUG_ROWS = 48


def _merge_kernel(x_ref, vc_ref, at_ref, gc_ref, ga_ref, wco_ref, wao_ref, wo_ref, g_ref, wr_ref,
                  br_ref, xo_ref, ht_ref, gone_ref):
    y_conv = _dot(vc_ref[...], wco_ref[...])
    y_attn = _dot(at_ref[...], wao_ref[...])
    merged = gc_ref[...].astype(F32) * y_conv + ga_ref[...].astype(F32) * y_attn
    x = x_ref[...] + _dot(merged.astype(BF16), wo_ref[...])
    xo_ref[...] = x
    h = x * lax.rsqrt(jnp.mean(x * x, axis=-1, keepdims=True) + EPS) * g_ref[...]
    hb = h.astype(BF16)
    d = h.shape[1]
    ht_ref[0:d, :] = hb.astype(F32).T.astype(BF16)

    logits = _dot(hb, wr_ref[...]) + br_ref[...]
    lane = lax.broadcasted_iota(jnp.int32, logits.shape, 1)
    neg_inf = -jnp.inf
    gl = jnp.where(lane < N_GROUPS, logits, neg_inf)
    gmax = jnp.max(gl, axis=-1, keepdims=True)
    grp = jnp.min(jnp.where(gl == gmax, lane, LANES), axis=-1, keepdims=True)
    p_grp = 1.0 / jnp.sum(jnp.exp(gl - gmax), axis=-1, keepdims=True)
    e_lo = ROUTER_E_OFF + grp * EXPERTS_PER_GROUP
    in_grp = jnp.logical_and(lane >= e_lo, lane < e_lo + EXPERTS_PER_GROUP)
    el = jnp.where(in_grp, logits, neg_inf)
    v1 = jnp.max(el, axis=-1, keepdims=True)
    i1 = jnp.min(jnp.where(jnp.logical_and(in_grp, el == v1), lane, LANES), axis=-1, keepdims=True)
    rest = jnp.logical_and(in_grp, lane != i1)
    el2 = jnp.where(rest, logits, neg_inf)
    v2 = jnp.max(el2, axis=-1, keepdims=True)
    i2 = jnp.min(jnp.where(jnp.logical_and(rest, el2 == v2), lane, LANES), axis=-1, keepdims=True)
    e2 = jnp.exp(v2 - v1)
    p1 = p_grp / (1.0 + e2)
    p2 = p_grp * e2 / (1.0 + e2)
    gone_ref[...] = jnp.where(lane == grp, 1.0, 0.0).astype(BF16)
    w8 = jnp.where(lane == i1 - e_lo, p1, jnp.where(lane == i2 - e_lo, p2, 0.0))
    w8t = w8.T[0:16, :]
    hi = w8t.astype(BF16)
    r1 = w8t - hi.astype(F32)
    mid = r1.astype(BF16)
    lo = (r1 - mid.astype(F32)).astype(BF16)
    ht_ref[d:d + 16, :] = hi
    ht_ref[d + 16:d + 32, :] = mid
    ht_ref[d + 32:d + 48, :] = lo


def _merge_call(x2, vc, at, gc, ga, wco, wao, wo, g, wr, br):
    t, d = x2.shape
    tm = MERGE_TM
    row = lambda i: (i, 0)
    const = lambda i: (0, 0)
    return pl.pallas_call(
        _merge_kernel,
        grid=(t // tm,),
        in_specs=[
            pl.BlockSpec((tm, d), row),
            pl.BlockSpec((tm, CONV_CH), row),
            pl.BlockSpec((tm, ATTN_W), row),
            pl.BlockSpec((tm, d), row),
            pl.BlockSpec((tm, d), row),
            pl.BlockSpec((CONV_CH, d), const),
            pl.BlockSpec((ATTN_W, d), const),
            pl.BlockSpec((d, d), const),
            pl.BlockSpec((1, d), const),
            pl.BlockSpec((d, LANES), const),
            pl.BlockSpec((1, LANES), const),
        ],
        out_specs=(pl.BlockSpec((tm, d), row), pl.BlockSpec((d + AUG_ROWS, tm), lambda i: (0, i)),
                   pl.BlockSpec((tm, LANES), row)),
        out_shape=(jax.ShapeDtypeStruct((t, d), F32),
                   jax.ShapeDtypeStruct((d + AUG_ROWS, t), BF16),
                   jax.ShapeDtypeStruct((t, LANES), BF16)),
        compiler_params=_cparams("parallel"),
        name="merge",
    )(x2, vc, at, gc, ga, wco, wao, wo, g, wr, br)


def _pack_router(w_gr, b_gr, w_er, b_er):
    d = w_gr.shape[0]
    w = jnp.zeros((d, LANES), F32)
    w = w.at[:, :N_GROUPS].set(w_gr)
    w = w.at[:, ROUTER_E_OFF:ROUTER_E_OFF + N_EXPERTS].set(
        jnp.moveaxis(w_er, 0, 1).reshape(d, N_EXPERTS))
    b = jnp.zeros((1, LANES), F32)
    b = b.at[0, :N_GROUPS].set(b_gr)
    b = b.at[0, ROUTER_E_OFF:ROUTER_E_OFF + N_EXPERTS].set(b_er.reshape(N_EXPERTS))
    return w.astype(BF16), b


SLOT_CHUNK = 256
FFN_SPLIT = 8
MOE_MAX_CHUNKS = MOE_TM // SLOT_CHUNK + N_GROUPS - 1
MOE_STEPS = MOE_MAX_CHUNKS + 1


def _plan_kernel(gone_ref, ltri_ref, utri_ref, slot_ref, cnt_ref):
    gone = gone_ref[...]
    before = _dot(ltri_ref[...], gone)
    last = gone.shape[0] - 1
    cnt = before[last:last + 1, :] + gone[last:last + 1, :].astype(F32)
    nchunks = jnp.floor((cnt + (SLOT_CHUNK - 1)) * (1.0 / SLOT_CHUNK))
    start = _dot(jnp.broadcast_to(nchunks, (SUBLANES, LANES)).astype(BF16), utri_ref[...])[0:1, :]
    slot = jnp.sum(gone.astype(F32) * (before + start * SLOT_CHUNK), axis=-1, keepdims=True)
    slot_ref[...] = jnp.broadcast_to(slot.astype(jnp.int32), slot_ref.shape)
    cnt_ref[0] = jnp.broadcast_to(cnt, (SUBLANES, LANES)).astype(jnp.int32)


def _plan_call(gone):
    t = gone.shape[0]
    tm = MOE_TM
    nt = t // tm
    r = lax.broadcasted_iota(jnp.int32, (tm, tm), 0)
    c = lax.broadcasted_iota(jnp.int32, (tm, tm), 1)
    ltri = (c < r).astype(BF16)
    utri = (lax.broadcasted_iota(jnp.int32, (LANES, LANES), 0)
            < lax.broadcasted_iota(jnp.int32, (LANES, LANES), 1)).astype(BF16)
    return pl.pallas_call(
        _plan_kernel,
        grid=(nt,),
        in_specs=[pl.BlockSpec((tm, LANES), lambda i: (i, 0)),
                  pl.BlockSpec((tm, tm), lambda i: (0, 0)),
                  pl.BlockSpec((LANES, LANES), lambda i: (0, 0))],
        out_specs=(pl.BlockSpec((tm, LANES), lambda i: (i, 0)),
                   pl.BlockSpec((1, SUBLANES, LANES), lambda i: (i, 0, 0))),
        out_shape=(jax.ShapeDtypeStruct((t, LANES), jnp.int32),
                   jax.ShapeDtypeStruct((nt, SUBLANES, LANES), jnp.int32)),
        compiler_params=_cparams("parallel"),
        name="moe_plan",
    )(gone, ltri, utri)


def _moe_kernel(cg_ref, nc_ref, ht_ref, slot_ref, wgu_ref, wd_ref, o_ref,
                pt_ref, acc_ref, xg_ref, cw_ref, yt_ref):
    i = pl.program_id(0)
    j = pl.program_id(1)
    nc = nc_ref[i]
    tm = slot_ref.shape[0]
    d = acc_ref.shape[1]

    def gather(jn, slot_idx):
        xa = _dot(ht_ref[...], pt_ref[jn])
        xg_ref[slot_idx] = xa[0:d, :].astype(BF16)
        cw_ref[slot_idx] = xa[d:d + 16, :] + xa[d + 16:d + 32, :] + xa[d + 32:d + 48, :]

    def scatter(jp, slot_idx):
        acc_ref[...] += _dot_nt(pt_ref[jp], yt_ref[slot_idx])

    @pl.when(j == 0)
    def _():
        acc_ref[...] = jnp.zeros(acc_ref.shape, F32)
        yt_ref[...] = jnp.zeros(yt_ref.shape, BF16)
        slot = slot_ref[...]
        lane = lax.broadcasted_iota(jnp.int32, (tm, LANES), 1)
        for jj in range(MOE_MAX_CHUNKS):
            for half in range(SLOT_CHUNK // LANES):
                off = jj * SLOT_CHUNK + half * LANES
                pt_ref[jj, :, half * LANES:(half + 1) * LANES] = jnp.where(
                    slot == lane + off, 1.0, 0.0).astype(BF16)
        gather(0, 0)

    @pl.when(j < nc)
    def _():
        cur = j % 2
        scatter(jnp.maximum(j - 1, 0), 1 - cur)
        xg = xg_ref[cur]
        cw = cw_ref[cur]
        es = EXPERTS_PER_GROUP // FFN_SPLIT
        sf = es * EXPERT_FF
        yt = None
        for s in range(FFN_SPLIT):
            ab = _dot(wgu_ref[0, 2 * sf * s:2 * sf * (s + 1), :], xg)
            mids = []
            for e in range(es):
                a = ab[e * EXPERT_FF:(e + 1) * EXPERT_FF, :]
                b = ab[sf + e * EXPERT_FF:sf + (e + 1) * EXPERT_FF, :]
                c = cw[s * es + e:s * es + e + 1, :]
                mids.append((a * _sigmoid(a) * b * c).astype(BF16))
            part = _dot(wd_ref[0, :, sf * s:sf * (s + 1)], jnp.concatenate(mids, axis=0))
            yt = part if yt is None else yt + part
        yt_ref[cur] = yt.astype(BF16)
        gather(jnp.minimum(j + 1, MOE_MAX_CHUNKS - 1), 1 - cur)

    @pl.when(j == nc)
    def _():
        scatter(nc - 1, (nc - 1) % 2)

    @pl.when(j == pl.num_programs(1) - 1)
    def _():
        o_ref[...] = acc_ref[...].astype(BF16)


def _moe_call(ht_aug, slot, chunk_group, n_chunks, wgu_t, wd_t):
    da, t = ht_aug.shape
    d = da - AUG_ROWS
    tm = MOE_TM
    nt = t // tm
    ff = EXPERTS_PER_GROUP * EXPERT_FF
    wmap = lambda i, j, cg, nc: (cg[i * MOE_STEPS + j], 0, 0)
    grid_spec = pltpu.PrefetchScalarGridSpec(
        num_scalar_prefetch=2,
        grid=(nt, MOE_STEPS),
        in_specs=[
            pl.BlockSpec((da, tm), lambda i, j, cg, nc: (0, i)),
            pl.BlockSpec((tm, LANES), lambda i, j, cg, nc: (i, 0)),
            pl.BlockSpec((1, 2 * ff, d), wmap),
            pl.BlockSpec((1, d, ff), wmap),
        ],
        out_specs=pl.BlockSpec((tm, d), lambda i, j, cg, nc: (i, 0)),
        scratch_shapes=[pltpu.VMEM((MOE_MAX_CHUNKS, tm, SLOT_CHUNK), BF16),
                        pltpu.VMEM((tm, d), F32),
                        pltpu.VMEM((2, d, SLOT_CHUNK), BF16),
                        pltpu.VMEM((2, 16, SLOT_CHUNK), F32),
                        pltpu.VMEM((2, d, SLOT_CHUNK), BF16)],
    )
    return pl.pallas_call(
        _moe_kernel,
        grid_spec=grid_spec,
        out_shape=jax.ShapeDtypeStruct((t, d), BF16),
        compiler_params=_cparams("parallel", "arbitrary"),
        name="moe",
    )(chunk_group, n_chunks, ht_aug, slot, wgu_t, wd_t)


def _chunk_tables(cnt):
    per_group = (cnt[:, 0, :N_GROUPS] + SLOT_CHUNK - 1) // SLOT_CHUNK
    ends = jnp.cumsum(per_group, axis=1)
    n_chunks = ends[:, -1]
    j = jnp.arange(MOE_STEPS, dtype=jnp.int32)[None, :]
    jj = jnp.minimum(j, n_chunks[:, None] - 1)
    group = jnp.sum((jj[:, :, None] >= ends[:, None, :]).astype(jnp.int32), axis=-1)
    return group.reshape(-1).astype(jnp.int32), n_chunks.astype(jnp.int32)


def _pack_expert_weights(w_gate, w_up, w_down):
    ne, d, f = w_gate.shape
    tr = lambda w: jnp.swapaxes(w, 1, 2).reshape(N_GROUPS, FFN_SPLIT, -1, d)
    wgu = jnp.concatenate([tr(w_gate), tr(w_up)], axis=2).reshape(
        N_GROUPS, 2 * EXPERTS_PER_GROUP * f, d).astype(BF16)
    wd = jnp.swapaxes(w_down.reshape(N_GROUPS, EXPERTS_PER_GROUP * f, d), 1, 2).astype(BF16)
    return wgu, wd


def _norm_kernel(x_ref, y_ref, g_ref, o_ref):
    x = x_ref[...] + y_ref[...].astype(F32)
    o_ref[...] = x * lax.rsqrt(jnp.mean(x * x, axis=-1, keepdims=True) + EPS) * g_ref[...]


def _norm_call(x2, y2, g):
    t, d = x2.shape
    tm = NORM_TM
    row = lambda i: (i, 0)
    return pl.pallas_call(
        _norm_kernel,
        grid=(t // tm,),
        in_specs=[pl.BlockSpec((tm, d), row), pl.BlockSpec((tm, d), row),
                  pl.BlockSpec((1, d), lambda i: (0, 0))],
        out_specs=pl.BlockSpec((tm, d), row),
        out_shape=jax.ShapeDtypeStruct((t, d), F32),
        compiler_params=_cparams("parallel"),
        name="final_norm",
    )(x2, y2, g)


def kernel(x, g_mix, w_in, conv_w, conv_b, conv_ln_g, conv_ln_b, w_conv_out, w_attn_out, w_out, rel_bias, g_ffn, w_group_router, b_group_router, w_expert_router, b_expert_router, w_e_gate, w_e_up, w_e_down, g_final):
    batch, seq, d = x.shape
    depth = g_mix.shape[0]
    t = batch * seq
    assert seq % CONV_TL == 0 and seq % KEY_CHUNK == 0 and t % MOE_TM == 0
    x2 = x.reshape(t, d)
    y2 = None
    bias_tiles = _bias_tiles(rel_bias)
    for l in range(depth):
        w_pack = _pack_proj_weights(w_in[l])
        outs = _proj_call(x2, y2, g_mix[l][None, :], w_pack)
        if y2 is not None:
            x2, outs = outs[0], outs[1:]
        u, q_hm, k2, vt, qi_hm, ki2, wit, gc, ga = outs
        vc = _conv_call(u.reshape(batch, seq, 2 * CONV_CH), conv_w[l], conv_b[l][None, :],
                        conv_ln_g[l][None, :], conv_ln_b[l][None, :]).reshape(t, CONV_CH)
        at = _attn_call(q_hm, qi_hm, wit, k2, vt, ki2, bias_tiles, batch, seq)
        wr, br = _pack_router(w_group_router[l], b_group_router[l], w_expert_router[l],
                              b_expert_router[l])
        x2, ht_aug, gone = _merge_call(x2, vc, at, gc, ga, w_conv_out[l].astype(BF16),
                                       w_attn_out[l].astype(BF16), w_out[l].astype(BF16),
                                       g_ffn[l][None, :], wr, br)
        slot, cnt = _plan_call(gone)
        chunk_group, n_chunks = _chunk_tables(cnt)
        wgu_t, wd_t = _pack_expert_weights(w_e_gate[l], w_e_up[l], w_e_down[l])
        y2 = _moe_call(ht_aug, slot, chunk_group, n_chunks, wgu_t, wd_t)
    return _norm_call(x2, y2, g_final[None, :]).reshape(batch, seq, d)
```

```python
import functools
import math

import jax
import jax.numpy as jnp
import numpy as np
from jax import lax
from jax.experimental import pallas as pl
from jax.experimental.pallas import tpu as pltpu

CHUNK = 64
CONV_CH = 512
CONV_WIDTH = 31
N_HEADS = 8
HEAD_DIM = 64
N_KV = 2
KV_REP = N_HEADS // N_KV
ATTN_W = N_HEADS * HEAD_DIM
IDX_HEADS = 8
IDX_DIM = 32
TOPK_MAX = 256
Q_BLOCK = 128
REL_BUCKETS = 32
REL_MAX_DIST = 128
N_GROUPS = 4
EXPERTS_PER_GROUP = 8
N_EXPERTS = N_GROUPS * EXPERTS_PER_GROUP
EXPERT_FF = 256
EPS = 1e-6

LANES = 128
SUBLANES = 8
VMEM_LIMIT_BYTES = 56 * 1024 * 1024

LOG2E = math.log2(math.e)
NEG_BIG = -1e30
INT_MIN = -(2 ** 31)
KEY_MIN_FINITE = -0x7F800000
KEY_MAX_FINITE = 0x7F7FFFFF

BF16 = jnp.bfloat16
F32 = jnp.float32

PROJ_TM = 512
CONV_TL = 256
CONV_HALO = 32
CONV_ROWS = 128
MERGE_TM = 512
MOE_TM = 1024
NORM_TM = 1024

ROUTER_E_OFF = 32


def _cparams(*sem):
    return pltpu.CompilerParams(dimension_semantics=sem, vmem_limit_bytes=VMEM_LIMIT_BYTES)


def _sigmoid(x):
    return 1.0 / (1.0 + jnp.exp(-x))


def _dot(a, b):
    return jnp.dot(a, b, preferred_element_type=F32)


def _dot_nt(a, b):
    return lax.dot_general(a, b, (((1,), (1,)), ((), ())), preferred_element_type=F32)


SEG_U = (0, 2 * CONV_CH)
SEG_Q = (SEG_U[1], SEG_U[1] + N_HEADS * LANES)
SEG_K = (SEG_Q[1], SEG_Q[1] + LANES)
SEG_V = (SEG_K[1], SEG_K[1] + LANES)
SEG_QI = (SEG_V[1], SEG_V[1] + IDX_HEADS * IDX_DIM)
SEG_KI = (SEG_QI[1], SEG_QI[1] + LANES)
SEG_WI = (SEG_KI[1], SEG_KI[1] + LANES)
SEG_GC = (SEG_WI[1], SEG_WI[1] + 1024)
SEG_GA = (SEG_GC[1], SEG_GC[1] + 1024)
PROJ_COLS = SEG_GA[1]


def _proj_kernel(*refs, has_y):
    if has_y:
        x_ref, y_ref, g_ref, w_ref, xo_ref = refs[:5]
        x = x_ref[...] + y_ref[...].astype(F32)
        xo_ref[...] = x
    else:
        x_ref, g_ref, w_ref = refs[:3]
        x = x_ref[...]
    u_ref, q_ref, k_ref, vt_ref, qi_ref, ki_ref, wit_ref, gc_ref, ga_ref = refs[-9:]
    h = x * lax.rsqrt(jnp.mean(x * x, axis=-1, keepdims=True) + EPS) * g_ref[...]
    hb = h.astype(BF16)

    def seg(s):
        return _dot(hb, w_ref[:, s[0]:s[1]])

    u_ref[...] = seg(SEG_U).astype(BF16)
    pq = seg(SEG_Q)
    for hh in range(N_HEADS):
        q_ref[hh] = pq[:, hh * LANES:(hh + 1) * LANES].astype(BF16)
    k_ref[...] = seg(SEG_K).astype(BF16)
    pv = seg(SEG_V)
    tm = pv.shape[0]
    for c in range(tm // LANES):
        vt_ref[c] = pv[c * LANES:(c + 1) * LANES, :].T.astype(BF16)
    pqi = seg(SEG_QI)
    for hh in range(IDX_HEADS):
        qi_ref[hh] = pqi[:, hh * IDX_DIM:(hh + 1) * IDX_DIM].astype(BF16)
    ki_ref[...] = seg(SEG_KI)[:, :IDX_DIM].astype(BF16)
    pwi = seg(SEG_WI)
    for c in range(tm // LANES):
        wit_ref[:, c * LANES:(c + 1) * LANES] = pwi[c * LANES:(c + 1) * LANES, :].T[:IDX_HEADS, :]
    gc_ref[...] = _sigmoid(seg(SEG_GC)).astype(BF16)
    ga_ref[...] = _sigmoid(seg(SEG_GA)).astype(BF16)


def _proj_call(x2, y2, g, w_pack):
    t, d = x2.shape
    tm = PROJ_TM
    nq = t // Q_BLOCK
    row = lambda i: (i, 0)
    has_y = y2 is not None
    out_shape = (jax.ShapeDtypeStruct((t, d), F32),) if has_y else ()
    out_specs = (pl.BlockSpec((tm, d), row),) if has_y else ()
    out_shape += (
        jax.ShapeDtypeStruct((t, 2 * CONV_CH), BF16),
        jax.ShapeDtypeStruct((N_HEADS, t, LANES), BF16),
        jax.ShapeDtypeStruct((t, LANES), BF16),
        jax.ShapeDtypeStruct((nq, LANES, Q_BLOCK), BF16),
        jax.ShapeDtypeStruct((IDX_HEADS, t, IDX_DIM), BF16),
        jax.ShapeDtypeStruct((t, IDX_DIM), BF16),
        jax.ShapeDtypeStruct((IDX_HEADS, t), F32),
        jax.ShapeDtypeStruct((t, d), BF16),
        jax.ShapeDtypeStruct((t, d), BF16),
    )
    out_specs += (
        pl.BlockSpec((tm, 2 * CONV_CH), row),
        pl.BlockSpec((N_HEADS, tm, LANES), lambda i: (0, i, 0)),
        pl.BlockSpec((tm, LANES), row),
        pl.BlockSpec((tm // Q_BLOCK, LANES, Q_BLOCK), lambda i: (i, 0, 0)),
        pl.BlockSpec((IDX_HEADS, tm, IDX_DIM), lambda i: (0, i, 0)),
        pl.BlockSpec((tm, IDX_DIM), row),
        pl.BlockSpec((IDX_HEADS, tm), lambda i: (0, i)),
        pl.BlockSpec((tm, d), row),
        pl.BlockSpec((tm, d), row),
    )
    streams = (x2, y2) if has_y else (x2,)
    return pl.pallas_call(
        functools.partial(_proj_kernel, has_y=has_y),
        grid=(t // tm,),
        in_specs=[pl.BlockSpec((tm, d), row) for _ in streams] + [
            pl.BlockSpec((1, d), lambda i: (0, 0)),
            pl.BlockSpec((d, PROJ_COLS), lambda i: (0, 0)),
        ],
        out_specs=out_specs,
        out_shape=out_shape,
        compiler_params=_cparams("parallel"),
        name="proj",
    )(*streams, g, w_pack)


def _pack_proj_weights(w_in):
    d = w_in.shape[0]
    splits = (2 * CONV_CH, ATTN_W, N_KV * HEAD_DIM, N_KV * HEAD_DIM, IDX_HEADS * IDX_DIM,
              IDX_DIM, IDX_HEADS, d, d)
    offs = np.concatenate([[0], np.cumsum(splits)])
    w_u, w_q, w_k, w_v, w_qi, w_ki, w_wi, w_gc, w_ga = [
        w_in[:, int(offs[i]):int(offs[i + 1])] for i in range(len(splits))]
    w_q = (w_q * (HEAD_DIM ** -0.5 * LOG2E)).reshape(d, N_KV, KV_REP, HEAD_DIM)
    q_pad = jnp.zeros((d, N_KV, KV_REP, N_KV, HEAD_DIM), w_in.dtype)
    for gidx in range(N_KV):
        q_pad = q_pad.at[:, gidx, :, gidx, :].set(w_q[:, gidx])
    q_pad = q_pad.reshape(d, N_HEADS * LANES)
    pad = lambda w: jnp.pad(w, ((0, 0), (0, LANES - w.shape[1])))
    packed = jnp.concatenate([w_u, q_pad, w_k, w_v, w_qi, pad(w_ki), pad(w_wi), w_gc, w_ga], axis=1)
    assert packed.shape[1] == PROJ_COLS
    return packed.astype(BF16)


def _conv_kernel(u_ref, halo_ref, cw_ref, cb_ref, lg_ref, lb_ref, o_ref, ext_ref, y_ref):
    i = pl.program_id(1)
    tl = u_ref.shape[1]

    def glu(u):
        u = u.astype(F32)
        return u[:, :CONV_CH] * _sigmoid(u[:, CONV_CH:])

    halo = glu(halo_ref[0])
    ext_ref[0:CONV_HALO, :] = jnp.where(i > 0, halo, 0.0)
    ext_ref[CONV_HALO:, :] = glu(u_ref[0])
    first = CONV_HALO - (CONV_WIDTH - 1)
    for c in range(CONV_CH // LANES):
        cs = slice(c * LANES, (c + 1) * LANES)
        for r0 in range(0, tl, CONV_ROWS):
            out = cb_ref[:, cs]
            for b in range(SUBLANES):
                win = CONV_ROWS + (SUBLANES if b else 0)
                part = jnp.zeros((win, LANES), F32)
                for j in range(CONV_WIDTH):
                    if (first + j) % SUBLANES == b:
                        lo = first + j - b + r0
                        part = part + cw_ref[j:j + 1, cs] * ext_ref[lo:lo + win, cs]
                out = out + part[b:b + CONV_ROWS]
            y_ref[r0:r0 + CONV_ROWS, cs] = out
    y = y_ref[...]
    mu = jnp.mean(y, axis=-1, keepdims=True)
    yc = y - mu
    var = jnp.mean(yc * yc, axis=-1, keepdims=True)
    yn = yc * lax.rsqrt(var + EPS) * lg_ref[...] + lb_ref[...]
    o_ref[0] = (yn * _sigmoid(yn)).astype(BF16)


def _conv_call(u3, conv_w, conv_b, ln_g, ln_b):
    b, l, _ = u3.shape
    tl = CONV_TL
    halo_blocks = tl // CONV_HALO
    const = lambda bi, i: (0, 0)
    return pl.pallas_call(
        _conv_kernel,
        grid=(b, l // tl),
        in_specs=[
            pl.BlockSpec((1, tl, 2 * CONV_CH), lambda bi, i: (bi, i, 0)),
            pl.BlockSpec((1, CONV_HALO, 2 * CONV_CH),
                         lambda bi, i: (bi, jnp.maximum(i * halo_blocks - 1, 0), 0)),
            pl.BlockSpec((CONV_WIDTH, CONV_CH), const),
            pl.BlockSpec((1, CONV_CH), const),
            pl.BlockSpec((1, CONV_CH), const),
            pl.BlockSpec((1, CONV_CH), const),
        ],
        out_specs=pl.BlockSpec((1, tl, CONV_CH), lambda bi, i: (bi, i, 0)),
        out_shape=jax.ShapeDtypeStruct((b, l, CONV_CH), BF16),
        scratch_shapes=[pltpu.VMEM((CONV_HALO + tl, CONV_CH), F32), pltpu.VMEM((tl, CONV_CH), F32)],
        compiler_params=_cparams("parallel", "parallel"),
        name="conv",
    )(u3, u3, conv_w, conv_b, ln_g, ln_b)


def _sortable_key(score):
    b = lax.bitcast_convert_type(score, jnp.int32)
    return b ^ ((b >> 31) & 0x7FFFFFFF)


KEY_CHUNK = 2 * Q_BLOCK
HALF16 = 1 << 15
ONES_ROWS = 16


def _tree_reduce(x, rows, op):
    parts = [x[j * rows:(j + 1) * rows] for j in range(x.shape[0] // rows)]
    while len(parts) > 1:
        parts = [op(a, b) for a, b in zip(parts[0::2], parts[1::2])]
    return parts[0]


def _attn_kernel(*refs, topk, max_chunks):
    nch = (pl.program_id(1) + 2) // 2
    lax.switch(nch - 1, [functools.partial(_attn_body, *refs, topk=topk, nch=n + 1)
                         for n in range(max_chunks)])


def _static_loop(lo, hi, body, carry):
    for c in range(lo, hi):
        carry = body(c, carry)
    return carry


def _attn_body(q_ref, qi_ref, wit_ref, k_ref, vt_ref, ki_ref, bias_ref, o_ref,
               key_ref, khi_ref, klo_ref, madd_ref, thr_ref, x_ref, p_ref, acc_ref, *, topk, nch):
    i = pl.program_id(1)
    qb = Q_BLOCK
    ch = KEY_CHUNK
    nkb = i + 1
    q_all = q_ref[...].reshape(N_HEADS * qb, LANES)
    qi_st = qi_ref[...].reshape(IDX_HEADS * qb, IDX_DIM)
    wit = wit_ref[...]
    t_loc = lax.broadcasted_iota(jnp.int32, (1, qb), 1)
    limit = ((i * qb + t_loc) // CHUNK + 1) * CHUNK
    idx_scale = (IDX_DIM ** -0.5) * (IDX_HEADS ** -0.5)

    def chunk_rows(c):
        return slice(c * ch, (c + 1) * ch)

    def score_body(c, carry):
        rows = chunk_rows(c)
        dots = _dot_nt(ki_ref[rows, :], qi_st)
        sc = jnp.zeros((ch, qb), F32)
        for hh in range(IDX_HEADS):
            sc = sc + jnp.maximum(dots[:, hh * qb:(hh + 1) * qb], 0.0) * wit[hh:hh + 1, :]
        key = _sortable_key(sc * idx_scale)
        s_glob = c * ch + lax.broadcasted_iota(jnp.int32, (ch, qb), 0)
        key = jnp.where(s_glob < limit, key, INT_MIN)
        key_ref[rows, :] = key
        khi_ref[rows, :] = (key >> 16).astype(jnp.int16)
        klo_ref[rows, :] = ((key & 0xFFFF) - HALF16).astype(jnp.int16)
        return carry

    _static_loop(0, nch, score_body, 0)

    def count32(pred):
        def body(c, acc):
            ind = jnp.where(pred(key_ref[chunk_rows(c), :]), 1, 0)
            return acc + _tree_reduce(ind, acc.shape[0], jnp.add)
        acc = _static_loop(0, nch, body, jnp.zeros((4 * SUBLANES, qb), jnp.int32))
        return jnp.sum(acc, axis=0, keepdims=True)

    def search(n):
        def count16(ref, pred):
            parts = []
            for c in range(n):
                ind = jnp.where(pred(ref[c * ch:(c + 1) * ch, :]), jnp.int16(1), jnp.int16(0))
                parts.append(_tree_reduce(ind, 4 * SUBLANES, jnp.add))
            while len(parts) > 1:
                pairs = [a + b for a, b in zip(parts[0::2], parts[1::2])]
                parts = pairs + parts[len(pairs) * 2:]
            return jnp.sum(parts[0].astype(jnp.int32), axis=0, keepdims=True)

        def kth_largest16(ref, want):
            def bit_body(it, lo):
                cand = lo + jnp.left_shift(jnp.int32(1), 15 - it)
                c16 = cand.astype(jnp.int16)
                return jnp.where(count16(ref, lambda v: v >= c16) >= want, cand, lo)
            return lax.fori_loop(0, 16, bit_body, jnp.full((1, qb), -HALF16, jnp.int32))

        t_hi = kth_largest16(khi_ref, topk)
        t_hi16 = t_hi.astype(jnp.int16)
        n_above = count16(khi_ref, lambda v: v > t_hi16)
        for c in range(n):
            rows = slice(c * ch, (c + 1) * ch)
            klo_ref[rows, :] = jnp.where(khi_ref[rows, :] == t_hi16, klo_ref[rows, :],
                                         jnp.int16(-HALF16))
        t_lo = kth_largest16(klo_ref, topk - n_above)
        thr_ref[0:1, :] = jnp.maximum(jnp.left_shift(t_hi, 16) | (t_lo + HALF16), KEY_MIN_FINITE)

    thr_ref[...] = jnp.full(thr_ref.shape, KEY_MIN_FINITE, jnp.int32)

    @pl.when(nkb * qb > topk)
    def _():
        search(nch)

    thr = thr_ref[0:1, :]
    n_ge = count32(lambda key: key >= thr)
    has_ties = jnp.max(n_ge) > topk

    @pl.when(jnp.logical_not(has_ties))
    def _():
        def body(c, carry):
            rows = chunk_rows(c)
            key = key_ref[rows, :]
            sel = jnp.logical_and(key >= thr, key <= KEY_MAX_FINITE)
            madd_ref[rows, :] = jnp.where(sel, 0.0, NEG_BIG)
            return carry
        _static_loop(0, nch, body, 0)

    @pl.when(has_ties)
    def _():
        n_gt = count32(lambda key: key > thr)
        need = (topk - n_gt).astype(F32)
        lower = (lax.broadcasted_iota(jnp.int32, (ch, ch), 1)
                 < lax.broadcasted_iota(jnp.int32, (ch, ch), 0)).astype(BF16)

        def body(c, seen):
            rows = chunk_rows(c)
            key = key_ref[rows, :]
            eq = key == thr
            eqf = jnp.where(eq, 1.0, 0.0)
            before = _dot(lower, eqf.astype(BF16)) + seen
            sel = jnp.logical_or(key > thr, jnp.logical_and(eq, before < need))
            sel = jnp.logical_and(sel, key <= KEY_MAX_FINITE)
            madd_ref[rows, :] = jnp.where(sel, 0.0, NEG_BIG)
            return seen + jnp.sum(eqf, axis=0, keepdims=True)
        _static_loop(0, nch, body, jnp.zeros((1, qb), F32))

    def mask_rows(rows, tile, mparts):
        madd = madd_ref[rows, :]
        lg = _dot_nt(k_ref[rows, :], q_all)
        out = []
        for hh in range(N_HEADS):
            x = lg[:, hh * qb:(hh + 1) * qb] + madd
            if tile is not None:
                x = x + bias_ref[tile, hh]
            x_ref[hh, rows, :] = x
            out.append(jnp.maximum(mparts[hh], _tree_reduce(x, SUBLANES, jnp.maximum)))
        return tuple(out)

    def far_body(c, mparts):
        return mask_rows(chunk_rows(c), None, mparts)

    def near_body(c, mparts):
        for half in range(ch // qb):
            jb = c * (ch // qb) + half
            d = i - jb
            tile = jnp.where(d == 0, 0, jnp.where(d == 1, 1, 2))
            mparts = mask_rows(slice(jb * qb, (jb + 1) * qb), tile, mparts)
        return mparts

    n_far = max(nch - 2, 0)
    mparts = tuple(jnp.full((SUBLANES, qb), NEG_BIG, F32) for _ in range(N_HEADS))
    mparts = _static_loop(0, n_far, far_body, mparts)
    mparts = _static_loop(n_far, nch, near_body, mparts)
    m_rows = [jnp.max(mp, axis=0, keepdims=True) for mp in mparts]

    acc_ref[...] = jnp.zeros(acc_ref.shape, F32)
    ones = jnp.ones((ONES_ROWS, ch), BF16)

    def prob_chunk(c):
        rows = chunk_rows(c)
        for hh in range(N_HEADS):
            cols = slice(hh * qb, (hh + 1) * qb)
            p_ref[hh, rows, :] = jnp.exp2((x_ref[hh, rows, :] - m_rows[hh]).astype(BF16))

    def pv_chunk(c):
        rows = chunk_rows(c)
        vt2 = jnp.concatenate([vt_ref[2 * c], vt_ref[2 * c + 1]], axis=1)
        for g in range(N_KV):
            lhs = jnp.concatenate([vt2[g * HEAD_DIM:(g + 1) * HEAD_DIM, :], ones], axis=0)
            probs = jnp.concatenate([p_ref[g * KV_REP + r, rows, :] for r in range(KV_REP)], axis=1)
            acc_ref[g] += _dot(lhs, probs)

    prob_chunk(0)

    def pv_body(c, carry):
        pv_chunk(c - 1)
        prob_chunk(c)
        return carry

    _static_loop(1, nch, pv_body, 0)
    pv_chunk(nch - 1)

    for hp in range(N_HEADS // 2):
        parts = []
        for hh in (2 * hp, 2 * hp + 1):
            g, r = hh // KV_REP, hh % KV_REP
            cols = slice(r * qb, (r + 1) * qb)
            inv_l = 1.0 / acc_ref[g, HEAD_DIM:HEAD_DIM + 1, cols]
            parts.append(acc_ref[g, 0:HEAD_DIM, cols] * inv_l)
        st = jnp.concatenate(parts, axis=0)
        o_ref[:, hp * LANES:(hp + 1) * LANES] = st.T.astype(BF16)


def _attn_call(q_hm, qi_hm, wit, k2, vt, ki2, bias_tiles, batch, seq):
    t = k2.shape[0]
    nblk = seq // Q_BLOCK
    topk = min(TOPK_MAX, seq // 4)
    qrow = lambda b, i: (0, b * nblk + i, 0)
    return pl.pallas_call(
        functools.partial(_attn_kernel, topk=topk, max_chunks=seq // KEY_CHUNK),
        grid=(batch, nblk),
        in_specs=[
            pl.BlockSpec((N_HEADS, Q_BLOCK, LANES), qrow),
            pl.BlockSpec((IDX_HEADS, Q_BLOCK, IDX_DIM), qrow),
            pl.BlockSpec((IDX_HEADS, Q_BLOCK), lambda b, i: (0, b * nblk + i)),
            pl.BlockSpec((seq, LANES), lambda b, i: (b, 0)),
            pl.BlockSpec((nblk, LANES, Q_BLOCK), lambda b, i: (b, 0, 0)),
            pl.BlockSpec((seq, IDX_DIM), lambda b, i: (b, 0)),
            pl.BlockSpec((3, N_HEADS, Q_BLOCK, Q_BLOCK), lambda b, i: (0, 0, 0, 0)),
        ],
        out_specs=pl.BlockSpec((Q_BLOCK, ATTN_W), lambda b, i: (b * nblk + i, 0)),
        out_shape=jax.ShapeDtypeStruct((t, ATTN_W), BF16),
        scratch_shapes=[
            pltpu.VMEM((seq, Q_BLOCK), jnp.int32),
            pltpu.VMEM((seq, Q_BLOCK), jnp.int16),
            pltpu.VMEM((seq, Q_BLOCK), jnp.int16),
            pltpu.VMEM((seq, Q_BLOCK), F32),
            pltpu.VMEM((SUBLANES, Q_BLOCK), jnp.int32),
            pltpu.VMEM((N_HEADS, seq, Q_BLOCK), F32),
            pltpu.VMEM((N_HEADS, seq, Q_BLOCK), BF16),
            pltpu.VMEM((N_KV, HEAD_DIM + ONES_ROWS, KV_REP * Q_BLOCK), F32),
        ],
        compiler_params=_cparams("parallel", "arbitrary"),
        name="attn",
    )(q_hm, qi_hm, wit, k2, vt, ki2, bias_tiles)


def _rel_bucket(rel):
    nb = REL_BUCKETS // 2
    max_exact = nb // 2
    ret = jnp.where(rel < 0, nb, 0)
    n = jnp.abs(rel)
    nf = jnp.maximum(n, 1).astype(jnp.float32)
    large = max_exact + (jnp.log(nf / max_exact) / math.log(REL_MAX_DIST / max_exact)
                         * (nb - max_exact)).astype(jnp.int32)
    large = jnp.minimum(large, nb - 1)
    return ret + jnp.where(n < max_exact, n, large)


def _bias_tiles(rel_bias):
    s = jnp.arange(Q_BLOCK, dtype=jnp.int32)[:, None]
    tq = jnp.arange(Q_BLOCK, dtype=jnp.int32)[None, :]
    d = jnp.arange(3, dtype=jnp.int32)[:, None, None]
    bucket = _rel_bucket(d * Q_BLOCK + tq - s)
    onehot = bucket[:, None, :, :, None] == jnp.arange(REL_BUCKETS, dtype=jnp.int32)
    tiles = jnp.sum(jnp.where(onehot, rel_bias.T[None, :, None, None, :], 0.0), axis=-1)
    return ((tiles - tiles[2:3, :, :1, :1]) * LOG2E).astype(F32)


AUG_ROWS = 48


def _merge_kernel(x_ref, vc_ref, at_ref, gc_ref, ga_ref, wco_ref, wao_ref, wo_ref, g_ref, wr_ref,
                  br_ref, xo_ref, ht_ref, gone_ref):
    y_conv = _dot(vc_ref[...], wco_ref[...])
    y_attn = _dot(at_ref[...], wao_ref[...])
    merged = gc_ref[...].astype(F32) * y_conv + ga_ref[...].astype(F32) * y_attn
    x = x_ref[...] + _dot(merged.astype(BF16), wo_ref[...])
    xo_ref[...] = x
    h = x * lax.rsqrt(jnp.mean(x * x, axis=-1, keepdims=True) + EPS) * g_ref[...]
    hb = h.astype(BF16)
    d = h.shape[1]
    ht_ref[0:d, :] = hb.astype(F32).T.astype(BF16)

    logits = _dot(hb, wr_ref[...]) + br_ref[...]
    lane = lax.broadcasted_iota(jnp.int32, logits.shape, 1)
    neg_inf = -jnp.inf
    gl = jnp.where(lane < N_GROUPS, logits, neg_inf)
    gmax = jnp.max(gl, axis=-1, keepdims=True)
    grp = jnp.min(jnp.where(gl == gmax, lane, LANES), axis=-1, keepdims=True)
    p_grp = 1.0 / jnp.sum(jnp.exp(gl - gmax), axis=-1, keepdims=True)
    e_lo = ROUTER_E_OFF + grp * EXPERTS_PER_GROUP
    in_grp = jnp.logical_and(lane >= e_lo, lane < e_lo + EXPERTS_PER_GROUP)
    el = jnp.where(in_grp, logits, neg_inf)
    v1 = jnp.max(el, axis=-1, keepdims=True)
    i1 = jnp.min(jnp.where(jnp.logical_and(in_grp, el == v1), lane, LANES), axis=-1, keepdims=True)
    rest = jnp.logical_and(in_grp, lane != i1)
    el2 = jnp.where(rest, logits, neg_inf)
    v2 = jnp.max(el2, axis=-1, keepdims=True)
    i2 = jnp.min(jnp.where(jnp.logical_and(rest, el2 == v2), lane, LANES), axis=-1, keepdims=True)
    e2 = jnp.exp(v2 - v1)
    p1 = p_grp / (1.0 + e2)
    p2 = p_grp * e2 / (1.0 + e2)
    gone_ref[...] = jnp.where(lane == grp, 1.0, 0.0).astype(BF16)
    w8 = jnp.where(lane == i1 - e_lo, p1, jnp.where(lane == i2 - e_lo, p2, 0.0))
    w8t = w8.T[0:16, :]
    hi = w8t.astype(BF16)
    r1 = w8t - hi.astype(F32)
    mid = r1.astype(BF16)
    lo = (r1 - mid.astype(F32)).astype(BF16)
    ht_ref[d:d + 16, :] = hi
    ht_ref[d + 16:d + 32, :] = mid
    ht_ref[d + 32:d + 48, :] = lo


def _merge_call(x2, vc, at, gc, ga, wco, wao, wo, g, wr, br):
    t, d = x2.shape
    tm = MERGE_TM
    row = lambda i: (i, 0)
    const = lambda i: (0, 0)
    return pl.pallas_call(
        _merge_kernel,
        grid=(t // tm,),
        in_specs=[
            pl.BlockSpec((tm, d), row),
            pl.BlockSpec((tm, CONV_CH), row),
            pl.BlockSpec((tm, ATTN_W), row),
            pl.BlockSpec((tm, d), row),
            pl.BlockSpec((tm, d), row),
            pl.BlockSpec((CONV_CH, d), const),
            pl.BlockSpec((ATTN_W, d), const),
            pl.BlockSpec((d, d), const),
            pl.BlockSpec((1, d), const),
            pl.BlockSpec((d, LANES), const),
            pl.BlockSpec((1, LANES), const),
        ],
        out_specs=(pl.BlockSpec((tm, d), row), pl.BlockSpec((d + AUG_ROWS, tm), lambda i: (0, i)),
                   pl.BlockSpec((tm, LANES), row)),
        out_shape=(jax.ShapeDtypeStruct((t, d), F32),
                   jax.ShapeDtypeStruct((d + AUG_ROWS, t), BF16),
                   jax.ShapeDtypeStruct((t, LANES), BF16)),
        compiler_params=_cparams("parallel"),
        name="merge",
    )(x2, vc, at, gc, ga, wco, wao, wo, g, wr, br)


def _pack_router(w_gr, b_gr, w_er, b_er):
    d = w_gr.shape[0]
    w = jnp.zeros((d, LANES), F32)
    w = w.at[:, :N_GROUPS].set(w_gr)
    w = w.at[:, ROUTER_E_OFF:ROUTER_E_OFF + N_EXPERTS].set(
        jnp.moveaxis(w_er, 0, 1).reshape(d, N_EXPERTS))
    b = jnp.zeros((1, LANES), F32)
    b = b.at[0, :N_GROUPS].set(b_gr)
    b = b.at[0, ROUTER_E_OFF:ROUTER_E_OFF + N_EXPERTS].set(b_er.reshape(N_EXPERTS))
    return w.astype(BF16), b


SLOT_CHUNK = 256
FFN_SPLIT = 8
MOE_MAX_CHUNKS = MOE_TM // SLOT_CHUNK + N_GROUPS - 1
MOE_STEPS = MOE_MAX_CHUNKS + 1


def _plan_kernel(gone_ref, ltri_ref, utri_ref, slot_ref, cnt_ref):
    gone = gone_ref[...]
    before = _dot(ltri_ref[...], gone)
    last = gone.shape[0] - 1
    cnt = before[last:last + 1, :] + gone[last:last + 1, :].astype(F32)
    nchunks = jnp.floor((cnt + (SLOT_CHUNK - 1)) * (1.0 / SLOT_CHUNK))
    start = _dot(jnp.broadcast_to(nchunks, (SUBLANES, LANES)).astype(BF16), utri_ref[...])[0:1, :]
    slot = jnp.sum(gone.astype(F32) * (before + start * SLOT_CHUNK), axis=-1, keepdims=True)
    slot_ref[...] = jnp.broadcast_to(slot.astype(jnp.int32), slot_ref.shape)
    cnt_ref[0] = jnp.broadcast_to(cnt, (SUBLANES, LANES)).astype(jnp.int32)


def _plan_call(gone):
    t = gone.shape[0]
    tm = MOE_TM
    nt = t // tm
    r = lax.broadcasted_iota(jnp.int32, (tm, tm), 0)
    c = lax.broadcasted_iota(jnp.int32, (tm, tm), 1)
    ltri = (c < r).astype(BF16)
    utri = (lax.broadcasted_iota(jnp.int32, (LANES, LANES), 0)
            < lax.broadcasted_iota(jnp.int32, (LANES, LANES), 1)).astype(BF16)
    return pl.pallas_call(
        _plan_kernel,
        grid=(nt,),
        in_specs=[pl.BlockSpec((tm, LANES), lambda i: (i, 0)),
                  pl.BlockSpec((tm, tm), lambda i: (0, 0)),
                  pl.BlockSpec((LANES, LANES), lambda i: (0, 0))],
        out_specs=(pl.BlockSpec((tm, LANES), lambda i: (i, 0)),
                   pl.BlockSpec((1, SUBLANES, LANES), lambda i: (i, 0, 0))),
        out_shape=(jax.ShapeDtypeStruct((t, LANES), jnp.int32),
                   jax.ShapeDtypeStruct((nt, SUBLANES, LANES), jnp.int32)),
        compiler_params=_cparams("parallel"),
        name="moe_plan",
    )(gone, ltri, utri)


def _moe_kernel(cg_ref, nc_ref, ht_ref, slot_ref, wgu_ref, wd_ref, o_ref,
                pt_ref, acc_ref, xg_ref, cw_ref, yt_ref):
    i = pl.program_id(0)
    j = pl.program_id(1)
    nc = nc_ref[i]
    tm = slot_ref.shape[0]
    d = acc_ref.shape[1]

    def gather(jn, slot_idx):
        xa = _dot(ht_ref[...], pt_ref[jn])
        xg_ref[slot_idx] = xa[0:d, :].astype(BF16)
        cw_ref[slot_idx] = xa[d:d + 16, :] + xa[d + 16:d + 32, :] + xa[d + 32:d + 48, :]

    def scatter(jp, slot_idx):
        acc_ref[...] += _dot_nt(pt_ref[jp], yt_ref[slot_idx])

    @pl.when(j == 0)
    def _():
        acc_ref[...] = jnp.zeros(acc_ref.shape, F32)
        yt_ref[...] = jnp.zeros(yt_ref.shape, BF16)
        slot = slot_ref[...]
        lane = lax.broadcasted_iota(jnp.int32, (tm, LANES), 1)
        for jj in range(MOE_MAX_CHUNKS):
            for half in range(SLOT_CHUNK // LANES):
                off = jj * SLOT_CHUNK + half * LANES
                pt_ref[jj, :, half * LANES:(half + 1) * LANES] = jnp.where(
                    slot == lane + off, 1.0, 0.0).astype(BF16)
        gather(0, 0)

    @pl.when(j < nc)
    def _():
        cur = j % 2
        scatter(jnp.maximum(j - 1, 0), 1 - cur)
        xg = xg_ref[cur]
        cw = cw_ref[cur]
        es = EXPERTS_PER_GROUP // FFN_SPLIT
        sf = es * EXPERT_FF
        yt = None
        for s in range(FFN_SPLIT):
            ab = _dot(wgu_ref[0, 2 * sf * s:2 * sf * (s + 1), :], xg)
            mids = []
            for e in range(es):
                a = ab[e * EXPERT_FF:(e + 1) * EXPERT_FF, :]
                b = ab[sf + e * EXPERT_FF:sf + (e + 1) * EXPERT_FF, :]
                c = cw[s * es + e:s * es + e + 1, :]
                mids.append((a * _sigmoid(a) * b * c).astype(BF16))
            part = _dot(wd_ref[0, :, sf * s:sf * (s + 1)], jnp.concatenate(mids, axis=0))
            yt = part if yt is None else yt + part
        yt_ref[cur] = yt.astype(BF16)
        gather(jnp.minimum(j + 1, MOE_MAX_CHUNKS - 1), 1 - cur)

    @pl.when(j == nc)
    def _():
        scatter(nc - 1, (nc - 1) % 2)

    @pl.when(j == pl.num_programs(1) - 1)
    def _():
        o_ref[...] = acc_ref[...].astype(BF16)


def _moe_call(ht_aug, slot, chunk_group, n_chunks, wgu_t, wd_t):
    da, t = ht_aug.shape
    d = da - AUG_ROWS
    tm = MOE_TM
    nt = t // tm
    ff = EXPERTS_PER_GROUP * EXPERT_FF
    wmap = lambda i, j, cg, nc: (cg[i * MOE_STEPS + j], 0, 0)
    grid_spec = pltpu.PrefetchScalarGridSpec(
        num_scalar_prefetch=2,
        grid=(nt, MOE_STEPS),
        in_specs=[
            pl.BlockSpec((da, tm), lambda i, j, cg, nc: (0, i)),
            pl.BlockSpec((tm, LANES), lambda i, j, cg, nc: (i, 0)),
            pl.BlockSpec((1, 2 * ff, d), wmap),
            pl.BlockSpec((1, d, ff), wmap),
        ],
        out_specs=pl.BlockSpec((tm, d), lambda i, j, cg, nc: (i, 0)),
        scratch_shapes=[pltpu.VMEM((MOE_MAX_CHUNKS, tm, SLOT_CHUNK), BF16),
                        pltpu.VMEM((tm, d), F32),
                        pltpu.VMEM((2, d, SLOT_CHUNK), BF16),
                        pltpu.VMEM((2, 16, SLOT_CHUNK), F32),
                        pltpu.VMEM((2, d, SLOT_CHUNK), BF16)],
    )
    return pl.pallas_call(
        _moe_kernel,
        grid_spec=grid_spec,
        out_shape=jax.ShapeDtypeStruct((t, d), BF16),
        compiler_params=_cparams("parallel", "arbitrary"),
        name="moe",
    )(chunk_group, n_chunks, ht_aug, slot, wgu_t, wd_t)


def _chunk_tables(cnt):
    per_group = (cnt[:, 0, :N_GROUPS] + SLOT_CHUNK - 1) // SLOT_CHUNK
    ends = jnp.cumsum(per_group, axis=1)
    n_chunks = ends[:, -1]
    j = jnp.arange(MOE_STEPS, dtype=jnp.int32)[None, :]
    jj = jnp.minimum(j, n_chunks[:, None] - 1)
    group = jnp.sum((jj[:, :, None] >= ends[:, None, :]).astype(jnp.int32), axis=-1)
    return group.reshape(-1).astype(jnp.int32), n_chunks.astype(jnp.int32)


def _pack_expert_weights(w_gate, w_up, w_down):
    ne, d, f = w_gate.shape
    tr = lambda w: jnp.swapaxes(w, 1, 2).reshape(N_GROUPS, FFN_SPLIT, -1, d)
    wgu = jnp.concatenate([tr(w_gate), tr(w_up)], axis=2).reshape(
        N_GROUPS, 2 * EXPERTS_PER_GROUP * f, d).astype(BF16)
    wd = jnp.swapaxes(w_down.reshape(N_GROUPS, EXPERTS_PER_GROUP * f, d), 1, 2).astype(BF16)
    return wgu, wd


def _norm_kernel(x_ref, y_ref, g_ref, o_ref):
    x = x_ref[...] + y_ref[...].astype(F32)
    o_ref[...] = x * lax.rsqrt(jnp.mean(x * x, axis=-1, keepdims=True) + EPS) * g_ref[...]


def _norm_call(x2, y2, g):
    t, d = x2.shape
    tm = NORM_TM
    row = lambda i: (i, 0)
    return pl.pallas_call(
        _norm_kernel,
        grid=(t // tm,),
        in_specs=[pl.BlockSpec((tm, d), row), pl.BlockSpec((tm, d), row),
                  pl.BlockSpec((1, d), lambda i: (0, 0))],
        out_specs=pl.BlockSpec((tm, d), row),
        out_shape=jax.ShapeDtypeStruct((t, d), F32),
        compiler_params=_cparams("parallel"),
        name="final_norm",
    )(x2, y2, g)


def kernel(x, g_mix, w_in, conv_w, conv_b, conv_ln_g, conv_ln_b, w_conv_out, w_attn_out, w_out, rel_bias, g_ffn, w_group_router, b_group_router, w_expert_router, b_expert_router, w_e_gate, w_e_up, w_e_down, g_final):
    batch, seq, d = x.shape
    depth = g_mix.shape[0]
    t = batch * seq
    assert seq % CONV_TL == 0 and seq % KEY_CHUNK == 0 and t % MOE_TM == 0
    x2 = x.reshape(t, d)
    y2 = None
    bias_tiles = _bias_tiles(rel_bias)
    for l in range(depth):
        w_pack = _pack_proj_weights(w_in[l])
        outs = _proj_call(x2, y2, g_mix[l][None, :], w_pack)
        if y2 is not None:
            x2, outs = outs[0], outs[1:]
        u, q_hm, k2, vt, qi_hm, ki2, wit, gc, ga = outs
        vc = _conv_call(u.reshape(batch, seq, 2 * CONV_CH), conv_w[l], conv_b[l][None, :],
                        conv_ln_g[l][None, :], conv_ln_b[l][None, :]).reshape(t, CONV_CH)
        at = _attn_call(q_hm, qi_hm, wit, k2, vt, ki2, bias_tiles, batch, seq)
        wr, br = _pack_router(w_group_router[l], b_group_router[l], w_expert_router[l],
                              b_expert_router[l])
        x2, ht_aug, gone = _merge_call(x2, vc, at, gc, ga, w_conv_out[l].astype(BF16),
                                       w_attn_out[l].astype(BF16), w_out[l].astype(BF16),
                                       g_ffn[l][None, :], wr, br)
        slot, cnt = _plan_call(gone)
        chunk_group, n_chunks = _chunk_tables(cnt)
        wgu_t, wd_t = _pack_expert_weights(w_e_gate[l], w_e_up[l], w_e_down[l])
        y2 = _moe_call(ht_aug, slot, chunk_group, n_chunks, wgu_t, wd_t)
    return _norm_call(x2, y2, g_final[None, :]).reshape(batch, seq, d)
```

```python
import functools
import math

import jax
import jax.numpy as jnp
import numpy as np
from jax import lax
from jax.experimental import pallas as pl
from jax.experimental.pallas import tpu as pltpu

CHUNK = 64
CONV_CH = 512
CONV_WIDTH = 31
N_HEADS = 8
HEAD_DIM = 64
N_KV = 2
KV_REP = N_HEADS // N_KV
ATTN_W = N_HEADS * HEAD_DIM
IDX_HEADS = 8
IDX_DIM = 32
TOPK_MAX = 256
Q_BLOCK = 128
REL_BUCKETS = 32
REL_MAX_DIST = 128
N_GROUPS = 4
EXPERTS_PER_GROUP = 8
N_EXPERTS = N_GROUPS * EXPERTS_PER_GROUP
EXPERT_FF = 256
EPS = 1e-6

LANES = 128
SUBLANES = 8
VMEM_LIMIT_BYTES = 56 * 1024 * 1024

LOG2E = math.log2(math.e)
NEG_BIG = -1e30
INT_MIN = -(2 ** 31)
KEY_MIN_FINITE = -0x7F800000
KEY_MAX_FINITE = 0x7F7FFFFF

BF16 = jnp.bfloat16
F32 = jnp.float32

PROJ_TM = 512
CONV_TL = 256
CONV_HALO = 32
CONV_ROWS = 128
MERGE_TM = 512
MOE_TM = 1024
NORM_TM = 1024

ROUTER_E_OFF = 32


def _cparams(*sem):
    return pltpu.CompilerParams(dimension_semantics=sem, vmem_limit_bytes=VMEM_LIMIT_BYTES)


def _sigmoid(x):
    return 1.0 / (1.0 + jnp.exp(-x))


def _dot(a, b):
    return jnp.dot(a, b, preferred_element_type=F32)


def _dot_nt(a, b):
    return lax.dot_general(a, b, (((1,), (1,)), ((), ())), preferred_element_type=F32)


SEG_U = (0, 2 * CONV_CH)
SEG_Q = (SEG_U[1], SEG_U[1] + N_HEADS * LANES)
SEG_K = (SEG_Q[1], SEG_Q[1] + LANES)
SEG_V = (SEG_K[1], SEG_K[1] + LANES)
SEG_QI = (SEG_V[1], SEG_V[1] + IDX_HEADS * IDX_DIM)
SEG_KI = (SEG_QI[1], SEG_QI[1] + LANES)
SEG_WI = (SEG_KI[1], SEG_KI[1] + LANES)
SEG_GC = (SEG_WI[1], SEG_WI[1] + 1024)
SEG_GA = (SEG_GC[1], SEG_GC[1] + 1024)
PROJ_COLS = SEG_GA[1]


def _proj_kernel(*refs, has_y):
    if has_y:
        x_ref, y_ref, g_ref, w_ref, xo_ref = refs[:5]
        x = x_ref[...] + y_ref[...].astype(F32)
        xo_ref[...] = x
    else:
        x_ref, g_ref, w_ref = refs[:3]
        x = x_ref[...]
    u_ref, q_ref, k_ref, vt_ref, qi_ref, ki_ref, wit_ref, gc_ref, ga_ref = refs[-9:]
    h = x * lax.rsqrt(jnp.mean(x * x, axis=-1, keepdims=True) + EPS) * g_ref[...]
    hb = h.astype(BF16)

    def seg(s):
        return _dot(hb, w_ref[:, s[0]:s[1]])

    u_ref[...] = seg(SEG_U).astype(BF16)
    pq = seg(SEG_Q)
    for hh in range(N_HEADS):
        q_ref[hh] = pq[:, hh * LANES:(hh + 1) * LANES].astype(BF16)
    k_ref[...] = seg(SEG_K).astype(BF16)
    pv = seg(SEG_V)
    tm = pv.shape[0]
    for c in range(tm // LANES):
        vt_ref[c] = pv[c * LANES:(c + 1) * LANES, :].T.astype(BF16)
    pqi = seg(SEG_QI)
    for hh in range(IDX_HEADS):
        qi_ref[hh] = pqi[:, hh * IDX_DIM:(hh + 1) * IDX_DIM].astype(BF16)
    ki_ref[...] = seg(SEG_KI)[:, :IDX_DIM].astype(BF16)
    pwi = seg(SEG_WI)
    for c in range(tm // LANES):
        wit_ref[:, c * LANES:(c + 1) * LANES] = pwi[c * LANES:(c + 1) * LANES, :].T[:IDX_HEADS, :]
    gc_ref[...] = _sigmoid(seg(SEG_GC)).astype(BF16)
    ga_ref[...] = _sigmoid(seg(SEG_GA)).astype(BF16)


def _proj_call(x2, y2, g, w_pack):
    t, d = x2.shape
    tm = PROJ_TM
    nq = t // Q_BLOCK
    row = lambda i: (i, 0)
    has_y = y2 is not None
    out_shape = (jax.ShapeDtypeStruct((t, d), F32),) if has_y else ()
    out_specs = (pl.BlockSpec((tm, d), row),) if has_y else ()
    out_shape += (
        jax.ShapeDtypeStruct((t, 2 * CONV_CH), BF16),
        jax.ShapeDtypeStruct((N_HEADS, t, LANES), BF16),
        jax.ShapeDtypeStruct((t, LANES), BF16),
        jax.ShapeDtypeStruct((nq, LANES, Q_BLOCK), BF16),
        jax.ShapeDtypeStruct((IDX_HEADS, t, IDX_DIM), BF16),
        jax.ShapeDtypeStruct((t, IDX_DIM), BF16),
        jax.ShapeDtypeStruct((IDX_HEADS, t), F32),
        jax.ShapeDtypeStruct((t, d), BF16),
        jax.ShapeDtypeStruct((t, d), BF16),
    )
    out_specs += (
        pl.BlockSpec((tm, 2 * CONV_CH), row),
        pl.BlockSpec((N_HEADS, tm, LANES), lambda i: (0, i, 0)),
        pl.BlockSpec((tm, LANES), row),
        pl.BlockSpec((tm // Q_BLOCK, LANES, Q_BLOCK), lambda i: (i, 0, 0)),
        pl.BlockSpec((IDX_HEADS, tm, IDX_DIM), lambda i: (0, i, 0)),
        pl.BlockSpec((tm, IDX_DIM), row),
        pl.BlockSpec((IDX_HEADS, tm), lambda i: (0, i)),
        pl.BlockSpec((tm, d), row),
        pl.BlockSpec((tm, d), row),
    )
    streams = (x2, y2) if has_y else (x2,)
    return pl.pallas_call(
        functools.partial(_proj_kernel, has_y=has_y),
        grid=(t // tm,),
        in_specs=[pl.BlockSpec((tm, d), row) for _ in streams] + [
            pl.BlockSpec((1, d), lambda i: (0, 0)),
            pl.BlockSpec((d, PROJ_COLS), lambda i: (0, 0)),
        ],
        out_specs=out_specs,
        out_shape=out_shape,
        compiler_params=_cparams("parallel"),
        name="proj",
    )(*streams, g, w_pack)


def _pack_proj_weights(w_in):
    d = w_in.shape[0]
    splits = (2 * CONV_CH, ATTN_W, N_KV * HEAD_DIM, N_KV * HEAD_DIM, IDX_HEADS * IDX_DIM,
              IDX_DIM, IDX_HEADS, d, d)
    offs = np.concatenate([[0], np.cumsum(splits)])
    w_u, w_q, w_k, w_v, w_qi, w_ki, w_wi, w_gc, w_ga = [
        w_in[:, int(offs[i]):int(offs[i + 1])] for i in range(len(splits))]
    w_q = (w_q * (HEAD_DIM ** -0.5 * LOG2E)).reshape(d, N_KV, KV_REP, HEAD_DIM)
    q_pad = jnp.zeros((d, N_KV, KV_REP, N_KV, HEAD_DIM), w_in.dtype)
    for gidx in range(N_KV):
        q_pad = q_pad.at[:, gidx, :, gidx, :].set(w_q[:, gidx])
    q_pad = q_pad.reshape(d, N_HEADS * LANES)
    pad = lambda w: jnp.pad(w, ((0, 0), (0, LANES - w.shape[1])))
    packed = jnp.concatenate([w_u, q_pad, w_k, w_v, w_qi, pad(w_ki), pad(w_wi), w_gc, w_ga], axis=1)
    assert packed.shape[1] == PROJ_COLS
    return packed.astype(BF16)


def _conv_kernel(u_ref, halo_ref, cw_ref, cb_ref, lg_ref, lb_ref, o_ref, ext_ref, y_ref):
    i = pl.program_id(1)
    tl = u_ref.shape[1]

    def glu(u):
        u = u.astype(F32)
        return u[:, :CONV_CH] * _sigmoid(u[:, CONV_CH:])

    halo = glu(halo_ref[0])
    ext_ref[0:CONV_HALO, :] = jnp.where(i > 0, halo, 0.0)
    ext_ref[CONV_HALO:, :] = glu(u_ref[0])
    first = CONV_HALO - (CONV_WIDTH - 1)
    for c in range(CONV_CH // LANES):
        cs = slice(c * LANES, (c + 1) * LANES)
        for r0 in range(0, tl, CONV_ROWS):
            out = cb_ref[:, cs]
            for b in range(SUBLANES):
                win = CONV_ROWS + (SUBLANES if b else 0)
                part = jnp.zeros((win, LANES), F32)
                for j in range(CONV_WIDTH):
                    if (first + j) % SUBLANES == b:
                        lo = first + j - b + r0
                        part = part + cw_ref[j:j + 1, cs] * ext_ref[lo:lo + win, cs]
                out = out + part[b:b + CONV_ROWS]
            y_ref[r0:r0 + CONV_ROWS, cs] = out
    y = y_ref[...]
    mu = jnp.mean(y, axis=-1, keepdims=True)
    yc = y - mu
    var = jnp.mean(yc * yc, axis=-1, keepdims=True)
    yn = yc * lax.rsqrt(var + EPS) * lg_ref[...] + lb_ref[...]
    o_ref[0] = (yn * _sigmoid(yn)).astype(BF16)


def _conv_call(u3, conv_w, conv_b, ln_g, ln_b):
    b, l, _ = u3.shape
    tl = CONV_TL
    halo_blocks = tl // CONV_HALO
    const = lambda bi, i: (0, 0)
    return pl.pallas_call(
        _conv_kernel,
        grid=(b, l // tl),
        in_specs=[
            pl.BlockSpec((1, tl, 2 * CONV_CH), lambda bi, i: (bi, i, 0)),
            pl.BlockSpec((1, CONV_HALO, 2 * CONV_CH),
                         lambda bi, i: (bi, jnp.maximum(i * halo_blocks - 1, 0), 0)),
            pl.BlockSpec((CONV_WIDTH, CONV_CH), const),
            pl.BlockSpec((1, CONV_CH), const),
            pl.BlockSpec((1, CONV_CH), const),
            pl.BlockSpec((1, CONV_CH), const),
        ],
        out_specs=pl.BlockSpec((1, tl, CONV_CH), lambda bi, i: (bi, i, 0)),
        out_shape=jax.ShapeDtypeStruct((b, l, CONV_CH), BF16),
        scratch_shapes=[pltpu.VMEM((CONV_HALO + tl, CONV_CH), F32), pltpu.VMEM((tl, CONV_CH), F32)],
        compiler_params=_cparams("parallel", "parallel"),
        name="conv",
    )(u3, u3, conv_w, conv_b, ln_g, ln_b)


def _sortable_key(score):
    b = lax.bitcast_convert_type(score, jnp.int32)
    return b ^ ((b >> 31) & 0x7FFFFFFF)


KEY_CHUNK = 2 * Q_BLOCK
HALF16 = 1 << 15
LO_SKIP_BITS = 4
ONES_ROWS = 16


def _tree_reduce(x, rows, op):
    parts = [x[j * rows:(j + 1) * rows] for j in range(x.shape[0] // rows)]
    while len(parts) > 1:
        parts = [op(a, b) for a, b in zip(parts[0::2], parts[1::2])]
    return parts[0]


def _attn_kernel(*refs, topk, max_chunks):
    nch = (pl.program_id(1) + 2) // 2
    lax.switch(nch - 1, [functools.partial(_attn_body, *refs, topk=topk, nch=n + 1)
                         for n in range(max_chunks)])


def _static_loop(lo, hi, body, carry):
    for c in range(lo, hi):
        carry = body(c, carry)
    return carry


def _attn_body(q_ref, qi_ref, wit_ref, k_ref, vt_ref, ki_ref, bias_ref, o_ref,
               key_ref, khi_ref, klo_ref, madd_ref, thr_ref, x_ref, p_ref, acc_ref, *, topk, nch):
    i = pl.program_id(1)
    qb = Q_BLOCK
    ch = KEY_CHUNK
    nkb = i + 1
    q_all = q_ref[...].reshape(N_HEADS * qb, LANES)
    qi_st = qi_ref[...].reshape(IDX_HEADS * qb, IDX_DIM)
    wit = wit_ref[...]
    t_loc = lax.broadcasted_iota(jnp.int32, (1, qb), 1)
    limit = ((i * qb + t_loc) // CHUNK + 1) * CHUNK
    idx_scale = (IDX_DIM ** -0.5) * (IDX_HEADS ** -0.5)

    def chunk_rows(c):
        return slice(c * ch, (c + 1) * ch)

    def score_body(c, carry):
        rows = chunk_rows(c)
        dots = _dot_nt(ki_ref[rows, :], qi_st)
        sc = jnp.zeros((ch, qb), F32)
        for hh in range(IDX_HEADS):
            sc = sc + jnp.maximum(dots[:, hh * qb:(hh + 1) * qb], 0.0) * wit[hh:hh + 1, :]
        key = _sortable_key(sc * idx_scale)
        s_glob = c * ch + lax.broadcasted_iota(jnp.int32, (ch, qb), 0)
        key = jnp.where(s_glob < limit, key, INT_MIN)
        key_ref[rows, :] = key
        khi_ref[rows, :] = (key >> 16).astype(jnp.int16)
        klo_ref[rows, :] = ((key & 0xFFFF) - HALF16).astype(jnp.int16)
        return carry

    _static_loop(0, nch, score_body, 0)

    def count32(pred):
        def body(c, acc):
            ind = jnp.where(pred(key_ref[chunk_rows(c), :]), 1, 0)
            return acc + _tree_reduce(ind, acc.shape[0], jnp.add)
        acc = _static_loop(0, nch, body, jnp.zeros((4 * SUBLANES, qb), jnp.int32))
        return jnp.sum(acc, axis=0, keepdims=True)

    def search(n):
        def count16(ref, pred):
            parts = []
            for c in range(n):
                ind = jnp.where(pred(ref[c * ch:(c + 1) * ch, :]), jnp.int16(1), jnp.int16(0))
                parts.append(_tree_reduce(ind, 4 * SUBLANES, jnp.add))
            while len(parts) > 1:
                pairs = [a + b for a, b in zip(parts[0::2], parts[1::2])]
                parts = pairs + parts[len(pairs) * 2:]
            return jnp.sum(parts[0].astype(jnp.int32), axis=0, keepdims=True)

        def refine(ref, want, state, first_bit, n_bits):
            def bit_body(it, state):
                lo, cnt = state
                cand = lo + jnp.left_shift(jnp.int32(1), first_bit - it)
                c16 = cand.astype(jnp.int16)
                cnt_c = count16(ref, lambda v: v >= c16)
                take = cnt_c >= want
                return jnp.where(take, cand, lo), jnp.where(take, cnt_c, cnt)
            return lax.fori_loop(0, n_bits, bit_body, state)

        start = (jnp.full((1, qb), -HALF16, jnp.int32), jnp.full((1, qb), 2 ** 30, jnp.int32))
        t_hi, _ = refine(khi_ref, topk, start, 15, 16)
        t_hi16 = t_hi.astype(jnp.int16)
        n_above = count16(khi_ref, lambda v: v > t_hi16)
        for c in range(n):
            rows = slice(c * ch, (c + 1) * ch)
            klo_ref[rows, :] = jnp.where(khi_ref[rows, :] == t_hi16, klo_ref[rows, :],
                                         jnp.int16(-HALF16))
        want_lo = topk - n_above
        t_lo, cnt_lo = refine(klo_ref, want_lo, start, 15, 16 - LO_SKIP_BITS)
        thr_ref[1:2, :] = t_lo

        @pl.when(jnp.max(cnt_lo - want_lo) > 0)
        def _():
            thr_ref[1:2, :] = refine(klo_ref, want_lo, (t_lo, cnt_lo), LO_SKIP_BITS - 1,
                                     LO_SKIP_BITS)[0]

        t_lo = thr_ref[1:2, :]
        thr_ref[0:1, :] = jnp.maximum(jnp.left_shift(t_hi, 16) | (t_lo + HALF16), KEY_MIN_FINITE)

    thr_ref[...] = jnp.full(thr_ref.shape, KEY_MIN_FINITE, jnp.int32)

    @pl.when(nkb * qb > topk)
    def _():
        search(nch)

    thr = thr_ref[0:1, :]
    n_ge = count32(lambda key: key >= thr)
    has_ties = jnp.max(n_ge) > topk

    @pl.when(jnp.logical_not(has_ties))
    def _():
        def body(c, carry):
            rows = chunk_rows(c)
            key = key_ref[rows, :]
            sel = jnp.logical_and(key >= thr, key <= KEY_MAX_FINITE)
            madd_ref[rows, :] = jnp.where(sel, 0.0, NEG_BIG)
            return carry
        _static_loop(0, nch, body, 0)

    @pl.when(has_ties)
    def _():
        n_gt = count32(lambda key: key > thr)
        need = (topk - n_gt).astype(F32)
        lower = (lax.broadcasted_iota(jnp.int32, (ch, ch), 1)
                 < lax.broadcasted_iota(jnp.int32, (ch, ch), 0)).astype(BF16)

        def body(c, seen):
            rows = chunk_rows(c)
            key = key_ref[rows, :]
            eq = key == thr
            eqf = jnp.where(eq, 1.0, 0.0)
            before = _dot(lower, eqf.astype(BF16)) + seen
            sel = jnp.logical_or(key > thr, jnp.logical_and(eq, before < need))
            sel = jnp.logical_and(sel, key <= KEY_MAX_FINITE)
            madd_ref[rows, :] = jnp.where(sel, 0.0, NEG_BIG)
            return seen + jnp.sum(eqf, axis=0, keepdims=True)
        _static_loop(0, nch, body, jnp.zeros((1, qb), F32))

    def mask_rows(rows, tile, mparts):
        madd = madd_ref[rows, :]
        lg = _dot_nt(k_ref[rows, :], q_all)
        out = []
        for hh in range(N_HEADS):
            x = lg[:, hh * qb:(hh + 1) * qb] + madd
            if tile is not None:
                x = x + bias_ref[tile, hh]
            x_ref[hh, rows, :] = x
            out.append(jnp.maximum(mparts[hh], _tree_reduce(x, SUBLANES, jnp.maximum)))
        return tuple(out)

    def far_body(c, mparts):
        return mask_rows(chunk_rows(c), None, mparts)

    def near_body(c, mparts):
        for half in range(ch // qb):
            jb = c * (ch // qb) + half
            d = i - jb
            tile = jnp.where(d == 0, 0, jnp.where(d == 1, 1, 2))
            mparts = mask_rows(slice(jb * qb, (jb + 1) * qb), tile, mparts)
        return mparts

    n_far = max(nch - 2, 0)
    mparts = tuple(jnp.full((SUBLANES, qb), NEG_BIG, F32) for _ in range(N_HEADS))
    mparts = _static_loop(0, n_far, far_body, mparts)
    mparts = _static_loop(n_far, nch, near_body, mparts)
    m_rows = [jnp.max(mp, axis=0, keepdims=True) for mp in mparts]

    acc_ref[...] = jnp.zeros(acc_ref.shape, F32)
    ones = jnp.ones((ONES_ROWS, ch), BF16)

    def prob_chunk(c):
        rows = chunk_rows(c)
        for hh in range(N_HEADS):
            cols = slice(hh * qb, (hh + 1) * qb)
            p_ref[hh, rows, :] = jnp.exp2((x_ref[hh, rows, :] - m_rows[hh]).astype(BF16))

    def pv_chunk(c):
        rows = chunk_rows(c)
        vt2 = jnp.concatenate([vt_ref[2 * c], vt_ref[2 * c + 1]], axis=1)
        for g in range(N_KV):
            lhs = jnp.concatenate([vt2[g * HEAD_DIM:(g + 1) * HEAD_DIM, :], ones], axis=0)
            probs = jnp.concatenate([p_ref[g * KV_REP + r, rows, :] for r in range(KV_REP)], axis=1)
            acc_ref[g] += _dot(lhs, probs)

    prob_chunk(0)

    def pv_body(c, carry):
        pv_chunk(c - 1)
        prob_chunk(c)
        return carry

    _static_loop(1, nch, pv_body, 0)
    pv_chunk(nch - 1)

    for hp in range(N_HEADS // 2):
        parts = []
        for hh in (2 * hp, 2 * hp + 1):
            g, r = hh // KV_REP, hh % KV_REP
            cols = slice(r * qb, (r + 1) * qb)
            inv_l = 1.0 / acc_ref[g, HEAD_DIM:HEAD_DIM + 1, cols]
            parts.append(acc_ref[g, 0:HEAD_DIM, cols] * inv_l)
        st = jnp.concatenate(parts, axis=0)
        o_ref[:, hp * LANES:(hp + 1) * LANES] = st.T.astype(BF16)


def _attn_call(q_hm, qi_hm, wit, k2, vt, ki2, bias_tiles, batch, seq):
    t = k2.shape[0]
    nblk = seq // Q_BLOCK
    topk = min(TOPK_MAX, seq // 4)
    qrow = lambda b, i: (0, b * nblk + i, 0)
    return pl.pallas_call(
        functools.partial(_attn_kernel, topk=topk, max_chunks=seq // KEY_CHUNK),
        grid=(batch, nblk),
        in_specs=[
            pl.BlockSpec((N_HEADS, Q_BLOCK, LANES), qrow),
            pl.BlockSpec((IDX_HEADS, Q_BLOCK, IDX_DIM), qrow),
            pl.BlockSpec((IDX_HEADS, Q_BLOCK), lambda b, i: (0, b * nblk + i)),
            pl.BlockSpec((seq, LANES), lambda b, i: (b, 0)),
            pl.BlockSpec((nblk, LANES, Q_BLOCK), lambda b, i: (b, 0, 0)),
            pl.BlockSpec((seq, IDX_DIM), lambda b, i: (b, 0)),
            pl.BlockSpec((3, N_HEADS, Q_BLOCK, Q_BLOCK), lambda b, i: (0, 0, 0, 0)),
        ],
        out_specs=pl.BlockSpec((Q_BLOCK, ATTN_W), lambda b, i: (b * nblk + i, 0)),
        out_shape=jax.ShapeDtypeStruct((t, ATTN_W), BF16),
        scratch_shapes=[
            pltpu.VMEM((seq, Q_BLOCK), jnp.int32),
            pltpu.VMEM((seq, Q_BLOCK), jnp.int16),
            pltpu.VMEM((seq, Q_BLOCK), jnp.int16),
            pltpu.VMEM((seq, Q_BLOCK), F32),
            pltpu.VMEM((SUBLANES, Q_BLOCK), jnp.int32),
            pltpu.VMEM((N_HEADS, seq, Q_BLOCK), F32),
            pltpu.VMEM((N_HEADS, seq, Q_BLOCK), BF16),
            pltpu.VMEM((N_KV, HEAD_DIM + ONES_ROWS, KV_REP * Q_BLOCK), F32),
        ],
        compiler_params=_cparams("parallel", "arbitrary"),
        name="attn",
    )(q_hm, qi_hm, wit, k2, vt, ki2, bias_tiles)


def _rel_bucket(rel):
    nb = REL_BUCKETS // 2
    max_exact = nb // 2
    ret = jnp.where(rel < 0, nb, 0)
    n = jnp.abs(rel)
    nf = jnp.maximum(n, 1).astype(jnp.float32)
    large = max_exact + (jnp.log(nf / max_exact) / math.log(REL_MAX_DIST / max_exact)
                         * (nb - max_exact)).astype(jnp.int32)
    large = jnp.minimum(large, nb - 1)
    return ret + jnp.where(n < max_exact, n, large)


def _bias_tiles(rel_bias):
    s = jnp.arange(Q_BLOCK, dtype=jnp.int32)[:, None]
    tq = jnp.arange(Q_BLOCK, dtype=jnp.int32)[None, :]
    d = jnp.arange(3, dtype=jnp.int32)[:, None, None]
    bucket = _rel_bucket(d * Q_BLOCK + tq - s)
    onehot = bucket[:, None, :, :, None] == jnp.arange(REL_BUCKETS, dtype=jnp.int32)
    tiles = jnp.sum(jnp.where(onehot, rel_bias.T[None, :, None, None, :], 0.0), axis=-1)
    return ((tiles - tiles[2:3, :, :1, :1]) * LOG2E).astype(F32)


AUG_ROWS = 48


def _merge_kernel(x_ref, vc_ref, at_ref, gc_ref, ga_ref, wco_ref, wao_ref, wo_ref, g_ref, wr_ref,
                  br_ref, xo_ref, ht_ref, gone_ref):
    y_conv = _dot(vc_ref[...], wco_ref[...])
    y_attn = _dot(at_ref[...], wao_ref[...])
    merged = gc_ref[...].astype(F32) * y_conv + ga_ref[...].astype(F32) * y_attn
    x = x_ref[...] + _dot(merged.astype(BF16), wo_ref[...])
    xo_ref[...] = x
    h = x * lax.rsqrt(jnp.mean(x * x, axis=-1, keepdims=True) + EPS) * g_ref[...]
    hb = h.astype(BF16)
    d = h.shape[1]
    ht_ref[0:d, :] = hb.astype(F32).T.astype(BF16)

    logits = _dot(hb, wr_ref[...]) + br_ref[...]
    lane = lax.broadcasted_iota(jnp.int32, logits.shape, 1)
    neg_inf = -jnp.inf
    gl = jnp.where(lane < N_GROUPS, logits, neg_inf)
    gmax = jnp.max(gl, axis=-1, keepdims=True)
    grp = jnp.min(jnp.where(gl == gmax, lane, LANES), axis=-1, keepdims=True)
    p_grp = 1.0 / jnp.sum(jnp.exp(gl - gmax), axis=-1, keepdims=True)
    e_lo = ROUTER_E_OFF + grp * EXPERTS_PER_GROUP
    in_grp = jnp.logical_and(lane >= e_lo, lane < e_lo + EXPERTS_PER_GROUP)
    el = jnp.where(in_grp, logits, neg_inf)
    v1 = jnp.max(el, axis=-1, keepdims=True)
    i1 = jnp.min(jnp.where(jnp.logical_and(in_grp, el == v1), lane, LANES), axis=-1, keepdims=True)
    rest = jnp.logical_and(in_grp, lane != i1)
    el2 = jnp.where(rest, logits, neg_inf)
    v2 = jnp.max(el2, axis=-1, keepdims=True)
    i2 = jnp.min(jnp.where(jnp.logical_and(rest, el2 == v2), lane, LANES), axis=-1, keepdims=True)
    e2 = jnp.exp(v2 - v1)
    p1 = p_grp / (1.0 + e2)
    p2 = p_grp * e2 / (1.0 + e2)
    gone_ref[...] = jnp.where(lane == grp, 1.0, 0.0).astype(BF16)
    w8 = jnp.where(lane == i1 - e_lo, p1, jnp.where(lane == i2 - e_lo, p2, 0.0))
    w8t = w8.T[0:16, :]
    hi = w8t.astype(BF16)
    r1 = w8t - hi.astype(F32)
    mid = r1.astype(BF16)
    lo = (r1 - mid.astype(F32)).astype(BF16)
    ht_ref[d:d + 16, :] = hi
    ht_ref[d + 16:d + 32, :] = mid
    ht_ref[d + 32:d + 48, :] = lo


def _merge_call(x2, vc, at, gc, ga, wco, wao, wo, g, wr, br):
    t, d = x2.shape
    tm = MERGE_TM
    row = lambda i: (i, 0)
    const = lambda i: (0, 0)
    return pl.pallas_call(
        _merge_kernel,
        grid=(t // tm,),
        in_specs=[
            pl.BlockSpec((tm, d), row),
            pl.BlockSpec((tm, CONV_CH), row),
            pl.BlockSpec((tm, ATTN_W), row),
            pl.BlockSpec((tm, d), row),
            pl.BlockSpec((tm, d), row),
            pl.BlockSpec((CONV_CH, d), const),
            pl.BlockSpec((ATTN_W, d), const),
            pl.BlockSpec((d, d), const),
            pl.BlockSpec((1, d), const),
            pl.BlockSpec((d, LANES), const),
            pl.BlockSpec((1, LANES), const),
        ],
        out_specs=(pl.BlockSpec((tm, d), row), pl.BlockSpec((d + AUG_ROWS, tm), lambda i: (0, i)),
                   pl.BlockSpec((tm, LANES), row)),
        out_shape=(jax.ShapeDtypeStruct((t, d), F32),
                   jax.ShapeDtypeStruct((d + AUG_ROWS, t), BF16),
                   jax.ShapeDtypeStruct((t, LANES), BF16)),
        compiler_params=_cparams("parallel"),
        name="merge",
    )(x2, vc, at, gc, ga, wco, wao, wo, g, wr, br)


def _pack_router(w_gr, b_gr, w_er, b_er):
    d = w_gr.shape[0]
    w = jnp.zeros((d, LANES), F32)
    w = w.at[:, :N_GROUPS].set(w_gr)
    w = w.at[:, ROUTER_E_OFF:ROUTER_E_OFF + N_EXPERTS].set(
        jnp.moveaxis(w_er, 0, 1).reshape(d, N_EXPERTS))
    b = jnp.zeros((1, LANES), F32)
    b = b.at[0, :N_GROUPS].set(b_gr)
    b = b.at[0, ROUTER_E_OFF:ROUTER_E_OFF + N_EXPERTS].set(b_er.reshape(N_EXPERTS))
    return w.astype(BF16), b


SLOT_CHUNK = 256
FFN_SPLIT = 8
MOE_MAX_CHUNKS = MOE_TM // SLOT_CHUNK + N_GROUPS - 1
MOE_STEPS = MOE_MAX_CHUNKS + 1


def _plan_kernel(gone_ref, ltri_ref, utri_ref, slot_ref, cnt_ref):
    gone = gone_ref[...]
    before = _dot(ltri_ref[...], gone)
    last = gone.shape[0] - 1
    cnt = before[last:last + 1, :] + gone[last:last + 1, :].astype(F32)
    nchunks = jnp.floor((cnt + (SLOT_CHUNK - 1)) * (1.0 / SLOT_CHUNK))
    start = _dot(jnp.broadcast_to(nchunks, (SUBLANES, LANES)).astype(BF16), utri_ref[...])[0:1, :]
    slot = jnp.sum(gone.astype(F32) * (before + start * SLOT_CHUNK), axis=-1, keepdims=True)
    slot_ref[...] = jnp.broadcast_to(slot.astype(jnp.int32), slot_ref.shape)
    cnt_ref[0] = jnp.broadcast_to(cnt, (SUBLANES, LANES)).astype(jnp.int32)


def _plan_call(gone):
    t = gone.shape[0]
    tm = MOE_TM
    nt = t // tm
    r = lax.broadcasted_iota(jnp.int32, (tm, tm), 0)
    c = lax.broadcasted_iota(jnp.int32, (tm, tm), 1)
    ltri = (c < r).astype(BF16)
    utri = (lax.broadcasted_iota(jnp.int32, (LANES, LANES), 0)
            < lax.broadcasted_iota(jnp.int32, (LANES, LANES), 1)).astype(BF16)
    return pl.pallas_call(
        _plan_kernel,
        grid=(nt,),
        in_specs=[pl.BlockSpec((tm, LANES), lambda i: (i, 0)),
                  pl.BlockSpec((tm, tm), lambda i: (0, 0)),
                  pl.BlockSpec((LANES, LANES), lambda i: (0, 0))],
        out_specs=(pl.BlockSpec((tm, LANES), lambda i: (i, 0)),
                   pl.BlockSpec((1, SUBLANES, LANES), lambda i: (i, 0, 0))),
        out_shape=(jax.ShapeDtypeStruct((t, LANES), jnp.int32),
                   jax.ShapeDtypeStruct((nt, SUBLANES, LANES), jnp.int32)),
        compiler_params=_cparams("parallel"),
        name="moe_plan",
    )(gone, ltri, utri)


def _moe_kernel(cg_ref, nc_ref, ht_ref, slot_ref, wgu_ref, wd_ref, o_ref,
                pt_ref, acc_ref, xg_ref, cw_ref, yt_ref):
    i = pl.program_id(0)
    j = pl.program_id(1)
    nc = nc_ref[i]
    tm = slot_ref.shape[0]
    d = acc_ref.shape[1]

    def gather(jn, slot_idx):
        xa = _dot(ht_ref[...], pt_ref[jn])
        xg_ref[slot_idx] = xa[0:d, :].astype(BF16)
        cw_ref[slot_idx] = xa[d:d + 16, :] + xa[d + 16:d + 32, :] + xa[d + 32:d + 48, :]

    def scatter(jp, slot_idx):
        acc_ref[...] += _dot_nt(pt_ref[jp], yt_ref[slot_idx])

    @pl.when(j == 0)
    def _():
        acc_ref[...] = jnp.zeros(acc_ref.shape, F32)
        yt_ref[...] = jnp.zeros(yt_ref.shape, BF16)
        slot = slot_ref[...]
        lane = lax.broadcasted_iota(jnp.int32, (tm, LANES), 1)
        for jj in range(MOE_MAX_CHUNKS):
            for half in range(SLOT_CHUNK // LANES):
                off = jj * SLOT_CHUNK + half * LANES
                pt_ref[jj, :, half * LANES:(half + 1) * LANES] = jnp.where(
                    slot == lane + off, 1.0, 0.0).astype(BF16)
        gather(0, 0)

    @pl.when(j < nc)
    def _():
        cur = j % 2
        scatter(jnp.maximum(j - 1, 0), 1 - cur)
        xg = xg_ref[cur]
        cw = cw_ref[cur]
        es = EXPERTS_PER_GROUP // FFN_SPLIT
        sf = es * EXPERT_FF
        yt = None
        for s in range(FFN_SPLIT):
            ab = _dot(wgu_ref[0, 2 * sf * s:2 * sf * (s + 1), :], xg)
            mids = []
            for e in range(es):
                a = ab[e * EXPERT_FF:(e + 1) * EXPERT_FF, :]
                b = ab[sf + e * EXPERT_FF:sf + (e + 1) * EXPERT_FF, :]
                c = cw[s * es + e:s * es + e + 1, :]
                mids.append((a * _sigmoid(a) * b * c).astype(BF16))
            part = _dot(wd_ref[0, :, sf * s:sf * (s + 1)], jnp.concatenate(mids, axis=0))
            yt = part if yt is None else yt + part
        yt_ref[cur] = yt.astype(BF16)
        gather(jnp.minimum(j + 1, MOE_MAX_CHUNKS - 1), 1 - cur)

    @pl.when(j == nc)
    def _():
        scatter(nc - 1, (nc - 1) % 2)

    @pl.when(j == pl.num_programs(1) - 1)
    def _():
        o_ref[...] = acc_ref[...].astype(BF16)


def _moe_call(ht_aug, slot, chunk_group, n_chunks, wgu_t, wd_t):
    da, t = ht_aug.shape
    d = da - AUG_ROWS
    tm = MOE_TM
    nt = t // tm
    ff = EXPERTS_PER_GROUP * EXPERT_FF
    wmap = lambda i, j, cg, nc: (cg[i * MOE_STEPS + j], 0, 0)
    grid_spec = pltpu.PrefetchScalarGridSpec(
        num_scalar_prefetch=2,
        grid=(nt, MOE_STEPS),
        in_specs=[
            pl.BlockSpec((da, tm), lambda i, j, cg, nc: (0, i)),
            pl.BlockSpec((tm, LANES), lambda i, j, cg, nc: (i, 0)),
            pl.BlockSpec((1, 2 * ff, d), wmap),
            pl.BlockSpec((1, d, ff), wmap),
        ],
        out_specs=pl.BlockSpec((tm, d), lambda i, j, cg, nc: (i, 0)),
        scratch_shapes=[pltpu.VMEM((MOE_MAX_CHUNKS, tm, SLOT_CHUNK), BF16),
                        pltpu.VMEM((tm, d), F32),
                        pltpu.VMEM((2, d, SLOT_CHUNK), BF16),
                        pltpu.VMEM((2, 16, SLOT_CHUNK), F32),
                        pltpu.VMEM((2, d, SLOT_CHUNK), BF16)],
    )
    return pl.pallas_call(
        _moe_kernel,
        grid_spec=grid_spec,
        out_shape=jax.ShapeDtypeStruct((t, d), BF16),
        compiler_params=_cparams("parallel", "arbitrary"),
        name="moe",
    )(chunk_group, n_chunks, ht_aug, slot, wgu_t, wd_t)


def _chunk_tables(cnt):
    per_group = (cnt[:, 0, :N_GROUPS] + SLOT_CHUNK - 1) // SLOT_CHUNK
    ends = jnp.cumsum(per_group, axis=1)
    n_chunks = ends[:, -1]
    j = jnp.arange(MOE_STEPS, dtype=jnp.int32)[None, :]
    jj = jnp.minimum(j, n_chunks[:, None] - 1)
    group = jnp.sum((jj[:, :, None] >= ends[:, None, :]).astype(jnp.int32), axis=-1)
    return group.reshape(-1).astype(jnp.int32), n_chunks.astype(jnp.int32)


def _pack_expert_weights(w_gate, w_up, w_down):
    ne, d, f = w_gate.shape
    tr = lambda w: jnp.swapaxes(w, 1, 2).reshape(N_GROUPS, FFN_SPLIT, -1, d)
    wgu = jnp.concatenate([tr(w_gate), tr(w_up)], axis=2).reshape(
        N_GROUPS, 2 * EXPERTS_PER_GROUP * f, d).astype(BF16)
    wd = jnp.swapaxes(w_down.reshape(N_GROUPS, EXPERTS_PER_GROUP * f, d), 1, 2).astype(BF16)
    return wgu, wd


def _norm_kernel(x_ref, y_ref, g_ref, o_ref):
    x = x_ref[...] + y_ref[...].astype(F32)
    o_ref[...] = x * lax.rsqrt(jnp.mean(x * x, axis=-1, keepdims=True) + EPS) * g_ref[...]


def _norm_call(x2, y2, g):
    t, d = x2.shape
    tm = NORM_TM
    row = lambda i: (i, 0)
    return pl.pallas_call(
        _norm_kernel,
        grid=(t // tm,),
        in_specs=[pl.BlockSpec((tm, d), row), pl.BlockSpec((tm, d), row),
                  pl.BlockSpec((1, d), lambda i: (0, 0))],
        out_specs=pl.BlockSpec((tm, d), row),
        out_shape=jax.ShapeDtypeStruct((t, d), F32),
        compiler_params=_cparams("parallel"),
        name="final_norm",
    )(x2, y2, g)


def kernel(x, g_mix, w_in, conv_w, conv_b, conv_ln_g, conv_ln_b, w_conv_out, w_attn_out, w_out, rel_bias, g_ffn, w_group_router, b_group_router, w_expert_router, b_expert_router, w_e_gate, w_e_up, w_e_down, g_final):
    batch, seq, d = x.shape
    depth = g_mix.shape[0]
    t = batch * seq
    assert seq % CONV_TL == 0 and seq % KEY_CHUNK == 0 and t % MOE_TM == 0
    x2 = x.reshape(t, d)
    y2 = None
    bias_tiles = _bias_tiles(rel_bias)
    for l in range(depth):
        w_pack = _pack_proj_weights(w_in[l])
        outs = _proj_call(x2, y2, g_mix[l][None, :], w_pack)
        if y2 is not None:
            x2, outs = outs[0], outs[1:]
        u, q_hm, k2, vt, qi_hm, ki2, wit, gc, ga = outs
        vc = _conv_call(u.reshape(batch, seq, 2 * CONV_CH), conv_w[l], conv_b[l][None, :],
                        conv_ln_g[l][None, :], conv_ln_b[l][None, :]).reshape(t, CONV_CH)
        at = _attn_call(q_hm, qi_hm, wit, k2, vt, ki2, bias_tiles, batch, seq)
        wr, br = _pack_router(w_group_router[l], b_group_router[l], w_expert_router[l],
                              b_expert_router[l])
        x2, ht_aug, gone = _merge_call(x2, vc, at, gc, ga, w_conv_out[l].astype(BF16),
                                       w_attn_out[l].astype(BF16), w_out[l].astype(BF16),
                                       g_ffn[l][None, :], wr, br)
        slot, cnt = _plan_call(gone)
        chunk_group, n_chunks = _chunk_tables(cnt)
        wgu_t, wd_t = _pack_expert_weights(w_e_gate[l], w_e_up[l], w_e_down[l])
        y2 = _moe_call(ht_aug, slot, chunk_group, n_chunks, wgu_t, wd_t)
    return _norm_call(x2, y2, g_final[None, :]).reshape(batch, seq, d)
```

```python
import functools
import math

import jax
import jax.numpy as jnp
import numpy as np
from jax import lax
from jax.experimental import pallas as pl
from jax.experimental.pallas import tpu as pltpu

CHUNK = 64
CONV_CH = 512
CONV_WIDTH = 31
N_HEADS = 8
HEAD_DIM = 64
N_KV = 2
KV_REP = N_HEADS // N_KV
ATTN_W = N_HEADS * HEAD_DIM
IDX_HEADS = 8
IDX_DIM = 32
TOPK_MAX = 256
Q_BLOCK = 128
REL_BUCKETS = 32
REL_MAX_DIST = 128
N_GROUPS = 4
EXPERTS_PER_GROUP = 8
N_EXPERTS = N_GROUPS * EXPERTS_PER_GROUP
EXPERT_FF = 256
EPS = 1e-6

LANES = 128
SUBLANES = 8
VMEM_LIMIT_BYTES = 56 * 1024 * 1024

LOG2E = math.log2(math.e)
NEG_BIG = -1e30
INT_MIN = -(2 ** 31)
KEY_MIN_FINITE = -0x7F800000
KEY_MAX_FINITE = 0x7F7FFFFF

BF16 = jnp.bfloat16
F32 = jnp.float32

PROJ_TM = 512
CONV_TL = 256
CONV_HALO = 32
CONV_ROWS = 128
MERGE_TM = 512
MOE_TM = 1024
NORM_TM = 1024

ROUTER_E_OFF = 32


def _cparams(*sem):
    return pltpu.CompilerParams(dimension_semantics=sem, vmem_limit_bytes=VMEM_LIMIT_BYTES)


def _sigmoid(x):
    return 1.0 / (1.0 + jnp.exp(-x))


def _dot(a, b):
    return jnp.dot(a, b, preferred_element_type=F32)


def _dot_nt(a, b):
    return lax.dot_general(a, b, (((1,), (1,)), ((), ())), preferred_element_type=F32)


SEG_U = (0, 2 * CONV_CH)
SEG_Q = (SEG_U[1], SEG_U[1] + N_HEADS * LANES)
SEG_K = (SEG_Q[1], SEG_Q[1] + LANES)
SEG_V = (SEG_K[1], SEG_K[1] + LANES)
SEG_QI = (SEG_V[1], SEG_V[1] + IDX_HEADS * IDX_DIM)
SEG_KI = (SEG_QI[1], SEG_QI[1] + LANES)
SEG_WI = (SEG_KI[1], SEG_KI[1] + LANES)
SEG_GC = (SEG_WI[1], SEG_WI[1] + 1024)
SEG_GA = (SEG_GC[1], SEG_GC[1] + 1024)
PROJ_COLS = SEG_GA[1]


def _proj_kernel(*refs, has_y):
    if has_y:
        x_ref, y_ref, g_ref, w_ref, xo_ref = refs[:5]
        x = x_ref[...] + y_ref[...].astype(F32)
        xo_ref[...] = x
    else:
        x_ref, g_ref, w_ref = refs[:3]
        x = x_ref[...]
    u_ref, q_ref, k_ref, vt_ref, qi_ref, ki_ref, wit_ref, gc_ref, ga_ref = refs[-9:]
    h = x * lax.rsqrt(jnp.mean(x * x, axis=-1, keepdims=True) + EPS) * g_ref[...]
    hb = h.astype(BF16)

    def seg(s):
        return _dot(hb, w_ref[:, s[0]:s[1]])

    u_ref[...] = seg(SEG_U).astype(BF16)
    pq = seg(SEG_Q)
    for hh in range(N_HEADS):
        q_ref[hh] = pq[:, hh * LANES:(hh + 1) * LANES].astype(BF16)
    k_ref[...] = seg(SEG_K).astype(BF16)
    pv = seg(SEG_V)
    tm = pv.shape[0]
    for c in range(tm // LANES):
        vt_ref[c] = pv[c * LANES:(c + 1) * LANES, :].T.astype(BF16)
    pqi = seg(SEG_QI)
    for hh in range(IDX_HEADS):
        qi_ref[hh] = pqi[:, hh * IDX_DIM:(hh + 1) * IDX_DIM].astype(BF16)
    ki_ref[...] = seg(SEG_KI)[:, :IDX_DIM].astype(BF16)
    pwi = seg(SEG_WI)
    for c in range(tm // LANES):
        wit_ref[:, c * LANES:(c + 1) * LANES] = pwi[c * LANES:(c + 1) * LANES, :].T[:IDX_HEADS, :]
    gc_ref[...] = _sigmoid(seg(SEG_GC)).astype(BF16)
    ga_ref[...] = _sigmoid(seg(SEG_GA)).astype(BF16)


def _proj_call(x2, y2, g, w_pack):
    t, d = x2.shape
    tm = PROJ_TM
    nq = t // Q_BLOCK
    row = lambda i: (i, 0)
    has_y = y2 is not None
    out_shape = (jax.ShapeDtypeStruct((t, d), F32),) if has_y else ()
    out_specs = (pl.BlockSpec((tm, d), row),) if has_y else ()
    out_shape += (
        jax.ShapeDtypeStruct((t, 2 * CONV_CH), BF16),
        jax.ShapeDtypeStruct((N_HEADS, t, LANES), BF16),
        jax.ShapeDtypeStruct((t, LANES), BF16),
        jax.ShapeDtypeStruct((nq, LANES, Q_BLOCK), BF16),
        jax.ShapeDtypeStruct((IDX_HEADS, t, IDX_DIM), BF16),
        jax.ShapeDtypeStruct((t, IDX_DIM), BF16),
        jax.ShapeDtypeStruct((IDX_HEADS, t), F32),
        jax.ShapeDtypeStruct((t, d), BF16),
        jax.ShapeDtypeStruct((t, d), BF16),
    )
    out_specs += (
        pl.BlockSpec((tm, 2 * CONV_CH), row),
        pl.BlockSpec((N_HEADS, tm, LANES), lambda i: (0, i, 0)),
        pl.BlockSpec((tm, LANES), row),
        pl.BlockSpec((tm // Q_BLOCK, LANES, Q_BLOCK), lambda i: (i, 0, 0)),
        pl.BlockSpec((IDX_HEADS, tm, IDX_DIM), lambda i: (0, i, 0)),
        pl.BlockSpec((tm, IDX_DIM), row),
        pl.BlockSpec((IDX_HEADS, tm), lambda i: (0, i)),
        pl.BlockSpec((tm, d), row),
        pl.BlockSpec((tm, d), row),
    )
    streams = (x2, y2) if has_y else (x2,)
    return pl.pallas_call(
        functools.partial(_proj_kernel, has_y=has_y),
        grid=(t // tm,),
        in_specs=[pl.BlockSpec((tm, d), row) for _ in streams] + [
            pl.BlockSpec((1, d), lambda i: (0, 0)),
            pl.BlockSpec((d, PROJ_COLS), lambda i: (0, 0)),
        ],
        out_specs=out_specs,
        out_shape=out_shape,
        compiler_params=_cparams("parallel"),
        name="proj",
    )(*streams, g, w_pack)


def _pack_proj_weights(w_in):
    d = w_in.shape[0]
    splits = (2 * CONV_CH, ATTN_W, N_KV * HEAD_DIM, N_KV * HEAD_DIM, IDX_HEADS * IDX_DIM,
              IDX_DIM, IDX_HEADS, d, d)
    offs = np.concatenate([[0], np.cumsum(splits)])
    w_u, w_q, w_k, w_v, w_qi, w_ki, w_wi, w_gc, w_ga = [
        w_in[:, int(offs[i]):int(offs[i + 1])] for i in range(len(splits))]
    w_q = (w_q * (HEAD_DIM ** -0.5 * LOG2E)).reshape(d, N_KV, KV_REP, HEAD_DIM)
    q_pad = jnp.zeros((d, N_KV, KV_REP, N_KV, HEAD_DIM), w_in.dtype)
    for gidx in range(N_KV):
        q_pad = q_pad.at[:, gidx, :, gidx, :].set(w_q[:, gidx])
    q_pad = q_pad.reshape(d, N_HEADS * LANES)
    pad = lambda w: jnp.pad(w, ((0, 0), (0, LANES - w.shape[1])))
    packed = jnp.concatenate([w_u, q_pad, w_k, w_v, w_qi, pad(w_ki), pad(w_wi), w_gc, w_ga], axis=1)
    assert packed.shape[1] == PROJ_COLS
    return packed.astype(BF16)


def _conv_kernel(u_ref, halo_ref, cw_ref, cb_ref, lg_ref, lb_ref, o_ref, ext_ref, y_ref):
    i = pl.program_id(1)
    tl = u_ref.shape[1]

    def glu(u):
        u = u.astype(F32)
        return u[:, :CONV_CH] * _sigmoid(u[:, CONV_CH:])

    halo = glu(halo_ref[0])
    ext_ref[0:CONV_HALO, :] = jnp.where(i > 0, halo, 0.0)
    ext_ref[CONV_HALO:, :] = glu(u_ref[0])
    first = CONV_HALO - (CONV_WIDTH - 1)
    for c in range(CONV_CH // LANES):
        cs = slice(c * LANES, (c + 1) * LANES)
        for r0 in range(0, tl, CONV_ROWS):
            out = cb_ref[:, cs]
            for b in range(SUBLANES):
                win = CONV_ROWS + (SUBLANES if b else 0)
                part = jnp.zeros((win, LANES), F32)
                for j in range(CONV_WIDTH):
                    if (first + j) % SUBLANES == b:
                        lo = first + j - b + r0
                        part = part + cw_ref[j:j + 1, cs] * ext_ref[lo:lo + win, cs]
                out = out + part[b:b + CONV_ROWS]
            y_ref[r0:r0 + CONV_ROWS, cs] = out
    y = y_ref[...]
    mu = jnp.mean(y, axis=-1, keepdims=True)
    yc = y - mu
    var = jnp.mean(yc * yc, axis=-1, keepdims=True)
    yn = yc * lax.rsqrt(var + EPS) * lg_ref[...] + lb_ref[...]
    o_ref[0] = (yn * _sigmoid(yn)).astype(BF16)


def _conv_call(u3, conv_w, conv_b, ln_g, ln_b):
    b, l, _ = u3.shape
    tl = CONV_TL
    halo_blocks = tl // CONV_HALO
    const = lambda bi, i: (0, 0)
    return pl.pallas_call(
        _conv_kernel,
        grid=(b, l // tl),
        in_specs=[
            pl.BlockSpec((1, tl, 2 * CONV_CH), lambda bi, i: (bi, i, 0)),
            pl.BlockSpec((1, CONV_HALO, 2 * CONV_CH),
                         lambda bi, i: (bi, jnp.maximum(i * halo_blocks - 1, 0), 0)),
            pl.BlockSpec((CONV_WIDTH, CONV_CH), const),
            pl.BlockSpec((1, CONV_CH), const),
            pl.BlockSpec((1, CONV_CH), const),
            pl.BlockSpec((1, CONV_CH), const),
        ],
        out_specs=pl.BlockSpec((1, tl, CONV_CH), lambda bi, i: (bi, i, 0)),
        out_shape=jax.ShapeDtypeStruct((b, l, CONV_CH), BF16),
        scratch_shapes=[pltpu.VMEM((CONV_HALO + tl, CONV_CH), F32), pltpu.VMEM((tl, CONV_CH), F32)],
        compiler_params=_cparams("parallel", "parallel"),
        name="conv",
    )(u3, u3, conv_w, conv_b, ln_g, ln_b)


def _sortable_key(score):
    b = lax.bitcast_convert_type(score, jnp.int32)
    return b ^ ((b >> 31) & 0x7FFFFFFF)


KEY_CHUNK = 2 * Q_BLOCK
HALF16 = 1 << 15
ONES_ROWS = 16


def _tree_reduce(x, rows, op):
    parts = [x[j * rows:(j + 1) * rows] for j in range(x.shape[0] // rows)]
    while len(parts) > 1:
        parts = [op(a, b) for a, b in zip(parts[0::2], parts[1::2])]
    return parts[0]


def _attn_kernel(*refs, topk, max_chunks):
    nch = (pl.program_id(1) + 2) // 2
    lax.switch(nch - 1, [functools.partial(_attn_body, *refs, topk=topk, nch=n + 1)
                         for n in range(max_chunks)])


def _static_loop(lo, hi, body, carry):
    for c in range(lo, hi):
        carry = body(c, carry)
    return carry


def _attn_body(q_ref, qi_ref, wit_ref, k_ref, vt_ref, ki_ref, bias_ref, o_ref,
               key_ref, khi_ref, klo_ref, madd_ref, thr_ref, x_ref, p_ref, acc_ref, *, topk, nch):
    i = pl.program_id(1)
    qb = Q_BLOCK
    ch = KEY_CHUNK
    nkb = i + 1
    q_all = q_ref[...].reshape(N_HEADS * qb, LANES)
    qi_st = qi_ref[...].reshape(IDX_HEADS * qb, IDX_DIM)
    wit = wit_ref[...]
    t_loc = lax.broadcasted_iota(jnp.int32, (1, qb), 1)
    limit = ((i * qb + t_loc) // CHUNK + 1) * CHUNK
    idx_scale = (IDX_DIM ** -0.5) * (IDX_HEADS ** -0.5)

    def chunk_rows(c):
        return slice(c * ch, (c + 1) * ch)

    def score_body(c, carry):
        rows = chunk_rows(c)
        dots = _dot_nt(ki_ref[rows, :], qi_st)
        sc = jnp.zeros((ch, qb), F32)
        for hh in range(IDX_HEADS):
            sc = sc + jnp.maximum(dots[:, hh * qb:(hh + 1) * qb], 0.0) * wit[hh:hh + 1, :]
        key = _sortable_key(sc * idx_scale)
        s_glob = c * ch + lax.broadcasted_iota(jnp.int32, (ch, qb), 0)
        key = jnp.where(s_glob < limit, key, INT_MIN)
        key_ref[rows, :] = key
        khi_ref[rows, :] = (key >> 16).astype(jnp.int16)
        klo_ref[rows, :] = ((key & 0xFFFF) - HALF16).astype(jnp.int16)
        return carry

    _static_loop(0, nch, score_body, 0)

    def count32(pred):
        def body(c, acc):
            ind = jnp.where(pred(key_ref[chunk_rows(c), :]), 1, 0)
            return acc + _tree_reduce(ind, acc.shape[0], jnp.add)
        acc = _static_loop(0, nch, body, jnp.zeros((4 * SUBLANES, qb), jnp.int32))
        return jnp.sum(acc, axis=0, keepdims=True)

    def search(n):
        def count16(ref, pred):
            parts = []
            for c in range(n):
                ind = jnp.where(pred(ref[c * ch:(c + 1) * ch, :]), jnp.int16(1), jnp.int16(0))
                parts.append(_tree_reduce(ind, 4 * SUBLANES, jnp.add))
            while len(parts) > 1:
                pairs = [a + b for a, b in zip(parts[0::2], parts[1::2])]
                parts = pairs + parts[len(pairs) * 2:]
            return jnp.sum(parts[0].astype(jnp.int32), axis=0, keepdims=True)

        def kth_largest16(ref, want):
            def bit_body(it, lo):
                cand = lo + jnp.left_shift(jnp.int32(1), 15 - it)
                c16 = cand.astype(jnp.int16)
                return jnp.where(count16(ref, lambda v: v >= c16) >= want, cand, lo)
            return lax.fori_loop(0, 16, bit_body, jnp.full((1, qb), -HALF16, jnp.int32))

        t_hi = kth_largest16(khi_ref, topk)
        t_hi16 = t_hi.astype(jnp.int16)
        n_above = count16(khi_ref, lambda v: v > t_hi16)
        for c in range(n):
            rows = slice(c * ch, (c + 1) * ch)
            klo_ref[rows, :] = jnp.where(khi_ref[rows, :] == t_hi16, klo_ref[rows, :],
                                         jnp.int16(-HALF16))
        t_lo = kth_largest16(klo_ref, topk - n_above)
        thr_ref[0:1, :] = jnp.maximum(jnp.left_shift(t_hi, 16) | (t_lo + HALF16), KEY_MIN_FINITE)

    thr_ref[...] = jnp.full(thr_ref.shape, KEY_MIN_FINITE, jnp.int32)

    @pl.when(nkb * qb > topk)
    def _():
        search(nch)

    thr = thr_ref[0:1, :]
    n_ge = count32(lambda key: key >= thr)
    has_ties = jnp.max(n_ge) > topk

    @pl.when(jnp.logical_not(has_ties))
    def _():
        def body(c, carry):
            rows = chunk_rows(c)
            key = key_ref[rows, :]
            sel = jnp.logical_and(key >= thr, key <= KEY_MAX_FINITE)
            madd_ref[rows, :] = jnp.where(sel, 0.0, NEG_BIG)
            return carry
        _static_loop(0, nch, body, 0)

    @pl.when(has_ties)
    def _():
        n_gt = count32(lambda key: key > thr)
        need = (topk - n_gt).astype(F32)
        lower = (lax.broadcasted_iota(jnp.int32, (ch, ch), 1)
                 < lax.broadcasted_iota(jnp.int32, (ch, ch), 0)).astype(BF16)

        def body(c, seen):
            rows = chunk_rows(c)
            key = key_ref[rows, :]
            eq = key == thr
            eqf = jnp.where(eq, 1.0, 0.0)
            before = _dot(lower, eqf.astype(BF16)) + seen
            sel = jnp.logical_or(key > thr, jnp.logical_and(eq, before < need))
            sel = jnp.logical_and(sel, key <= KEY_MAX_FINITE)
            madd_ref[rows, :] = jnp.where(sel, 0.0, NEG_BIG)
            return seen + jnp.sum(eqf, axis=0, keepdims=True)
        _static_loop(0, nch, body, jnp.zeros((1, qb), F32))

    def mask_rows(rows, tile, mparts):
        madd = madd_ref[rows, :]
        lg = _dot_nt(k_ref[rows, :], q_all)
        out = []
        for hh in range(N_HEADS):
            x = lg[:, hh * qb:(hh + 1) * qb] + madd
            if tile is not None:
                x = x + bias_ref[tile, hh]
            x_ref[hh, rows, :] = x
            out.append(jnp.maximum(mparts[hh], _tree_reduce(x, SUBLANES, jnp.maximum)))
        return tuple(out)

    def far_body(c, mparts):
        return mask_rows(chunk_rows(c), None, mparts)

    def near_body(c, mparts):
        for half in range(ch // qb):
            jb = c * (ch // qb) + half
            d = i - jb
            tile = jnp.where(d == 0, 0, jnp.where(d == 1, 1, 2))
            mparts = mask_rows(slice(jb * qb, (jb + 1) * qb), tile, mparts)
        return mparts

    n_far = max(nch - 2, 0)
    mparts = tuple(jnp.full((SUBLANES, qb), NEG_BIG, F32) for _ in range(N_HEADS))
    mparts = _static_loop(0, n_far, far_body, mparts)
    mparts = _static_loop(n_far, nch, near_body, mparts)
    m_rows = [jnp.max(mp, axis=0, keepdims=True) for mp in mparts]

    acc_ref[...] = jnp.zeros(acc_ref.shape, F32)
    ones = jnp.ones((ONES_ROWS, ch), BF16)

    def prob_chunk(c):
        rows = chunk_rows(c)
        for hh in range(N_HEADS):
            cols = slice(hh * qb, (hh + 1) * qb)
            p_ref[hh, rows, :] = jnp.exp2((x_ref[hh, rows, :] - m_rows[hh]).astype(BF16))

    def pv_chunk(c):
        rows = chunk_rows(c)
        vt2 = jnp.concatenate([vt_ref[2 * c], vt_ref[2 * c + 1]], axis=1)
        for g in range(N_KV):
            lhs = jnp.concatenate([vt2[g * HEAD_DIM:(g + 1) * HEAD_DIM, :], ones], axis=0)
            probs = jnp.concatenate([p_ref[g * KV_REP + r, rows, :] for r in range(KV_REP)], axis=1)
            acc_ref[g] += _dot(lhs, probs)

    prob_chunk(0)

    def pv_body(c, carry):
        pv_chunk(c - 1)
        prob_chunk(c)
        return carry

    _static_loop(1, nch, pv_body, 0)
    pv_chunk(nch - 1)

    for hp in range(N_HEADS // 2):
        parts = []
        for hh in (2 * hp, 2 * hp + 1):
            g, r = hh // KV_REP, hh % KV_REP
            cols = slice(r * qb, (r + 1) * qb)
            inv_l = 1.0 / acc_ref[g, HEAD_DIM:HEAD_DIM + 1, cols]
            parts.append(acc_ref[g, 0:HEAD_DIM, cols] * inv_l)
        st = jnp.concatenate(parts, axis=0)
        o_ref[:, hp * LANES:(hp + 1) * LANES] = st.T.astype(BF16)


def _attn_call(q_hm, qi_hm, wit, k2, vt, ki2, bias_tiles, batch, seq):
    t = k2.shape[0]
    nblk = seq // Q_BLOCK
    topk = min(TOPK_MAX, seq // 4)
    qrow = lambda b, i: (0, b * nblk + i, 0)
    return pl.pallas_call(
        functools.partial(_attn_kernel, topk=topk, max_chunks=seq // KEY_CHUNK),
        grid=(batch, nblk),
        in_specs=[
            pl.BlockSpec((N_HEADS, Q_BLOCK, LANES), qrow),
            pl.BlockSpec((IDX_HEADS, Q_BLOCK, IDX_DIM), qrow),
            pl.BlockSpec((IDX_HEADS, Q_BLOCK), lambda b, i: (0, b * nblk + i)),
            pl.BlockSpec((seq, LANES), lambda b, i: (b, 0)),
            pl.BlockSpec((nblk, LANES, Q_BLOCK), lambda b, i: (b, 0, 0)),
            pl.BlockSpec((seq, IDX_DIM), lambda b, i: (b, 0)),
            pl.BlockSpec((3, N_HEADS, Q_BLOCK, Q_BLOCK), lambda b, i: (0, 0, 0, 0)),
        ],
        out_specs=pl.BlockSpec((Q_BLOCK, ATTN_W), lambda b, i: (b * nblk + i, 0)),
        out_shape=jax.ShapeDtypeStruct((t, ATTN_W), BF16),
        scratch_shapes=[
            pltpu.VMEM((seq, Q_BLOCK), jnp.int32),
            pltpu.VMEM((seq, Q_BLOCK), jnp.int16),
            pltpu.VMEM((seq, Q_BLOCK), jnp.int16),
            pltpu.VMEM((seq, Q_BLOCK), F32),
            pltpu.VMEM((SUBLANES, Q_BLOCK), jnp.int32),
            pltpu.VMEM((N_HEADS, seq, Q_BLOCK), F32),
            pltpu.VMEM((N_HEADS, seq, Q_BLOCK), BF16),
            pltpu.VMEM((N_KV, HEAD_DIM + ONES_ROWS, KV_REP * Q_BLOCK), F32),
        ],
        compiler_params=_cparams("parallel", "arbitrary"),
        name="attn",
    )(q_hm, qi_hm, wit, k2, vt, ki2, bias_tiles)


def _rel_bucket(rel):
    nb = REL_BUCKETS // 2
    max_exact = nb // 2
    ret = jnp.where(rel < 0, nb, 0)
    n = jnp.abs(rel)
    nf = jnp.maximum(n, 1).astype(jnp.float32)
    large = max_exact + (jnp.log(nf / max_exact) / math.log(REL_MAX_DIST / max_exact)
                         * (nb - max_exact)).astype(jnp.int32)
    large = jnp.minimum(large, nb - 1)
    return ret + jnp.where(n < max_exact, n, large)


def _bias_tiles(rel_bias):
    s = jnp.arange(Q_BLOCK, dtype=jnp.int32)[:, None]
    tq = jnp.arange(Q_BLOCK, dtype=jnp.int32)[None, :]
    d = jnp.arange(3, dtype=jnp.int32)[:, None, None]
    bucket = _rel_bucket(d * Q_BLOCK + tq - s)
    onehot = bucket[:, None, :, :, None] == jnp.arange(REL_BUCKETS, dtype=jnp.int32)
    tiles = jnp.sum(jnp.where(onehot, rel_bias.T[None, :, None, None, :], 0.0), axis=-1)
    return ((tiles - tiles[2:3, :, :1, :1]) * LOG2E).astype(F32)


AUG_ROWS = 48


def _merge_kernel(x_ref, vc_ref, at_ref, gc_ref, ga_ref, wco_ref, wao_ref, wo_ref, g_ref, wr_ref,
                  br_ref, xo_ref, ht_ref, gone_ref):
    y_conv = _dot(vc_ref[...], wco_ref[...])
    y_attn = _dot(at_ref[...], wao_ref[...])
    merged = gc_ref[...].astype(F32) * y_conv + ga_ref[...].astype(F32) * y_attn
    x = x_ref[...] + _dot(merged.astype(BF16), wo_ref[...])
    xo_ref[...] = x
    h = x * lax.rsqrt(jnp.mean(x * x, axis=-1, keepdims=True) + EPS) * g_ref[...]
    hb = h.astype(BF16)
    d = h.shape[1]
    ht_ref[0:d, :] = hb.astype(F32).T.astype(BF16)

    logits = _dot(hb, wr_ref[...]) + br_ref[...]
    lane = lax.broadcasted_iota(jnp.int32, logits.shape, 1)
    neg_inf = -jnp.inf
    gl = jnp.where(lane < N_GROUPS, logits, neg_inf)
    gmax = jnp.max(gl, axis=-1, keepdims=True)
    grp = jnp.min(jnp.where(gl == gmax, lane, LANES), axis=-1, keepdims=True)
    p_grp = 1.0 / jnp.sum(jnp.exp(gl - gmax), axis=-1, keepdims=True)
    e_lo = ROUTER_E_OFF + grp * EXPERTS_PER_GROUP
    in_grp = jnp.logical_and(lane >= e_lo, lane < e_lo + EXPERTS_PER_GROUP)
    el = jnp.where(in_grp, logits, neg_inf)
    v1 = jnp.max(el, axis=-1, keepdims=True)
    i1 = jnp.min(jnp.where(jnp.logical_and(in_grp, el == v1), lane, LANES), axis=-1, keepdims=True)
    rest = jnp.logical_and(in_grp, lane != i1)
    el2 = jnp.where(rest, logits, neg_inf)
    v2 = jnp.max(el2, axis=-1, keepdims=True)
    i2 = jnp.min(jnp.where(jnp.logical_and(rest, el2 == v2), lane, LANES), axis=-1, keepdims=True)
    e2 = jnp.exp(v2 - v1)
    p1 = p_grp / (1.0 + e2)
    p2 = p_grp * e2 / (1.0 + e2)
    gone_ref[...] = jnp.where(lane == grp, 1.0, 0.0).astype(BF16)
    w8 = jnp.where(lane == i1 - e_lo, p1, jnp.where(lane == i2 - e_lo, p2, 0.0))
    w8t = w8.T[0:16, :]
    hi = w8t.astype(BF16)
    r1 = w8t - hi.astype(F32)
    mid = r1.astype(BF16)
    lo = (r1 - mid.astype(F32)).astype(BF16)
    ht_ref[d:d + 16, :] = hi
    ht_ref[d + 16:d + 32, :] = mid
    ht_ref[d + 32:d + 48, :] = lo


def _merge_call(x2, vc, at, gc, ga, wco, wao, wo, g, wr, br):
    t, d = x2.shape
    tm = MERGE_TM
    row = lambda i: (i, 0)
    const = lambda i: (0, 0)
    return pl.pallas_call(
        _merge_kernel,
        grid=(t // tm,),
        in_specs=[
            pl.BlockSpec((tm, d), row),
            pl.BlockSpec((tm, CONV_CH), row),
            pl.BlockSpec((tm, ATTN_W), row),
            pl.BlockSpec((tm, d), row),
            pl.BlockSpec((tm, d), row),
            pl.BlockSpec((CONV_CH, d), const),
            pl.BlockSpec((ATTN_W, d), const),
            pl.BlockSpec((d, d), const),
            pl.BlockSpec((1, d), const),
            pl.BlockSpec((d, LANES), const),
            pl.BlockSpec((1, LANES), const),
        ],
        out_specs=(pl.BlockSpec((tm, d), row), pl.BlockSpec((d + AUG_ROWS, tm), lambda i: (0, i)),
                   pl.BlockSpec((tm, LANES), row)),
        out_shape=(jax.ShapeDtypeStruct((t, d), F32),
                   jax.ShapeDtypeStruct((d + AUG_ROWS, t), BF16),
                   jax.ShapeDtypeStruct((t, LANES), BF16)),
        compiler_params=_cparams("parallel"),
        name="merge",
    )(x2, vc, at, gc, ga, wco, wao, wo, g, wr, br)


def _pack_router(w_gr, b_gr, w_er, b_er):
    d = w_gr.shape[0]
    w = jnp.zeros((d, LANES), F32)
    w = w.at[:, :N_GROUPS].set(w_gr)
    w = w.at[:, ROUTER_E_OFF:ROUTER_E_OFF + N_EXPERTS].set(
        jnp.moveaxis(w_er, 0, 1).reshape(d, N_EXPERTS))
    b = jnp.zeros((1, LANES), F32)
    b = b.at[0, :N_GROUPS].set(b_gr)
    b = b.at[0, ROUTER_E_OFF:ROUTER_E_OFF + N_EXPERTS].set(b_er.reshape(N_EXPERTS))
    return w.astype(BF16), b


SLOT_CHUNK = 256
FFN_SPLIT = 8
MOE_MAX_CHUNKS = MOE_TM // SLOT_CHUNK + N_GROUPS - 1
MOE_STEPS = MOE_MAX_CHUNKS + 1


def _plan_kernel(gone_ref, ltri_ref, utri_ref, slot_ref, cnt_ref):
    gone = gone_ref[...]
    before = _dot(ltri_ref[...], gone)
    last = gone.shape[0] - 1
    cnt = before[last:last + 1, :] + gone[last:last + 1, :].astype(F32)
    nchunks = jnp.floor((cnt + (SLOT_CHUNK - 1)) * (1.0 / SLOT_CHUNK))
    nb = jnp.broadcast_to(nchunks, (SUBLANES, LANES)).astype(BF16)
    odd = pl.program_id(0) % 2 == 1
    start = jnp.where(odd, _dot_nt(nb, utri_ref[...]), _dot(nb, utri_ref[...]))[0:1, :]
    slot = jnp.sum(gone.astype(F32) * (before + start * SLOT_CHUNK), axis=-1, keepdims=True)
    slot_ref[...] = jnp.broadcast_to(slot.astype(jnp.int32), slot_ref.shape)
    cnt_ref[0] = jnp.broadcast_to(cnt, (SUBLANES, LANES)).astype(jnp.int32)


def _plan_call(gone):
    t = gone.shape[0]
    tm = MOE_TM
    nt = t // tm
    r = lax.broadcasted_iota(jnp.int32, (tm, tm), 0)
    c = lax.broadcasted_iota(jnp.int32, (tm, tm), 1)
    ltri = (c < r).astype(BF16)
    utri = (lax.broadcasted_iota(jnp.int32, (LANES, LANES), 0)
            < lax.broadcasted_iota(jnp.int32, (LANES, LANES), 1)).astype(BF16)
    return pl.pallas_call(
        _plan_kernel,
        grid=(nt,),
        in_specs=[pl.BlockSpec((tm, LANES), lambda i: (i, 0)),
                  pl.BlockSpec((tm, tm), lambda i: (0, 0)),
                  pl.BlockSpec((LANES, LANES), lambda i: (0, 0))],
        out_specs=(pl.BlockSpec((tm, LANES), lambda i: (i, 0)),
                   pl.BlockSpec((1, SUBLANES, LANES), lambda i: (i, 0, 0))),
        out_shape=(jax.ShapeDtypeStruct((t, LANES), jnp.int32),
                   jax.ShapeDtypeStruct((nt, SUBLANES, LANES), jnp.int32)),
        compiler_params=_cparams("parallel"),
        name="moe_plan",
    )(gone, ltri, utri)


def _moe_kernel(cg_ref, nc_ref, ht_ref, slot_ref, wgu_ref, wd_ref, o_ref,
                pt_ref, acc_ref, xg_ref, cw_ref, yt_ref):
    i = pl.program_id(0)
    j = pl.program_id(1)
    nc = nc_ref[i]
    tm = slot_ref.shape[0]
    d = acc_ref.shape[1]

    def gather(jn, slot_idx):
        xa = _dot(ht_ref[...], pt_ref[jn])
        xg_ref[slot_idx] = xa[0:d, :].astype(BF16)
        cw_ref[slot_idx] = xa[d:d + 16, :] + xa[d + 16:d + 32, :] + xa[d + 32:d + 48, :]

    def scatter(jp, slot_idx):
        acc_ref[...] += _dot_nt(pt_ref[jp], yt_ref[slot_idx])

    @pl.when(j == 0)
    def _():
        acc_ref[...] = jnp.zeros(acc_ref.shape, F32)
        yt_ref[...] = jnp.zeros(yt_ref.shape, BF16)
        slot = slot_ref[...]
        lane = lax.broadcasted_iota(jnp.int32, (tm, LANES), 1)
        for jj in range(MOE_MAX_CHUNKS):
            for half in range(SLOT_CHUNK // LANES):
                off = jj * SLOT_CHUNK + half * LANES
                pt_ref[jj, :, half * LANES:(half + 1) * LANES] = jnp.where(
                    slot == lane + off, 1.0, 0.0).astype(BF16)
        gather(0, 0)

    @pl.when(j < nc)
    def _():
        cur = j % 2
        scatter(jnp.maximum(j - 1, 0), 1 - cur)
        xg = xg_ref[cur]
        cw = cw_ref[cur]
        es = EXPERTS_PER_GROUP // FFN_SPLIT
        sf = es * EXPERT_FF
        yt = None
        for s in range(FFN_SPLIT):
            ab = _dot(wgu_ref[0, 2 * sf * s:2 * sf * (s + 1), :], xg)
            mids = []
            for e in range(es):
                a = ab[e * EXPERT_FF:(e + 1) * EXPERT_FF, :]
                b = ab[sf + e * EXPERT_FF:sf + (e + 1) * EXPERT_FF, :]
                c = cw[s * es + e:s * es + e + 1, :]
                mids.append((a * _sigmoid(a) * b * c).astype(BF16))
            part = _dot(wd_ref[0, :, sf * s:sf * (s + 1)], jnp.concatenate(mids, axis=0))
            yt = part if yt is None else yt + part
        yt_ref[cur] = yt.astype(BF16)
        gather(jnp.minimum(j + 1, MOE_MAX_CHUNKS - 1), 1 - cur)

    @pl.when(j == nc)
    def _():
        scatter(nc - 1, (nc - 1) % 2)

    @pl.when(j == pl.num_programs(1) - 1)
    def _():
        o_ref[...] = acc_ref[...].astype(BF16)


def _moe_call(ht_aug, slot, chunk_group, n_chunks, wgu_t, wd_t):
    da, t = ht_aug.shape
    d = da - AUG_ROWS
    tm = MOE_TM
    nt = t // tm
    ff = EXPERTS_PER_GROUP * EXPERT_FF
    wmap = lambda i, j, cg, nc: (cg[i * MOE_STEPS + j], 0, 0)
    grid_spec = pltpu.PrefetchScalarGridSpec(
        num_scalar_prefetch=2,
        grid=(nt, MOE_STEPS),
        in_specs=[
            pl.BlockSpec((da, tm), lambda i, j, cg, nc: (0, i)),
            pl.BlockSpec((tm, LANES), lambda i, j, cg, nc: (i, 0)),
            pl.BlockSpec((1, 2 * ff, d), wmap),
            pl.BlockSpec((1, d, ff), wmap),
        ],
        out_specs=pl.BlockSpec((tm, d), lambda i, j, cg, nc: (i, 0)),
        scratch_shapes=[pltpu.VMEM((MOE_MAX_CHUNKS, tm, SLOT_CHUNK), BF16),
                        pltpu.VMEM((tm, d), F32),
                        pltpu.VMEM((2, d, SLOT_CHUNK), BF16),
                        pltpu.VMEM((2, 16, SLOT_CHUNK), F32),
                        pltpu.VMEM((2, d, SLOT_CHUNK), BF16)],
    )
    return pl.pallas_call(
        _moe_kernel,
        grid_spec=grid_spec,
        out_shape=jax.ShapeDtypeStruct((t, d), BF16),
        compiler_params=_cparams("parallel", "arbitrary"),
        name="moe",
    )(chunk_group, n_chunks, ht_aug, slot, wgu_t, wd_t)


def _chunk_tables(cnt):
    per_group = (cnt[:, 0, :N_GROUPS] + SLOT_CHUNK - 1) // SLOT_CHUNK
    odd = (jnp.arange(per_group.shape[0]) % 2 == 1)[:, None]
    ends = jnp.cumsum(jnp.where(odd, per_group[:, ::-1], per_group), axis=1)
    n_chunks = ends[:, -1]
    j = jnp.arange(MOE_STEPS, dtype=jnp.int32)[None, :]
    jj = jnp.minimum(j, n_chunks[:, None] - 1)
    rank = jnp.sum((jj[:, :, None] >= ends[:, None, :]).astype(jnp.int32), axis=-1)
    group = jnp.where(odd, N_GROUPS - 1 - rank, rank)
    return group.reshape(-1).astype(jnp.int32), n_chunks.astype(jnp.int32)


def _pack_expert_weights(w_gate, w_up, w_down):
    ne, d, f = w_gate.shape
    tr = lambda w: jnp.swapaxes(w, 1, 2).reshape(N_GROUPS, FFN_SPLIT, -1, d)
    wgu = jnp.concatenate([tr(w_gate), tr(w_up)], axis=2).reshape(
        N_GROUPS, 2 * EXPERTS_PER_GROUP * f, d).astype(BF16)
    wd = jnp.swapaxes(w_down.reshape(N_GROUPS, EXPERTS_PER_GROUP * f, d), 1, 2).astype(BF16)
    return wgu, wd


def _norm_kernel(x_ref, y_ref, g_ref, o_ref):
    x = x_ref[...] + y_ref[...].astype(F32)
    o_ref[...] = x * lax.rsqrt(jnp.mean(x * x, axis=-1, keepdims=True) + EPS) * g_ref[...]


def _norm_call(x2, y2, g):
    t, d = x2.shape
    tm = NORM_TM
    row = lambda i: (i, 0)
    return pl.pallas_call(
        _norm_kernel,
        grid=(t // tm,),
        in_specs=[pl.BlockSpec((tm, d), row), pl.BlockSpec((tm, d), row),
                  pl.BlockSpec((1, d), lambda i: (0, 0))],
        out_specs=pl.BlockSpec((tm, d), row),
        out_shape=jax.ShapeDtypeStruct((t, d), F32),
        compiler_params=_cparams("parallel"),
        name="final_norm",
    )(x2, y2, g)


def kernel(x, g_mix, w_in, conv_w, conv_b, conv_ln_g, conv_ln_b, w_conv_out, w_attn_out, w_out, rel_bias, g_ffn, w_group_router, b_group_router, w_expert_router, b_expert_router, w_e_gate, w_e_up, w_e_down, g_final):
    batch, seq, d = x.shape
    depth = g_mix.shape[0]
    t = batch * seq
    assert seq % CONV_TL == 0 and seq % KEY_CHUNK == 0 and t % MOE_TM == 0
    x2 = x.reshape(t, d)
    y2 = None
    bias_tiles = _bias_tiles(rel_bias)
    for l in range(depth):
        w_pack = _pack_proj_weights(w_in[l])
        outs = _proj_call(x2, y2, g_mix[l][None, :], w_pack)
        if y2 is not None:
            x2, outs = outs[0], outs[1:]
        u, q_hm, k2, vt, qi_hm, ki2, wit, gc, ga = outs
        vc = _conv_call(u.reshape(batch, seq, 2 * CONV_CH), conv_w[l], conv_b[l][None, :],
                        conv_ln_g[l][None, :], conv_ln_b[l][None, :]).reshape(t, CONV_CH)
        at = _attn_call(q_hm, qi_hm, wit, k2, vt, ki2, bias_tiles, batch, seq)
        wr, br = _pack_router(w_group_router[l], b_group_router[l], w_expert_router[l],
                              b_expert_router[l])
        x2, ht_aug, gone = _merge_call(x2, vc, at, gc, ga, w_conv_out[l].astype(BF16),
                                       w_attn_out[l].astype(BF16), w_out[l].astype(BF16),
                                       g_ffn[l][None, :], wr, br)
        slot, cnt = _plan_call(gone)
        chunk_group, n_chunks = _chunk_tables(cnt)
        wgu_t, wd_t = _pack_expert_weights(w_e_gate[l], w_e_up[l], w_e_down[l])
        y2 = _moe_call(ht_aug, slot, chunk_group, n_chunks, wgu_t, wd_t)
    return _norm_call(x2, y2, g_final[None, :]).reshape(batch, seq, d)
```
